```python
import math
import jax, jax.numpy as jnp
from jax import lax
import numpy as np

D_MODEL = 1024
BATCH = 32
SEQ = 2048
DEPTH = 1
DEC_BATCH = 128
DEC_SEQ = 8
PAST_LEN = 8192
PAGE_SIZE = 128

HEAD_DIM = 64
N_ATTN_HEADS = 12
ATTN_WIDTH = N_ATTN_HEADS * HEAD_DIM
CONV_WIDTH = D_MODEL - ATTN_WIDTH
CONV_K = 3
BRANCHES = ((128, 1), (512, 4), (2048, 16))
WINDOW_MAX = 2048
BLK = 128
ROPE_THETA = 10000.0
NORM_EPS = 1e-6
IN_WIDTH = 3 * ATTN_WIDTH + 3 * CONV_WIDTH
PEER_HEADS = 8
PEER_NKEYS = 128
PEER_EXPERTS = PEER_NKEYS * PEER_NKEYS
PEER_QDIM = 256
PEER_TOPK = 16
PEER_CHUNK = 256
NEG = -1e30

kernel_name = 'hymba_dilated_conv_peer_step'


def rms_norm(x, g):
    xf = x.astype(jnp.float32)
    y = xf * lax.rsqrt(jnp.mean(xf * xf, axis=-1, keepdims=True) + NORM_EPS)
    return (y * g.astype(jnp.float32)).astype(x.dtype)


def rope(x, pos):
    half = HEAD_DIM // 2
    inv = jnp.exp(-math.log(ROPE_THETA) * jnp.arange(half, dtype=jnp.float32) * (2.0 / HEAD_DIM))
    ang = pos.astype(jnp.float32)[:, None] * inv[None, :]
    cos = jnp.cos(ang)[None, :, None, :]
    sin = jnp.sin(ang)[None, :, None, :]
    xf = x.astype(jnp.float32)
    x1, x2 = xf[..., :half], xf[..., half:]
    return jnp.concatenate([x1 * cos - x2 * sin, x2 * cos + x1 * sin], axis=-1).astype(x.dtype)


def softmax_parts(s):
    m = jnp.max(s, axis=-1, keepdims=True)
    p = jnp.exp(s - m)
    den = jnp.sum(p, axis=-1, keepdims=True)
    return p, den, m[..., 0] + jnp.log(den[..., 0])


def dilated_branch_prompt(q, k, v, window, dil):
    B, S, H, Dh = q.shape
    L = S // dil
    nb = -(-L // BLK)
    Lp = nb * BLK
    span = window // dil
    scale = 1.0 / math.sqrt(Dh)

    def to_sub(t):
        t = t.reshape(B, L, dil, H, Dh).transpose(0, 2, 1, 3, 4)
        t = jnp.pad(t, ((0, 0), (0, 0), (0, Lp - L), (0, 0), (0, 0)))
        return t.reshape(B, dil, nb, BLK, H, Dh)

    def with_prev(t):
        prev = jnp.pad(t, ((0, 0), (0, 0), (1, 0), (0, 0), (0, 0), (0, 0)))[:, :, :-1]
        return jnp.concatenate([prev, t], axis=3)

    qs = to_sub(q).astype(jnp.float32)
    kb = with_prev(to_sub(k)).astype(jnp.float32)
    vb = with_prev(to_sub(v)).astype(jnp.float32)
    s = jnp.einsum('brnqhd,brnkhd->brnhqk', qs, kb) * scale
    qi = jnp.arange(BLK)[:, None]
    kj = jnp.arange(2 * BLK)[None, :]
    dist = qi + BLK - kj
    kidx = jnp.arange(nb)[:, None, None] * BLK + kj[None] - BLK
    valid = (dist >= 0)[None] & (dist <= span)[None] & (kidx >= 0)
    s = jnp.where(valid[None, None, :, None], s, NEG)
    p, den, lse = softmax_parts(s)
    o = jnp.einsum('brnhqk,brnkhd->brnhqd', p, vb) / den
    o = o.transpose(0, 1, 2, 4, 3, 5).reshape(B, dil, Lp, H, Dh)[:, :, :L]
    o = o.transpose(0, 2, 1, 3, 4).reshape(B, S, H, Dh)
    lse = lse.transpose(0, 1, 2, 4, 3).reshape(B, dil, Lp, H)[:, :, :L]
    lse = lse.transpose(0, 2, 1, 3).reshape(B, S, H)
    return o, lse


def dilated_branch_sample(q, kcat, vcat, n_past, window, dil):
    T = q.shape[1]
    offs = jnp.arange(window // dil + 1) * dil
    idx = n_past + jnp.arange(T)[:, None] - offs[None, :]
    valid = idx >= 0
    idxc = jnp.maximum(idx, 0)
    kg = jnp.take(kcat, idxc, axis=1).astype(jnp.float32)
    vg = jnp.take(vcat, idxc, axis=1).astype(jnp.float32)
    s = jnp.einsum('bthd,btkhd->bthk', q.astype(jnp.float32), kg) * (1.0 / math.sqrt(HEAD_DIM))
    s = jnp.where(valid[None, :, None, :], s, NEG)
    p, den, lse = softmax_parts(s)
    o = jnp.einsum('bthk,btkhd->bthd', p, vg) / den
    return o, lse


def merge_branches(parts):
    lse = jnp.stack([l for _, l in parts], axis=0)
    wts = jax.nn.softmax(lse, axis=0)
    outs = jnp.stack([o for o, _ in parts], axis=0)
    return jnp.einsum('rbth,rbthd->bthd', wts, outs)


def short_conv(cb, cc, ch, conv_state, conv_w):
    z = cc * ch
    T = z.shape[1]
    zp = jnp.concatenate([conv_state.astype(z.dtype), z], axis=1)
    y = conv_w[0] * zp[:, 0:T]
    for i in range(1, CONV_K):
        y = y + conv_w[i] * zp[:, i:i + T]
    return cb * y, zp[:, -(CONV_K - 1):]


def peer_ffn(xn, wq, sub_keys, u, v):
    B, T, D = xn.shape
    N = B * T
    nch = -(-N // PEER_CHUNK)
    flat = jnp.pad(xn.reshape(N, D), ((0, nch * PEER_CHUNK - N), (0, 0)))

    def one(xc):
        xf = xc.astype(jnp.float32)
        q = (xc @ wq).astype(jnp.float32).reshape(-1, PEER_HEADS, 2, PEER_QDIM // 2)
        s = jnp.einsum('nhpc,hpkc->nhpk', q, sub_keys.astype(jnp.float32))
        sv, si = lax.top_k(s, PEER_TOPK)
        cand = (sv[:, :, 0, :, None] + sv[:, :, 1, None, :]).reshape(-1, PEER_HEADS, PEER_TOPK * PEER_TOPK)
        cid = (si[:, :, 0, :, None] * PEER_NKEYS + si[:, :, 1, None, :]).reshape(-1, PEER_HEADS, PEER_TOPK * PEER_TOPK)
        top, pos = lax.top_k(cand, PEER_TOPK)
        eid = jnp.take_along_axis(cid, pos, axis=-1)
        g = jax.nn.softmax(top, axis=-1)
        ue = u[eid].astype(jnp.float32)
        ve = v[eid].astype(jnp.float32)
        act = jax.nn.gelu(jnp.einsum('nd,nhkd->nhk', xf, ue), approximate=False)
        return jnp.einsum('nhk,nhkd->nd', g * act, ve).astype(xc.dtype)

    out = lax.map(one, flat.reshape(nch, PEER_CHUNK, D)).reshape(-1, D)[:N]
    return out.reshape(B, T, D)


def hybrid_layer(h, pos, conv_state, attend, norm1_g, w_in, conv_w, attn_out_g, conv_out_g,
                 w_out, norm2_g, peer_wq, peer_sub_keys, peer_u, peer_v):
    B, T, _ = h.shape
    xn = rms_norm(h, norm1_g)
    proj = xn @ w_in
    q, k, vv = [proj[..., i * ATTN_WIDTH:(i + 1) * ATTN_WIDTH].reshape(B, T, N_ATTN_HEADS, HEAD_DIM)
                for i in range(3)]
    c0 = 3 * ATTN_WIDTH
    cb, cc, ch = [proj[..., c0 + i * CONV_WIDTH:c0 + (i + 1) * CONV_WIDTH] for i in range(3)]
    q = rope(q, pos)
    k = rope(k, pos)
    attn = attend(q, k, vv).reshape(B, T, ATTN_WIDTH).astype(h.dtype)
    conv_out, conv_new = short_conv(cb, cc, ch, conv_state, conv_w)
    mixed = jnp.concatenate([rms_norm(attn, attn_out_g), rms_norm(conv_out, conv_out_g)], axis=-1) @ w_out
    h = h + mixed
    h = h + peer_ffn(rms_norm(h, norm2_g), peer_wq, peer_sub_keys, peer_u, peer_v)
    return h, k, vv, conv_new


def setup_inputs(seed: int = 0) -> dict:
    key = jax.random.key(seed)
    ks = jax.random.split(key, 20)
    f32 = jnp.float32
    win_s = min(WINDOW_MAX, PAST_LEN)
    nrm = lambda k, shape, sc: jax.random.normal(k, shape, f32) * sc
    return {
        'x_prompt': nrm(ks[0], (BATCH, SEQ, D_MODEL), 1.0),
        'x_sample': nrm(ks[1], (DEC_BATCH, DEC_SEQ, D_MODEL), 1.0),
        'cache_win_k': nrm(ks[2], (DEPTH, DEC_BATCH, win_s, N_ATTN_HEADS, HEAD_DIM), 1.0),
        'cache_win_v': nrm(ks[3], (DEPTH, DEC_BATCH, win_s, N_ATTN_HEADS, HEAD_DIM), 1.0),
        'state_conv': nrm(ks[4], (DEPTH, DEC_BATCH, CONV_K - 1, CONV_WIDTH), 1.0),
        'norm1_g': 1.0 + nrm(ks[5], (DEPTH, D_MODEL), 0.01),
        'w_in': nrm(ks[6], (DEPTH, D_MODEL, IN_WIDTH), D_MODEL ** -0.5),
        'conv_w': nrm(ks[7], (DEPTH, CONV_K, CONV_WIDTH), CONV_K ** -0.5),
        'attn_out_g': 1.0 + nrm(ks[8], (DEPTH, ATTN_WIDTH), 0.01),
        'conv_out_g': 1.0 + nrm(ks[9], (DEPTH, CONV_WIDTH), 0.01),
        'w_out': nrm(ks[10], (DEPTH, D_MODEL, D_MODEL), D_MODEL ** -0.5),
        'norm2_g': 1.0 + nrm(ks[11], (DEPTH, D_MODEL), 0.01),
        'peer_wq': nrm(ks[12], (DEPTH, D_MODEL, PEER_HEADS * PEER_QDIM), D_MODEL ** -0.5),
        'peer_sub_keys': nrm(ks[13], (DEPTH, PEER_HEADS, 2, PEER_NKEYS, PEER_QDIM // 2), (PEER_QDIM // 2) ** -0.5),
        'peer_u': nrm(ks[14], (DEPTH, PEER_EXPERTS, D_MODEL), D_MODEL ** -0.5),
        'peer_v': nrm(ks[15], (DEPTH, PEER_EXPERTS, D_MODEL), PEER_HEADS ** -0.5),
        'final_norm_g': 1.0 + nrm(ks[16], (D_MODEL,), 0.01),
    }


def reference(x_prompt, x_sample, cache_win_k, cache_win_v, state_conv, norm1_g, w_in, conv_w,
              attn_out_g, conv_out_g, w_out, norm2_g, peer_wq, peer_sub_keys, peer_u, peer_v,
              final_norm_g):
    S = x_prompt.shape[1]
    T = x_sample.shape[1]
    n_past = cache_win_k.shape[2]
    win_p = min(WINDOW_MAX, S)
    pos_p = jnp.arange(S, dtype=jnp.int32)
    pos_s = PAST_LEN + jnp.arange(T, dtype=jnp.int32)

    def attend_prompt(q, k, v):
        return merge_branches([dilated_branch_prompt(q, k, v, w, d) for w, d in BRANCHES])

    hp, hs = x_prompt, x_sample
    kp_l, vp_l, cp_l, ks_l, vs_l, cs_l = [], [], [], [], [], []
    for l in range(DEPTH):
        params = (norm1_g[l], w_in[l], conv_w[l], attn_out_g[l], conv_out_g[l], w_out[l],
                  norm2_g[l], peer_wq[l], peer_sub_keys[l], peer_u[l], peer_v[l])
        conv0 = jnp.zeros((hp.shape[0], CONV_K - 1, CONV_WIDTH), hp.dtype)
        hp, k_p, v_p, c_p = hybrid_layer(hp, pos_p, conv0, attend_prompt, *params)
        kp_l.append(k_p[:, S - win_p:])
        vp_l.append(v_p[:, S - win_p:])
        cp_l.append(c_p)
        kc, vc = cache_win_k[l], cache_win_v[l]

        def attend_sample(q, k, v, kc=kc, vc=vc):
            kcat = jnp.concatenate([kc.astype(k.dtype), k], axis=1)
            vcat = jnp.concatenate([vc.astype(v.dtype), v], axis=1)
            return merge_branches([dilated_branch_sample(q, kcat, vcat, n_past, w, d) for w, d in BRANCHES])

        hs, k_s, v_s, c_s = hybrid_layer(hs, pos_s, state_conv[l], attend_sample, *params)
        ks_l.append(k_s)
        vs_l.append(v_s)
        cs_l.append(c_s)
    y_prompt = rms_norm(hp, final_norm_g)
    y_sample = rms_norm(hs, final_norm_g)
    return (y_prompt, y_sample, jnp.stack(kp_l), jnp.stack(vp_l), jnp.stack(cp_l),
            jnp.stack(ks_l), jnp.stack(vs_l), jnp.stack(cs_l))
```

```python
import functools
import math

import jax
import jax.numpy as jnp
from jax import lax
from jax.experimental import pallas as pl
from jax.experimental.pallas import tpu as pltpu

F32 = jnp.float32
BF16 = jnp.bfloat16

HEAD_DIM = 64
N_HEADS = 12
ATTN_W = N_HEADS * HEAD_DIM
CONV_W = 256
CONV_K = 3
BRANCHES = ((128, 1), (512, 4), (2048, 16))
ROPE_THETA = 10000.0
EPS = 1e-6
PEER_HEADS = 8
PEER_NKEYS = 128
PEER_TOPK = 16
NEG = -1e30
LANES = 128
VMEM_LIMIT = 56 * 1024 * 1024

_NT = (((1,), (1,)), ((), ()))


def _rms(x, g):
    return x * lax.rsqrt(jnp.mean(x * x, axis=-1, keepdims=True) + EPS) * g


def _params(sem):
    return pltpu.CompilerParams(dimension_semantics=sem, vmem_limit_bytes=VMEM_LIMIT)


def _inproj_kernel(x_ref, g_ref, w_ref, cos_ref, sin_ref,
                   q_ref, k_ref, v_ref, cb_ref, z_ref):
    xb = _rms(x_ref[...], g_ref[...]).astype(BF16)
    tm = xb.shape[0]
    lane = lax.broadcasted_iota(jnp.int32, (tm, LANES), 1)
    low_half = (lane % HEAD_DIM) < (HEAD_DIM // 2)
    cos = cos_ref[...]
    sin = sin_ref[...]

    def rope(t):
        partner = jnp.where(low_half, pltpu.roll(t, LANES - HEAD_DIM // 2, 1),
                            pltpu.roll(t, HEAD_DIM // 2, 1))
        return t * cos + partner * sin

    scale = 1.0 / math.sqrt(HEAD_DIM)
    q = jnp.dot(xb, w_ref[:, 0:ATTN_W], preferred_element_type=F32)
    for c in range(ATTN_W // LANES):
        sl = slice(c * LANES, (c + 1) * LANES)
        q_ref[:, sl] = (rope(q[:, sl]) * scale).astype(BF16)
    k = jnp.dot(xb, w_ref[:, ATTN_W:2 * ATTN_W], preferred_element_type=F32)
    for c in range(ATTN_W // LANES):
        sl = slice(c * LANES, (c + 1) * LANES)
        k_ref[:, sl] = rope(k[:, sl])
    v_ref[...] = jnp.dot(xb, w_ref[:, 2 * ATTN_W:3 * ATTN_W], preferred_element_type=F32)
    c0 = 3 * ATTN_W
    conv = jnp.dot(xb, w_ref[:, c0:c0 + 3 * CONV_W], preferred_element_type=F32)
    cb_ref[...] = conv[:, 0:CONV_W]
    z_ref[...] = conv[:, CONV_W:2 * CONV_W] * conv[:, 2 * CONV_W:3 * CONV_W]


def _inproj(x, g, w_bf, cos, sin, tm):
    n, d = x.shape
    win = w_bf.shape[1]
    period = cos.shape[0] // tm
    row = lambda i: (i, 0)
    const = lambda i: (0, 0)
    return pl.pallas_call(
        _inproj_kernel,
        grid=(n // tm,),
        in_specs=[pl.BlockSpec((tm, d), row),
                  pl.BlockSpec((1, d), const),
                  pl.BlockSpec((d, win), const),
                  pl.BlockSpec((tm, LANES), lambda i: (i % period, 0)),
                  pl.BlockSpec((tm, LANES), lambda i: (i % period, 0))],
        out_specs=[pl.BlockSpec((tm, ATTN_W), row), pl.BlockSpec((tm, ATTN_W), row),
                   pl.BlockSpec((tm, ATTN_W), row), pl.BlockSpec((tm, CONV_W), row),
                   pl.BlockSpec((tm, CONV_W), row)],
        out_shape=[jax.ShapeDtypeStruct((n, ATTN_W), BF16),
                   jax.ShapeDtypeStruct((n, ATTN_W), F32),
                   jax.ShapeDtypeStruct((n, ATTN_W), F32),
                   jax.ShapeDtypeStruct((n, CONV_W), F32),
                   jax.ShapeDtypeStruct((n, CONV_W), F32)],
        compiler_params=_params(("parallel",)),
        name="inproj",
    )(x, g, w_bf, cos, sin)


def _rope_tables(pos):
    half = HEAD_DIM // 2
    inv = jnp.exp(-math.log(ROPE_THETA) * jnp.arange(half, dtype=F32) * (2.0 / HEAD_DIM))
    ang = pos.astype(F32)[:, None] * inv[None, :]
    cos = jnp.cos(ang)
    sin = jnp.sin(ang)
    reps = LANES // HEAD_DIM
    cos_t = jnp.tile(jnp.concatenate([cos, cos], axis=1), (1, reps))
    sin_t = jnp.tile(jnp.concatenate([-sin, sin], axis=1), (1, reps))
    return cos_t, sin_t


def _branch_count(dist):
    cnt = jnp.zeros(dist.shape, F32)
    for window, dil in BRANCHES:
        hit = (dist <= window) & ((dist & (dil - 1)) == 0)
        cnt = cnt + jnp.where(hit, 1.0, 0.0)
    return jnp.where(dist >= 0, cnt, 0.0)


def _attn_prompt_kernel(q_ref, k_ref, v_ref, o_ref, kb_ref, vb_ref, *, tq):
    qi = pl.program_id(2)

    @pl.when(qi == 0)
    def _():
        kb_ref[...] = k_ref[...].astype(BF16)
        vb_ref[...] = v_ref[...].astype(BF16)

    q = q_ref[...]
    row = lax.broadcasted_iota(jnp.int32, (tq, tq), 0)
    col = lax.broadcasted_iota(jnp.int32, (tq, tq), 1)
    heads = LANES // HEAD_DIM

    def body(j, carry):
        cnt = _branch_count((qi - j) * tq + row - col)
        live = cnt > 0.0
        start = pl.multiple_of(j * tq, tq)
        kblk = kb_ref[pl.ds(start, tq), :]
        vblk = vb_ref[pl.ds(start, tq), :]
        out = []
        for h in range(heads):
            m, l, acc = carry[h]
            sl = slice(h * HEAD_DIM, (h + 1) * HEAD_DIM)
            s = lax.dot_general(q[:, sl], kblk[:, sl], _NT, preferred_element_type=F32)
            s = jnp.where(live, s, NEG)
            m_new = jnp.maximum(m, jnp.max(s, axis=-1, keepdims=True))
            alpha = jnp.exp(m - m_new)
            p = cnt * jnp.exp(s - m_new)
            l_new = alpha * l + jnp.sum(p, axis=-1, keepdims=True)
            acc_new = alpha * acc + jnp.dot(p.astype(BF16), vblk[:, sl],
                                            preferred_element_type=F32)
            out.append((m_new, l_new, acc_new))
        return tuple(out)

    init = tuple((jnp.full((tq, 1), NEG, F32), jnp.zeros((tq, 1), F32),
                  jnp.zeros((tq, HEAD_DIM), F32)) for _ in range(heads))
    res = lax.fori_loop(0, qi + 1, body, init)
    o_ref[...] = jnp.concatenate([acc / l for (_, l, acc) in res], axis=-1)


def _attn_prompt(q, k, v, tq):
    b, s, _ = q.shape
    blk = lambda bi, hp, qi: (bi, qi, hp)
    full = lambda bi, hp, qi: (bi, 0, hp)
    return pl.pallas_call(
        functools.partial(_attn_prompt_kernel, tq=tq),
        grid=(b, ATTN_W // LANES, s // tq),
        in_specs=[pl.BlockSpec((None, tq, LANES), blk),
                  pl.BlockSpec((None, s, LANES), full),
                  pl.BlockSpec((None, s, LANES), full)],
        out_specs=pl.BlockSpec((None, tq, LANES), blk),
        out_shape=jax.ShapeDtypeStruct((b, s, ATTN_W), F32),
        scratch_shapes=[pltpu.VMEM((s, LANES), BF16), pltpu.VMEM((s, LANES), BF16)],
        compiler_params=_params(("parallel", "parallel", "arbitrary")),
        name="attn_prompt",
    )(q, k, v)


def _attn_sample_kernel(q_ref, kn_ref, vn_ref, kc_ref, vc_ref, o_ref, *, chunk):
    t = q_ref.shape[0]
    n_past = kc_ref.shape[0]
    rows = N_HEADS * t
    qt = jnp.concatenate([q_ref[...]] * N_HEADS, axis=0)
    r_id = lax.broadcasted_iota(jnp.int32, (rows, ATTN_W), 0)
    c_id = lax.broadcasted_iota(jnp.int32, (rows, ATTN_W), 1)
    own = (r_id // t) == (c_id // HEAD_DIM)
    qbd = jnp.where(own, qt, jnp.zeros_like(qt))

    def scores(kblk, first_key):
        nk = kblk.shape[0]
        s = lax.dot_general(qbd, kblk.astype(BF16), _NT, preferred_element_type=F32)
        qpos = n_past + lax.broadcasted_iota(jnp.int32, (rows, nk), 0) % t
        kpos = first_key + lax.broadcasted_iota(jnp.int32, (rows, nk), 1)
        cnt = _branch_count(qpos - kpos)
        return jnp.where(cnt > 0.0, s, NEG), cnt

    parts = [scores(kc_ref[c * chunk:(c + 1) * chunk, :], c * chunk)
             for c in range(n_past // chunk)]
    parts.append(scores(kn_ref[...], n_past))
    m = functools.reduce(jnp.maximum, [jnp.max(s, axis=-1, keepdims=True) for s, _ in parts])
    den = jnp.zeros((rows, 1), F32)
    acc = jnp.zeros((rows, ATTN_W), F32)
    for c, (s, cnt) in enumerate(parts):
        p = cnt * jnp.exp(s - m)
        den = den + jnp.sum(p, axis=-1, keepdims=True)
        vblk = vn_ref[...] if c == len(parts) - 1 else vc_ref[c * chunk:(c + 1) * chunk, :]
        acc = acc + jnp.dot(p.astype(BF16), vblk.astype(BF16), preferred_element_type=F32)
    o_full = jnp.where(own, acc / den, 0.0)
    out = o_full[0:t, :]
    for h in range(1, N_HEADS):
        out = out + o_full[h * t:(h + 1) * t, :]
    o_ref[...] = out


def _attn_sample(q, k_new, v_new, k_cache, v_cache):
    b, t, _ = q.shape
    n_past = k_cache.shape[1]
    new = lambda bi: (bi, 0, 0)
    return pl.pallas_call(
        functools.partial(_attn_sample_kernel, chunk=512),
        grid=(b,),
        in_specs=[pl.BlockSpec((None, t, ATTN_W), new),
                  pl.BlockSpec((None, t, ATTN_W), new),
                  pl.BlockSpec((None, t, ATTN_W), new),
                  pl.BlockSpec((None, n_past, ATTN_W), new),
                  pl.BlockSpec((None, n_past, ATTN_W), new)],
        out_specs=pl.BlockSpec((None, t, ATTN_W), new),
        out_shape=jax.ShapeDtypeStruct((b, t, ATTN_W), F32),
        compiler_params=_params(("parallel",)),
        name="attn_sample",
    )(q, k_new, v_new, k_cache, v_cache)


def _mix_kernel(attn_ref, cb_ref, z_ref, zprev_ref, state_ref, x_ref, cw_ref, ga_ref, gc_ref,
                wo_ref, g2_ref, h_ref, xn_ref, zbuf, *, stride, tiles_per_seq):
    tm = z_ref.shape[0]
    hb = state_ref.shape[0]
    if tiles_per_seq > 1:
        first = (pl.program_id(0) % tiles_per_seq) == 0
        zbuf[0:hb, :] = jnp.where(first, state_ref[...], zprev_ref[...])
    else:
        zbuf[0:hb, :] = state_ref[...]
    z = z_ref[...]
    zbuf[hb:hb + tm, :] = z
    cw = cw_ref[...]
    y = (cw[0:1, :] * zbuf[hb - 2 * stride:hb - 2 * stride + tm, :]
         + cw[1:2, :] * zbuf[hb - stride:hb - stride + tm, :]
         + cw[2:3, :] * z)
    conv_out = cb_ref[...] * y
    a = _rms(attn_ref[...], ga_ref[...]).astype(BF16)
    c = _rms(conv_out, gc_ref[...]).astype(BF16)
    mixed = (jnp.dot(a, wo_ref[0:ATTN_W, :], preferred_element_type=F32)
             + jnp.dot(c, wo_ref[ATTN_W:ATTN_W + CONV_W, :], preferred_element_type=F32))
    h = x_ref[...] + mixed
    h_ref[...] = h
    xn_ref[...] = _rms(h, g2_ref[...]).astype(BF16)


def _mix(attn, cb, z, state, x, conv_w, ga, gc, wo_bf, g2, tm, stride, tiles_per_seq):
    n, d = x.shape
    hb = state.shape[1]
    row = lambda i: (i, 0)
    const = lambda i: (0, 0)
    prev = lambda i: (jnp.maximum(i * (tm // hb) - 1, 0), 0)
    return pl.pallas_call(
        functools.partial(_mix_kernel, stride=stride, tiles_per_seq=tiles_per_seq),
        grid=(n // tm,),
        in_specs=[pl.BlockSpec((tm, ATTN_W), row),
                  pl.BlockSpec((tm, CONV_W), row),
                  pl.BlockSpec((tm, CONV_W), row),
                  pl.BlockSpec((hb, CONV_W), prev),
                  pl.BlockSpec((None, hb, CONV_W), lambda i: (i // tiles_per_seq, 0, 0)),
                  pl.BlockSpec((tm, d), row),
                  pl.BlockSpec((CONV_K, CONV_W), const),
                  pl.BlockSpec((1, ATTN_W), const),
                  pl.BlockSpec((1, CONV_W), const),
                  pl.BlockSpec((d, d), const),
                  pl.BlockSpec((1, d), const)],
        out_specs=[pl.BlockSpec((tm, d), row), pl.BlockSpec((tm, d), row)],
        out_shape=[jax.ShapeDtypeStruct((n, d), F32), jax.ShapeDtypeStruct((n, d), BF16)],
        scratch_shapes=[pltpu.VMEM((hb + tm, CONV_W), F32)],
        compiler_params=_params(("parallel",)),
        name="mix",
    )(attn, cb, z, z, state, x, conv_w, ga, gc, wo_bf, g2)


def _top16(s):
    n = s.shape[0]
    idx = lax.broadcasted_iota(jnp.int32, s.shape, 0)
    rank = jnp.full(s.shape, float(PEER_TOPK), F32)
    vals = []
    for r in range(PEER_TOPK):
        m = jnp.max(s, axis=0, keepdims=True)
        pick = jnp.min(jnp.where(s == m, idx, n), axis=0, keepdims=True)
        sel = idx == pick
        rank = jnp.where(sel, float(r), rank)
        s = jnp.where(sel, -jnp.inf, s)
        vals.append(m)
    return rank, jnp.concatenate(vals, axis=0)


def _route_kernel(xn_ref, wq_ref, keys_ref, r2_ref, a2_ref, lb_ref, a1_ref, q_scr):
    h = pl.program_id(1)
    nsub = 2 * PEER_HEADS

    @pl.when(h == 0)
    def _():
        q = jnp.dot(xn_ref[...], wq_ref[...], preferred_element_type=F32)
        for c in range(nsub):
            q_scr[c] = q[:, c * PEER_NKEYS:(c + 1) * PEER_NKEYS].astype(BF16)

    s1 = lax.dot_general(keys_ref[0], q_scr[2 * h], _NT, preferred_element_type=F32)
    s2 = lax.dot_general(keys_ref[1], q_scr[2 * h + 1], _NT, preferred_element_type=F32)
    rank1, sv1 = _top16(s1)
    rank2, sv2 = _top16(s2)
    cand = jnp.concatenate([sv1[a:a + 1, :] + sv2 for a in range(PEER_TOPK)], axis=0)
    crank, _ = _top16(cand)
    chosen = crank < float(PEER_TOPK)
    cmax = sv1[0:1, :] + sv2[0:1, :]
    zsum = jnp.sum(jnp.where(chosen, jnp.exp(cand - cmax), 0.0), axis=0, keepdims=True)
    picked = jnp.where(chosen, 1.0, 0.0)
    lb = jnp.zeros(rank1.shape, F32)
    for a in range(PEER_TOPK):
        n_a = jnp.sum(picked[a * PEER_TOPK:(a + 1) * PEER_TOPK, :], axis=0, keepdims=True)
        lb = jnp.where(rank1 == float(a), n_a, lb)
    r2_ref[...] = rank2
    a2_ref[...] = jnp.exp(s2 - sv2[0:1, :])
    lb_ref[...] = lb
    a1_ref[...] = jnp.exp(s1 - sv1[0:1, :]) / zsum


def _route(xn, wq_bf, keys_bf, tm):
    n, d = xn.shape
    qd = wq_bf.shape[1]
    out_spec = pl.BlockSpec((None, PEER_NKEYS, tm), lambda i, h: (h, 0, i))
    out_sds = jax.ShapeDtypeStruct((PEER_HEADS, PEER_NKEYS, n), F32)
    return pl.pallas_call(
        _route_kernel,
        grid=(n // tm, PEER_HEADS),
        in_specs=[pl.BlockSpec((tm, d), lambda i, h: (i, 0)),
                  pl.BlockSpec((d, qd), lambda i, h: (0, 0)),
                  pl.BlockSpec((None, 2, PEER_NKEYS, PEER_NKEYS), lambda i, h: (h, 0, 0, 0))],
        out_specs=[out_spec] * 4,
        out_shape=[out_sds] * 4,
        scratch_shapes=[pltpu.VMEM((2 * PEER_HEADS, tm, PEER_NKEYS), BF16)],
        compiler_params=_params(("parallel", "arbitrary")),
        name="peer_route",
    )(xn, wq_bf, keys_bf)


def _peer_kernel(xn_ref, u_ref, vt_ref, r2_ref, a2_ref, lb_ref, a1_ref, h_ref, gf_ref,
                 y_ref, acc_ref, *, final_norm):
    j = pl.program_id(1)
    te = u_ref.shape[0]
    nblk = te // PEER_NKEYS

    @pl.when(j == 0)
    def _():
        acc_ref[...] = jnp.zeros_like(acc_ref)

    ht = lax.dot_general(u_ref[...], xn_ref[...], _NT, preferred_element_type=F32)
    act = 0.5 * ht * (1.0 + lax.erf(ht * (1.0 / math.sqrt(2.0))))
    gates = []
    for bb in range(nblk):
        blk = j * nblk + bb
        g = jnp.zeros((PEER_NKEYS, ht.shape[1]), F32)
        for hh in range(PEER_HEADS):
            lim = lb_ref[hh, pl.ds(blk, 1), :]
            w1 = a1_ref[hh, pl.ds(blk, 1), :]
            g = g + jnp.where(r2_ref[hh] < lim, w1 * a2_ref[hh], 0.0)
        gates.append(g)
    gate = jnp.concatenate(gates, axis=0) if nblk > 1 else gates[0]
    p = (act * gate).astype(BF16)
    acc_ref[...] += jnp.dot(vt_ref[...], p, preferred_element_type=F32)

    @pl.when(j == pl.num_programs(1) - 1)
    def _():
        out = h_ref[...] + acc_ref[...].T
        y_ref[...] = _rms(out, gf_ref[...]) if final_norm else out


def _peer(xn, u_bf, vt_bf, r2, a2, lb, a1, h, gf, final_norm, tm, te):
    n, d = xn.shape
    ne = u_bf.shape[0]
    gate_spec = pl.BlockSpec((PEER_HEADS, PEER_NKEYS, tm), lambda i, j: (0, 0, i))
    return pl.pallas_call(
        functools.partial(_peer_kernel, final_norm=final_norm),
        grid=(n // tm, ne // te),
        in_specs=[pl.BlockSpec((tm, d), lambda i, j: (i, 0)),
                  pl.BlockSpec((te, d), lambda i, j: (j, 0)),
                  pl.BlockSpec((d, te), lambda i, j: (0, j)),
                  gate_spec, gate_spec, gate_spec, gate_spec,
                  pl.BlockSpec((tm, d), lambda i, j: (i, 0)),
                  pl.BlockSpec((1, d), lambda i, j: (0, 0))],
        out_specs=pl.BlockSpec((tm, d), lambda i, j: (i, 0)),
        out_shape=jax.ShapeDtypeStruct((n, d), F32),
        scratch_shapes=[pltpu.VMEM((d, tm), F32)],
        compiler_params=_params(("parallel", "arbitrary")),
        name="peer_experts",
    )(xn, u_bf, vt_bf, r2, a2, lb, a1, h, gf)


def _tile(n, pref):
    t = min(n, pref)
    assert n % t == 0, (n, t)
    return t


def kernel(x_prompt, x_sample, cache_win_k, cache_win_v, state_conv, norm1_g, w_in, conv_w,
           attn_out_g, conv_out_g, w_out, norm2_g, peer_wq, peer_sub_keys, peer_u, peer_v,
           final_norm_g):
    bp, s, d = x_prompt.shape
    bs, t, _ = x_sample.shape
    depth = w_in.shape[0]
    past_len = 8192
    win_p = min(BRANCHES[-1][0], s)
    np_, ns = bp * s, bs * t

    tm_p = _tile(s, 512)
    tq = _tile(s, 256)
    cos_p, sin_p = _rope_tables(jnp.arange(s, dtype=jnp.int32))
    pos_s = past_len + jnp.repeat(jnp.arange(t, dtype=jnp.int32), bs)
    cos_s, sin_s = _rope_tables(pos_s)

    hp = x_prompt.reshape(np_, d)
    hs = x_sample.transpose(1, 0, 2).reshape(ns, d)
    outs = {k: [] for k in ("kp", "vp", "cp", "ks", "vs", "cs")}
    row = lambda g: g.reshape(1, -1)
    y_p = y_s = None
    for l in range(depth):
        w_in_bf = w_in[l].astype(BF16)
        w_out_bf = w_out[l].astype(BF16)
        wq_bf = peer_wq[l].astype(BF16)
        keys_bf = peer_sub_keys[l].astype(BF16)
        u_bf = peer_u[l].astype(BF16)
        vt_bf = peer_v[l].T.astype(BF16)
        last = l == depth - 1

        def tail(hres, attn, cb, z, state, stride, tiles_per_seq, tm):
            h_mid, xn2 = _mix(attn, cb, z, state, hres, conv_w[l], row(attn_out_g[l]),
                              row(conv_out_g[l]), w_out_bf, row(norm2_g[l]), tm, stride,
                              tiles_per_seq)
            n = hres.shape[0]
            r2, a2, lb, a1 = _route(xn2, wq_bf, keys_bf, _tile(n, 256))
            return _peer(xn2, u_bf, vt_bf, r2, a2, lb, a1, h_mid, row(final_norm_g), last,
                         _tile(n, 512), 512)

        q, k, v, cb, z = _inproj(hp, row(norm1_g[l]), w_in_bf, cos_p, sin_p, tm_p)
        attn = _attn_prompt(q.reshape(bp, s, ATTN_W), k.reshape(bp, s, ATTN_W),
                            v.reshape(bp, s, ATTN_W), tq).reshape(np_, ATTN_W)
        k4 = k.reshape(bp, s, N_HEADS, HEAD_DIM)
        v4 = v.reshape(bp, s, N_HEADS, HEAD_DIM)
        outs["kp"].append(k4[:, s - win_p:])
        outs["vp"].append(v4[:, s - win_p:])
        outs["cp"].append(z.reshape(bp, s, CONV_W)[:, s - (CONV_K - 1):])
        zero_state = jnp.zeros((bp, 8, CONV_W), F32)
        y_p = tail(hp, attn, cb, z, zero_state, 1, s // tm_p, tm_p)

        q, k, v, cb, z = _inproj(hs, row(norm1_g[l]), w_in_bf, cos_s, sin_s, _tile(ns, 512))
        to_bt = lambda a: a.reshape(t, bs, -1).transpose(1, 0, 2)
        k_bt, v_bt = to_bt(k), to_bt(v)
        attn = _attn_sample(to_bt(q), k_bt, v_bt,
                            cache_win_k[l].reshape(bs, -1, ATTN_W),
                            cache_win_v[l].reshape(bs, -1, ATTN_W))
        attn = attn.transpose(1, 0, 2).reshape(ns, ATTN_W)
        outs["ks"].append(k_bt.reshape(bs, t, N_HEADS, HEAD_DIM))
        outs["vs"].append(v_bt.reshape(bs, t, N_HEADS, HEAD_DIM))
        outs["cs"].append(to_bt(z)[:, t - (CONV_K - 1):])
        state = state_conv[l].transpose(1, 0, 2).reshape(1, (CONV_K - 1) * bs, CONV_W)
        y_s = tail(hs, attn, cb, z, state, bs, 1, ns)
        hp, hs = y_p, y_s

    y_prompt = y_p.reshape(bp, s, d)
    y_sample = y_s.reshape(t, bs, d).transpose(1, 0, 2)
    st = lambda name: jnp.stack(outs[name])
    return (y_prompt, y_sample, st("kp"), st("vp"), st("cp"), st("ks"), st("vs"), st("cs"))
```

```python
import functools
import math

import jax
import jax.numpy as jnp
from jax import lax
from jax.experimental import pallas as pl
from jax.experimental.pallas import tpu as pltpu

F32 = jnp.float32
BF16 = jnp.bfloat16

HEAD_DIM = 64
N_HEADS = 12
ATTN_W = N_HEADS * HEAD_DIM
CONV_W = 256
CONV_K = 3
BRANCHES = ((128, 1), (512, 4), (2048, 16))
ROPE_THETA = 10000.0
EPS = 1e-6
PEER_HEADS = 8
PEER_NKEYS = 128
PEER_TOPK = 16
NEG = -1e30
LANES = 128
VMEM_LIMIT = 56 * 1024 * 1024

_NT = (((1,), (1,)), ((), ()))


def _rms(x, g):
    return x * lax.rsqrt(jnp.mean(x * x, axis=-1, keepdims=True) + EPS) * g


def _params(sem):
    return pltpu.CompilerParams(dimension_semantics=sem, vmem_limit_bytes=VMEM_LIMIT)


def _inproj_kernel(x_ref, g_ref, w_ref, cos_ref, sin_ref,
                   q_ref, k_ref, v_ref, cb_ref, z_ref):
    xb = _rms(x_ref[...], g_ref[...]).astype(BF16)
    tm = xb.shape[0]
    lane = lax.broadcasted_iota(jnp.int32, (tm, LANES), 1)
    low_half = (lane % HEAD_DIM) < (HEAD_DIM // 2)
    cos = cos_ref[...]
    sin = sin_ref[...]

    def rope(t):
        partner = jnp.where(low_half, pltpu.roll(t, LANES - HEAD_DIM // 2, 1),
                            pltpu.roll(t, HEAD_DIM // 2, 1))
        return t * cos + partner * sin

    scale = 1.0 / math.sqrt(HEAD_DIM)
    q = jnp.dot(xb, w_ref[:, 0:ATTN_W], preferred_element_type=F32)
    for c in range(ATTN_W // LANES):
        sl = slice(c * LANES, (c + 1) * LANES)
        q_ref[:, sl] = (rope(q[:, sl]) * scale).astype(BF16)
    k = jnp.dot(xb, w_ref[:, ATTN_W:2 * ATTN_W], preferred_element_type=F32)
    for c in range(ATTN_W // LANES):
        sl = slice(c * LANES, (c + 1) * LANES)
        k_ref[:, sl] = rope(k[:, sl])
    v_ref[...] = jnp.dot(xb, w_ref[:, 2 * ATTN_W:3 * ATTN_W], preferred_element_type=F32)
    c0 = 3 * ATTN_W
    conv = jnp.dot(xb, w_ref[:, c0:c0 + 3 * CONV_W], preferred_element_type=F32)
    cb_ref[...] = conv[:, 0:CONV_W]
    z_ref[...] = conv[:, CONV_W:2 * CONV_W] * conv[:, 2 * CONV_W:3 * CONV_W]


def _inproj(x, g, w_bf, cos, sin, tm):
    n, d = x.shape
    win = w_bf.shape[1]
    period = cos.shape[0] // tm
    row = lambda i: (i, 0)
    const = lambda i: (0, 0)
    return pl.pallas_call(
        _inproj_kernel,
        grid=(n // tm,),
        in_specs=[pl.BlockSpec((tm, d), row),
                  pl.BlockSpec((1, d), const),
                  pl.BlockSpec((d, win), const),
                  pl.BlockSpec((tm, LANES), lambda i: (i % period, 0)),
                  pl.BlockSpec((tm, LANES), lambda i: (i % period, 0))],
        out_specs=[pl.BlockSpec((tm, ATTN_W), row), pl.BlockSpec((tm, ATTN_W), row),
                   pl.BlockSpec((tm, ATTN_W), row), pl.BlockSpec((tm, CONV_W), row),
                   pl.BlockSpec((tm, CONV_W), row)],
        out_shape=[jax.ShapeDtypeStruct((n, ATTN_W), BF16),
                   jax.ShapeDtypeStruct((n, ATTN_W), F32),
                   jax.ShapeDtypeStruct((n, ATTN_W), F32),
                   jax.ShapeDtypeStruct((n, CONV_W), F32),
                   jax.ShapeDtypeStruct((n, CONV_W), F32)],
        compiler_params=_params(("parallel",)),
        name="inproj",
    )(x, g, w_bf, cos, sin)


def _rope_tables(pos):
    half = HEAD_DIM // 2
    inv = jnp.exp(-math.log(ROPE_THETA) * jnp.arange(half, dtype=F32) * (2.0 / HEAD_DIM))
    ang = pos.astype(F32)[:, None] * inv[None, :]
    cos = jnp.cos(ang)
    sin = jnp.sin(ang)
    reps = LANES // HEAD_DIM
    cos_t = jnp.tile(jnp.concatenate([cos, cos], axis=1), (1, reps))
    sin_t = jnp.tile(jnp.concatenate([-sin, sin], axis=1), (1, reps))
    return cos_t, sin_t


def _branch_count(dist):
    cnt = jnp.zeros(dist.shape, F32)
    for window, dil in BRANCHES:
        hit = (dist <= window) & ((dist & (dil - 1)) == 0)
        cnt = cnt + jnp.where(hit, 1.0, 0.0)
    return jnp.where(dist >= 0, cnt, 0.0)


def _attn_prompt_kernel(q_ref, k_ref, v_ref, bias_ref, o_ref,
                        kt_ref, vz_ref, m_ref, l_ref, acc_ref, *, tq):
    qi = pl.program_id(2)
    heads = LANES // HEAD_DIM
    lane = lax.broadcasted_iota(jnp.int32, (tq, LANES), 1)
    head_of_lane = lane // HEAD_DIM

    @pl.when(qi == 0)
    def _():
        for jb in range(k_ref.shape[0] // tq):
            rows = slice(jb * tq, (jb + 1) * tq)
            kt_ref[jb] = k_ref[rows, :].T.astype(BF16)
            vb = v_ref[rows, :]
            for h in range(heads):
                vz_ref[heads * jb + h] = jnp.where(head_of_lane == h, vb, 0.0).astype(BF16)

    q = q_ref[...]
    qz = [jnp.where(head_of_lane == h, q, jnp.zeros_like(q)) for h in range(heads)]
    m_ref[...] = jnp.full(m_ref.shape, NEG, F32)
    l_ref[...] = jnp.zeros(l_ref.shape, F32)
    acc_ref[...] = jnp.zeros(acc_ref.shape, F32)

    def body(j, _):
        bias = bias_ref[qi - j]
        kt = kt_ref[j]
        alphas = []
        pv = None
        for h in range(heads):
            s = jnp.dot(qz[h], kt, preferred_element_type=F32) + bias
            m_prev = m_ref[h]
            m_next = jnp.maximum(m_prev, jnp.max(s, axis=-1, keepdims=True))
            p = jnp.concatenate(
                [jnp.exp(s[:, c * LANES:(c + 1) * LANES] - m_next) for c in range(tq // LANES)],
                axis=-1)
            alpha = jnp.exp(m_prev - m_next)
            l_ref[h] = alpha * l_ref[h] + jnp.sum(p, axis=-1, keepdims=True)
            m_ref[h] = m_next
            alphas.append(alpha)
            d = jnp.dot(p.astype(BF16), vz_ref[heads * j + h], preferred_element_type=F32)
            pv = d if pv is None else pv + d
        alpha_both = jnp.where(head_of_lane == 0, alphas[0], alphas[1])
        acc_ref[...] = alpha_both * acc_ref[...] + pv
        return 0

    lax.fori_loop(0, qi + 1, body, 0)
    l_both = jnp.where(head_of_lane == 0, l_ref[0], l_ref[1])
    o_ref[...] = acc_ref[...] / l_both


def _attn_bias(s, tq):
    d = jnp.arange(s // tq, dtype=jnp.int32)[:, None, None] * tq
    dist = d + jnp.arange(tq, dtype=jnp.int32)[None, :, None] \
        - jnp.arange(tq, dtype=jnp.int32)[None, None, :]
    cnt = _branch_count(dist)
    return jnp.where(cnt > 0.0, jnp.log(jnp.maximum(cnt, 1.0)), NEG)


def _attn_prompt(q, k, v, tq):
    b, s, _ = q.shape
    heads = LANES // HEAD_DIM
    assert heads == 2 and tq % LANES == 0
    nkb = s // tq
    blk = lambda bi, hp, qi: (bi, qi, hp)
    full = lambda bi, hp, qi: (bi, 0, hp)
    return pl.pallas_call(
        functools.partial(_attn_prompt_kernel, tq=tq),
        grid=(b, ATTN_W // LANES, nkb),
        in_specs=[pl.BlockSpec((None, tq, LANES), blk),
                  pl.BlockSpec((None, s, LANES), full),
                  pl.BlockSpec((None, s, LANES), full),
                  pl.BlockSpec((nkb, tq, tq), lambda bi, hp, qi: (0, 0, 0))],
        out_specs=pl.BlockSpec((None, tq, LANES), blk),
        out_shape=jax.ShapeDtypeStruct((b, s, ATTN_W), F32),
        scratch_shapes=[pltpu.VMEM((nkb, LANES, tq), BF16),
                        pltpu.VMEM((nkb * heads, tq, LANES), BF16),
                        pltpu.VMEM((heads, tq, LANES), F32),
                        pltpu.VMEM((heads, tq, LANES), F32),
                        pltpu.VMEM((tq, LANES), F32)],
        compiler_params=_params(("parallel", "parallel", "arbitrary")),
        name="attn_prompt",
    )(q, k, v, _attn_bias(s, tq))


def _attn_sample_kernel(q_ref, kn_ref, vn_ref, kc_ref, vc_ref, o_ref, *, chunk):
    t = q_ref.shape[0]
    n_past = kc_ref.shape[0]
    rows = N_HEADS * t
    qt = jnp.concatenate([q_ref[...]] * N_HEADS, axis=0)
    r_id = lax.broadcasted_iota(jnp.int32, (rows, ATTN_W), 0)
    c_id = lax.broadcasted_iota(jnp.int32, (rows, ATTN_W), 1)
    own = (r_id // t) == (c_id // HEAD_DIM)
    qbd = jnp.where(own, qt, jnp.zeros_like(qt))

    def scores(kblk, first_key):
        nk = kblk.shape[0]
        s = lax.dot_general(qbd, kblk.astype(BF16), _NT, preferred_element_type=F32)
        qpos = n_past + lax.broadcasted_iota(jnp.int32, (rows, nk), 0) % t
        kpos = first_key + lax.broadcasted_iota(jnp.int32, (rows, nk), 1)
        cnt = _branch_count(qpos - kpos)
        return jnp.where(cnt > 0.0, s, NEG), cnt

    parts = [scores(kc_ref[c * chunk:(c + 1) * chunk, :], c * chunk)
             for c in range(n_past // chunk)]
    parts.append(scores(kn_ref[...], n_past))
    m = functools.reduce(jnp.maximum, [jnp.max(s, axis=-1, keepdims=True) for s, _ in parts])
    den = jnp.zeros((rows, 1), F32)
    acc = jnp.zeros((rows, ATTN_W), F32)
    for c, (s, cnt) in enumerate(parts):
        p = cnt * jnp.exp(s - m)
        den = den + jnp.sum(p, axis=-1, keepdims=True)
        vblk = vn_ref[...] if c == len(parts) - 1 else vc_ref[c * chunk:(c + 1) * chunk, :]
        acc = acc + jnp.dot(p.astype(BF16), vblk.astype(BF16), preferred_element_type=F32)
    o_full = jnp.where(own, acc / den, 0.0)
    out = o_full[0:t, :]
    for h in range(1, N_HEADS):
        out = out + o_full[h * t:(h + 1) * t, :]
    o_ref[...] = out


def _attn_sample(q, k_new, v_new, k_cache, v_cache):
    b, t, _ = q.shape
    n_past = k_cache.shape[1]
    new = lambda bi: (bi, 0, 0)
    return pl.pallas_call(
        functools.partial(_attn_sample_kernel, chunk=512),
        grid=(b,),
        in_specs=[pl.BlockSpec((None, t, ATTN_W), new),
                  pl.BlockSpec((None, t, ATTN_W), new),
                  pl.BlockSpec((None, t, ATTN_W), new),
                  pl.BlockSpec((None, n_past, ATTN_W), new),
                  pl.BlockSpec((None, n_past, ATTN_W), new)],
        out_specs=pl.BlockSpec((None, t, ATTN_W), new),
        out_shape=jax.ShapeDtypeStruct((b, t, ATTN_W), F32),
        compiler_params=_params(("parallel",)),
        name="attn_sample",
    )(q, k_new, v_new, k_cache, v_cache)


def _mix_kernel(attn_ref, cb_ref, z_ref, zprev_ref, state_ref, x_ref, cw_ref, ga_ref, gc_ref,
                wo_ref, g2_ref, h_ref, xn_ref, zbuf, *, stride, tiles_per_seq):
    tm = z_ref.shape[0]
    hb = state_ref.shape[0]
    if tiles_per_seq > 1:
        first = (pl.program_id(0) % tiles_per_seq) == 0
        zbuf[0:hb, :] = jnp.where(first, state_ref[...], zprev_ref[...])
    else:
        zbuf[0:hb, :] = state_ref[...]
    z = z_ref[...]
    zbuf[hb:hb + tm, :] = z
    cw = cw_ref[...]
    y = (cw[0:1, :] * zbuf[hb - 2 * stride:hb - 2 * stride + tm, :]
         + cw[1:2, :] * zbuf[hb - stride:hb - stride + tm, :]
         + cw[2:3, :] * z)
    conv_out = cb_ref[...] * y
    a = _rms(attn_ref[...], ga_ref[...]).astype(BF16)
    c = _rms(conv_out, gc_ref[...]).astype(BF16)
    mixed = (jnp.dot(a, wo_ref[0:ATTN_W, :], preferred_element_type=F32)
             + jnp.dot(c, wo_ref[ATTN_W:ATTN_W + CONV_W, :], preferred_element_type=F32))
    h = x_ref[...] + mixed
    h_ref[...] = h
    xn_ref[...] = _rms(h, g2_ref[...]).astype(BF16)


def _mix(attn, cb, z, state, x, conv_w, ga, gc, wo_bf, g2, tm, stride, tiles_per_seq):
    n, d = x.shape
    hb = state.shape[1]
    row = lambda i: (i, 0)
    const = lambda i: (0, 0)
    prev = lambda i: (jnp.maximum(i * (tm // hb) - 1, 0), 0)
    return pl.pallas_call(
        functools.partial(_mix_kernel, stride=stride, tiles_per_seq=tiles_per_seq),
        grid=(n // tm,),
        in_specs=[pl.BlockSpec((tm, ATTN_W), row),
                  pl.BlockSpec((tm, CONV_W), row),
                  pl.BlockSpec((tm, CONV_W), row),
                  pl.BlockSpec((hb, CONV_W), prev),
                  pl.BlockSpec((None, hb, CONV_W), lambda i: (i // tiles_per_seq, 0, 0)),
                  pl.BlockSpec((tm, d), row),
                  pl.BlockSpec((CONV_K, CONV_W), const),
                  pl.BlockSpec((1, ATTN_W), const),
                  pl.BlockSpec((1, CONV_W), const),
                  pl.BlockSpec((d, d), const),
                  pl.BlockSpec((1, d), const)],
        out_specs=[pl.BlockSpec((tm, d), row), pl.BlockSpec((tm, d), row)],
        out_shape=[jax.ShapeDtypeStruct((n, d), F32), jax.ShapeDtypeStruct((n, d), BF16)],
        scratch_shapes=[pltpu.VMEM((hb + tm, CONV_W), F32)],
        compiler_params=_params(("parallel",)),
        name="mix",
    )(attn, cb, z, z, state, x, conv_w, ga, gc, wo_bf, g2)


def _top16(s, order):
    rank = jnp.full(s.shape, float(PEER_TOPK), F32)
    vals = []
    for r in range(PEER_TOPK):
        m = jnp.max(s, axis=0, keepdims=True)
        if order is None:
            sel = s == m
        else:
            big = jnp.int32(2 ** 30)
            sel = order == jnp.min(jnp.where(s == m, order, big), axis=0, keepdims=True)
        rank = jnp.where(sel, float(r), rank)
        s = jnp.where(sel, -jnp.inf, s)
        vals.append(m)
    return rank, jnp.concatenate(vals, axis=0)


_CAND_GROUPS = ((0, 0), (0, 8), (1, 0), (2, 0), (3, 0), (4, 0), (5, 0), (6, 0), (7, 0))


def _route_tile(s1, s2, exact):
    tm = s1.shape[1]
    sub = 8
    iota128 = lax.broadcasted_iota(jnp.int32, s1.shape, 0) if exact else None
    rank1, sv1 = _top16(s1, iota128)
    rank2, sv2 = _top16(s2, iota128)
    groups = [sv1[a:a + 1, :] + sv2[b0:b0 + sub, :] for a, b0 in _CAND_GROUPS]
    groups.append(sv1[sub:2 * sub, :] + sv2[0:1, :])
    cand = jnp.concatenate(groups, axis=0)
    order = None
    if exact:
        i8 = lax.broadcasted_iota(jnp.int32, (sub, tm), 0)
        order = jnp.concatenate([a * PEER_TOPK + b0 + i8 for a, b0 in _CAND_GROUPS]
                                + [(sub + i8) * PEER_TOPK], axis=0)
    crank, _ = _top16(cand, order)
    chosen = crank < float(PEER_TOPK)
    cmax = sv1[0:1, :] + sv2[0:1, :]
    zsum = jnp.sum(jnp.where(chosen, jnp.exp(cand - cmax), 0.0), axis=0, keepdims=True)
    picked = jnp.where(chosen, 1.0, 0.0)
    n_of_a = [jnp.sum(picked[0:2 * sub, :], axis=0, keepdims=True)]
    for a in range(1, sub):
        n_of_a.append(jnp.sum(picked[(a + 1) * sub:(a + 2) * sub, :], axis=0, keepdims=True))
    last = (len(_CAND_GROUPS)) * sub
    for a in range(sub, PEER_TOPK):
        n_of_a.append(picked[last + a - sub:last + a - sub + 1, :])
    lb = jnp.zeros(rank1.shape, F32)
    for a in range(PEER_TOPK):
        lb = jnp.where(rank1 == float(a), n_of_a[a], lb)
    a2 = jnp.exp(s2 - sv2[0:1, :])
    a1 = jnp.exp(s1 - sv1[0:1, :]) / zsum
    tied = None
    if not exact:
        def extra(rk):
            n_sel = jnp.sum(jnp.where(rk < float(PEER_TOPK), 1.0, 0.0), axis=0, keepdims=True)
            return jnp.max(n_sel) > float(PEER_TOPK)
        tied = extra(rank1) | extra(rank2) | extra(crank)
    return rank2, a2, lb, a1, tied


def _route_kernel(xn_ref, wq_ref, keys_ref, r2_ref, a2_ref, lb_ref, a1_ref, q_scr):
    h = pl.program_id(1)
    nsub = 2 * PEER_HEADS

    @pl.when(h == 0)
    def _():
        q = jnp.dot(xn_ref[...], wq_ref[...], preferred_element_type=F32)
        for c in range(nsub):
            q_scr[c] = q[:, c * PEER_NKEYS:(c + 1) * PEER_NKEYS].astype(BF16)

    s1 = lax.dot_general(keys_ref[0], q_scr[2 * h], _NT, preferred_element_type=F32)
    s2 = lax.dot_general(keys_ref[1], q_scr[2 * h + 1], _NT, preferred_element_type=F32)

    def twice(x):
        hi = lax.bitcast_convert_type(x.astype(BF16).astype(F32), jnp.uint32)
        return hi | (hi >> 16)

    def emit(rank2, a2, lb, a1):
        r2_ref[...] = rank2.astype(BF16)
        a2_ref[...] = a2.astype(BF16)
        lb_ref[...] = twice(lb)
        a1_ref[...] = twice(a1)

    *outs, tied = _route_tile(s1, s2, exact=False)
    emit(*outs)

    @pl.when(tied)
    def _():
        emit(*_route_tile(s1, s2, exact=True)[:4])


def _route(xn, wq_bf, keys_bf, tm):
    n, d = xn.shape
    qd = wq_bf.shape[1]
    out_spec = pl.BlockSpec((None, PEER_NKEYS, tm), lambda i, h: (h, 0, i))
    out_sds = lambda dt: jax.ShapeDtypeStruct((PEER_HEADS, PEER_NKEYS, n), dt)
    return pl.pallas_call(
        _route_kernel,
        grid=(n // tm, PEER_HEADS),
        in_specs=[pl.BlockSpec((tm, d), lambda i, h: (i, 0)),
                  pl.BlockSpec((d, qd), lambda i, h: (0, 0)),
                  pl.BlockSpec((None, 2, PEER_NKEYS, PEER_NKEYS), lambda i, h: (h, 0, 0, 0))],
        out_specs=[out_spec] * 4,
        out_shape=[out_sds(BF16), out_sds(BF16), out_sds(jnp.uint32), out_sds(jnp.uint32)],
        scratch_shapes=[pltpu.VMEM((2 * PEER_HEADS, tm, PEER_NKEYS), BF16)],
        compiler_params=_params(("parallel", "arbitrary")),
        name="peer_route",
    )(xn, wq_bf, keys_bf)


def _peer_kernel(xn_ref, u_ref, vt_ref, r2_ref, a2_ref, lb_ref, a1_ref, h_ref, gf_ref,
                 y_ref, acc_ref, *, final_norm):
    j = pl.program_id(1)
    te = u_ref.shape[0]
    nblk = te // PEER_NKEYS

    @pl.when(j == 0)
    def _():
        acc_ref[...] = jnp.zeros_like(acc_ref)

    def row_bf16(ref, hh, blk, cols):
        word = jnp.broadcast_to(ref[hh, pl.ds(blk, 1), cols], (8, cols.stop - cols.start))
        tile = pltpu.bitcast(word, BF16)
        return jnp.concatenate([tile] * (PEER_NKEYS // tile.shape[0]), axis=0)

    tm = xn_ref.shape[0]
    halves = [slice(c * (tm // 2), (c + 1) * (tm // 2)) for c in range(2)] if tm >= 512 \
        else [slice(0, tm)]
    hts = [lax.dot_general(u_ref[...], xn_ref[cols, :], _NT, preferred_element_type=F32)
           for cols in halves]
    for cols, ht in zip(halves, hts):
        act = 0.5 * ht * (1.0 + lax.erf(ht * (1.0 / math.sqrt(2.0))))
        gates = []
        for bb in range(nblk):
            blk = j * nblk + bb
            g = jnp.zeros((PEER_NKEYS, ht.shape[1]), BF16)
            for hh in range(PEER_HEADS):
                lim = row_bf16(lb_ref, hh, blk, cols)
                w1 = row_bf16(a1_ref, hh, blk, cols)
                g = g + jnp.where(r2_ref[hh, :, cols] < lim, w1 * a2_ref[hh, :, cols],
                                  jnp.zeros_like(g))
            gates.append(g)
        gate = jnp.concatenate(gates, axis=0) if nblk > 1 else gates[0]
        p = act.astype(BF16) * gate
        acc_ref[:, cols] += jnp.dot(vt_ref[...], p, preferred_element_type=F32)

    @pl.when(j == pl.num_programs(1) - 1)
    def _():
        out = h_ref[...] + acc_ref[...].T
        y_ref[...] = _rms(out, gf_ref[...]) if final_norm else out


def _peer(xn, u_bf, vt_bf, r2, a2, lb, a1, h, gf, final_norm, tm, te):
    n, d = xn.shape
    ne = u_bf.shape[0]
    gate_spec = pl.BlockSpec((PEER_HEADS, PEER_NKEYS, tm), lambda i, j: (0, 0, i))
    return pl.pallas_call(
        functools.partial(_peer_kernel, final_norm=final_norm),
        grid=(n // tm, ne // te),
        in_specs=[pl.BlockSpec((tm, d), lambda i, j: (i, 0)),
                  pl.BlockSpec((te, d), lambda i, j: (j, 0)),
                  pl.BlockSpec((d, te), lambda i, j: (0, j)),
                  gate_spec, gate_spec, gate_spec, gate_spec,
                  pl.BlockSpec((tm, d), lambda i, j: (i, 0)),
                  pl.BlockSpec((1, d), lambda i, j: (0, 0))],
        out_specs=pl.BlockSpec((tm, d), lambda i, j: (i, 0)),
        out_shape=jax.ShapeDtypeStruct((n, d), F32),
        scratch_shapes=[pltpu.VMEM((d, tm), F32)],
        compiler_params=_params(("parallel", "arbitrary")),
        name="peer_experts",
    )(xn, u_bf, vt_bf, r2, a2, lb, a1, h, gf)


def _tile(n, pref):
    t = min(n, pref)
    assert n % t == 0, (n, t)
    return t


def kernel(x_prompt, x_sample, cache_win_k, cache_win_v, state_conv, norm1_g, w_in, conv_w,
           attn_out_g, conv_out_g, w_out, norm2_g, peer_wq, peer_sub_keys, peer_u, peer_v,
           final_norm_g):
    bp, s, d = x_prompt.shape
    bs, t, _ = x_sample.shape
    depth = w_in.shape[0]
    past_len = 8192
    win_p = min(BRANCHES[-1][0], s)
    np_, ns = bp * s, bs * t

    tm_p = _tile(s, 512)
    tq = _tile(s, 256)
    cos_p, sin_p = _rope_tables(jnp.arange(s, dtype=jnp.int32))
    pos_s = past_len + jnp.repeat(jnp.arange(t, dtype=jnp.int32), bs)
    cos_s, sin_s = _rope_tables(pos_s)

    hp = x_prompt.reshape(np_, d)
    hs = x_sample.transpose(1, 0, 2).reshape(ns, d)
    outs = {k: [] for k in ("kp", "vp", "cp", "ks", "vs", "cs")}
    row = lambda g: g.reshape(1, -1)
    y_p = y_s = None
    for l in range(depth):
        w_in_bf = w_in[l].astype(BF16)
        w_out_bf = w_out[l].astype(BF16)
        wq_bf = peer_wq[l].astype(BF16)
        keys_bf = peer_sub_keys[l].astype(BF16)
        u_bf = peer_u[l].astype(BF16)
        vt_bf = peer_v[l].T.astype(BF16)
        last = l == depth - 1

        def tail(hres, attn, cb, z, state, stride, tiles_per_seq, tm):
            h_mid, xn2 = _mix(attn, cb, z, state, hres, conv_w[l], row(attn_out_g[l]),
                              row(conv_out_g[l]), w_out_bf, row(norm2_g[l]), tm, stride,
                              tiles_per_seq)
            n = hres.shape[0]
            r2, a2, lb, a1 = _route(xn2, wq_bf, keys_bf, _tile(n, 256))
            return _peer(xn2, u_bf, vt_bf, r2, a2, lb, a1, h_mid, row(final_norm_g), last,
                         _tile(n, 512), 512)

        q, k, v, cb, z = _inproj(hp, row(norm1_g[l]), w_in_bf, cos_p, sin_p, tm_p)
        attn = _attn_prompt(q.reshape(bp, s, ATTN_W), k.reshape(bp, s, ATTN_W),
                            v.reshape(bp, s, ATTN_W), tq).reshape(np_, ATTN_W)
        k4 = k.reshape(bp, s, N_HEADS, HEAD_DIM)
        v4 = v.reshape(bp, s, N_HEADS, HEAD_DIM)
        outs["kp"].append(k4[:, s - win_p:])
        outs["vp"].append(v4[:, s - win_p:])
        outs["cp"].append(z.reshape(bp, s, CONV_W)[:, s - (CONV_K - 1):])
        zero_state = jnp.zeros((bp, 8, CONV_W), F32)
        y_p = tail(hp, attn, cb, z, zero_state, 1, s // tm_p, tm_p)

        q, k, v, cb, z = _inproj(hs, row(norm1_g[l]), w_in_bf, cos_s, sin_s, _tile(ns, 512))
        to_bt = lambda a: a.reshape(t, bs, -1).transpose(1, 0, 2)
        k_bt, v_bt = to_bt(k), to_bt(v)
        attn = _attn_sample(to_bt(q), k_bt, v_bt,
                            cache_win_k[l].reshape(bs, -1, ATTN_W),
                            cache_win_v[l].reshape(bs, -1, ATTN_W))
        attn = attn.transpose(1, 0, 2).reshape(ns, ATTN_W)
        outs["ks"].append(k_bt.reshape(bs, t, N_HEADS, HEAD_DIM))
        outs["vs"].append(v_bt.reshape(bs, t, N_HEADS, HEAD_DIM))
        outs["cs"].append(to_bt(z)[:, t - (CONV_K - 1):])
        state = state_conv[l].transpose(1, 0, 2).reshape(1, (CONV_K - 1) * bs, CONV_W)
        y_s = tail(hs, attn, cb, z, state, bs, 1, ns)
        hp, hs = y_p, y_s

    y_prompt = y_p.reshape(bp, s, d)
    y_sample = y_s.reshape(t, bs, d).transpose(1, 0, 2)
    st = lambda name: jnp.stack(outs[name])
    return (y_prompt, y_sample, st("kp"), st("vp"), st("cp"), st("ks"), st("vs"), st("cs"))
```

```python
import functools
import math

import jax
import jax.numpy as jnp
from jax import lax
from jax.experimental import pallas as pl
from jax.experimental.pallas import tpu as pltpu

F32 = jnp.float32
BF16 = jnp.bfloat16

HEAD_DIM = 64
N_HEADS = 12
ATTN_W = N_HEADS * HEAD_DIM
CONV_W = 256
CONV_K = 3
BRANCHES = ((128, 1), (512, 4), (2048, 16))
ROPE_THETA = 10000.0
EPS = 1e-6
PEER_HEADS = 8
PEER_NKEYS = 128
PEER_TOPK = 16
NEG = -1e30
LANES = 128
SUB_EXPERTS = 256
VMEM_LIMIT = 56 * 1024 * 1024

_NT = (((1,), (1,)), ((), ()))


def _rms(x, g):
    return x * lax.rsqrt(jnp.mean(x * x, axis=-1, keepdims=True) + EPS) * g


def _pack_pairs(x):
    r, n = x.shape
    bits = lax.bitcast_convert_type(x.astype(BF16).astype(F32), jnp.uint32) >> 16
    bits = bits.reshape(r // 16, 2, 8, n)
    return (bits[:, 0] | (bits[:, 1] << 16)).reshape(r // 2, n)


def _pair_order(x):
    r = x.shape[0]
    return x.reshape(r // 16, 2, 8, -1).transpose(0, 2, 1, 3).reshape(x.shape)


def _params(sem):
    return pltpu.CompilerParams(dimension_semantics=sem, vmem_limit_bytes=VMEM_LIMIT)


def _inproj_kernel(x_ref, g_ref, w_ref, cos_ref, sin_ref,
                   q_ref, k_ref, v_ref, cb_ref, z_ref):
    xb = _rms(x_ref[...], g_ref[...]).astype(BF16)
    tm = xb.shape[0]
    lane = lax.broadcasted_iota(jnp.int32, (tm, LANES), 1)
    low_half = (lane % HEAD_DIM) < (HEAD_DIM // 2)
    cos = cos_ref[...]
    sin = sin_ref[...]

    def rope(t):
        partner = jnp.where(low_half, pltpu.roll(t, LANES - HEAD_DIM // 2, 1),
                            pltpu.roll(t, HEAD_DIM // 2, 1))
        return t * cos + partner * sin

    scale = 1.0 / math.sqrt(HEAD_DIM)
    q = jnp.dot(xb, w_ref[:, 0:ATTN_W], preferred_element_type=F32)
    for c in range(ATTN_W // LANES):
        sl = slice(c * LANES, (c + 1) * LANES)
        q_ref[:, sl] = (rope(q[:, sl]) * scale).astype(BF16)
    k = jnp.dot(xb, w_ref[:, ATTN_W:2 * ATTN_W], preferred_element_type=F32)
    for c in range(ATTN_W // LANES):
        sl = slice(c * LANES, (c + 1) * LANES)
        k_ref[:, sl] = rope(k[:, sl])
    v_ref[...] = jnp.dot(xb, w_ref[:, 2 * ATTN_W:3 * ATTN_W], preferred_element_type=F32)
    c0 = 3 * ATTN_W
    conv = jnp.dot(xb, w_ref[:, c0:c0 + 3 * CONV_W], preferred_element_type=F32)
    cb_ref[...] = conv[:, 0:CONV_W]
    z_ref[...] = conv[:, CONV_W:2 * CONV_W] * conv[:, 2 * CONV_W:3 * CONV_W]


def _inproj(x, g, w_bf, cos, sin, tm):
    n, d = x.shape
    win = w_bf.shape[1]
    period = cos.shape[0] // tm
    row = lambda i: (i, 0)
    const = lambda i: (0, 0)
    return pl.pallas_call(
        _inproj_kernel,
        grid=(n // tm,),
        in_specs=[pl.BlockSpec((tm, d), row),
                  pl.BlockSpec((1, d), const),
                  pl.BlockSpec((d, win), const),
                  pl.BlockSpec((tm, LANES), lambda i: (i % period, 0)),
                  pl.BlockSpec((tm, LANES), lambda i: (i % period, 0))],
        out_specs=[pl.BlockSpec((tm, ATTN_W), row), pl.BlockSpec((tm, ATTN_W), row),
                   pl.BlockSpec((tm, ATTN_W), row), pl.BlockSpec((tm, CONV_W), row),
                   pl.BlockSpec((tm, CONV_W), row)],
        out_shape=[jax.ShapeDtypeStruct((n, ATTN_W), BF16),
                   jax.ShapeDtypeStruct((n, ATTN_W), F32),
                   jax.ShapeDtypeStruct((n, ATTN_W), F32),
                   jax.ShapeDtypeStruct((n, CONV_W), F32),
                   jax.ShapeDtypeStruct((n, CONV_W), F32)],
        compiler_params=_params(("parallel",)),
        name="inproj",
    )(x, g, w_bf, cos, sin)


def _rope_tables(pos):
    half = HEAD_DIM // 2
    inv = jnp.exp(-math.log(ROPE_THETA) * jnp.arange(half, dtype=F32) * (2.0 / HEAD_DIM))
    ang = pos.astype(F32)[:, None] * inv[None, :]
    cos = jnp.cos(ang)
    sin = jnp.sin(ang)
    reps = LANES // HEAD_DIM
    cos_t = jnp.tile(jnp.concatenate([cos, cos], axis=1), (1, reps))
    sin_t = jnp.tile(jnp.concatenate([-sin, sin], axis=1), (1, reps))
    return cos_t, sin_t


def _branch_count(dist):
    cnt = jnp.zeros(dist.shape, F32)
    for window, dil in BRANCHES:
        hit = (dist <= window) & ((dist & (dil - 1)) == 0)
        cnt = cnt + jnp.where(hit, 1.0, 0.0)
    return jnp.where(dist >= 0, cnt, 0.0)


def _attn_prompt_kernel(q_ref, k_ref, v_ref, bias_ref, o_ref,
                        kt_ref, vz_ref, m_ref, l_ref, acc_ref, *, tq):
    qi = pl.program_id(2)
    heads = LANES // HEAD_DIM
    lane = lax.broadcasted_iota(jnp.int32, (tq, LANES), 1)
    head_of_lane = lane // HEAD_DIM

    @pl.when(qi == 0)
    def _():
        for jb in range(k_ref.shape[0] // tq):
            rows = slice(jb * tq, (jb + 1) * tq)
            kt_ref[jb] = k_ref[rows, :].T.astype(BF16)
            vb = v_ref[rows, :]
            for h in range(heads):
                vz_ref[heads * jb + h] = jnp.where(head_of_lane == h, vb, 0.0).astype(BF16)

    q = q_ref[...]
    qz = [jnp.where(head_of_lane == h, q, jnp.zeros_like(q)) for h in range(heads)]
    m_ref[...] = jnp.full(m_ref.shape, NEG, F32)
    l_ref[...] = jnp.zeros(l_ref.shape, F32)
    acc_ref[...] = jnp.zeros(acc_ref.shape, F32)

    def body(j, _):
        bias = bias_ref[qi - j]
        kt = kt_ref[j]
        alphas = []
        pv = None
        for h in range(heads):
            s = jnp.dot(qz[h], kt, preferred_element_type=F32) + bias
            m_prev = m_ref[h]
            m_next = jnp.maximum(m_prev, jnp.max(s, axis=-1, keepdims=True))
            p = jnp.concatenate(
                [jnp.exp(s[:, c * LANES:(c + 1) * LANES] - m_next) for c in range(tq // LANES)],
                axis=-1)
            alpha = jnp.exp(m_prev - m_next)
            l_ref[h] = alpha * l_ref[h] + jnp.sum(p, axis=-1, keepdims=True)
            m_ref[h] = m_next
            alphas.append(alpha)
            d = jnp.dot(p.astype(BF16), vz_ref[heads * j + h], preferred_element_type=F32)
            pv = d if pv is None else pv + d
        alpha_both = jnp.where(head_of_lane == 0, alphas[0], alphas[1])
        acc_ref[...] = alpha_both * acc_ref[...] + pv
        return 0

    lax.fori_loop(0, qi + 1, body, 0)
    l_both = jnp.where(head_of_lane == 0, l_ref[0], l_ref[1])
    o_ref[...] = acc_ref[...] / l_both


def _attn_bias(s, tq):
    d = jnp.arange(s // tq, dtype=jnp.int32)[:, None, None] * tq
    dist = d + jnp.arange(tq, dtype=jnp.int32)[None, :, None] \
        - jnp.arange(tq, dtype=jnp.int32)[None, None, :]
    cnt = _branch_count(dist)
    return jnp.where(cnt > 0.0, jnp.log(jnp.maximum(cnt, 1.0)), NEG)


def _attn_prompt(q, k, v, tq):
    b, s, _ = q.shape
    heads = LANES // HEAD_DIM
    assert heads == 2 and tq % LANES == 0
    nkb = s // tq
    blk = lambda bi, hp, qi: (bi, qi, hp)
    full = lambda bi, hp, qi: (bi, 0, hp)
    return pl.pallas_call(
        functools.partial(_attn_prompt_kernel, tq=tq),
        grid=(b, ATTN_W // LANES, nkb),
        in_specs=[pl.BlockSpec((None, tq, LANES), blk),
                  pl.BlockSpec((None, s, LANES), full),
                  pl.BlockSpec((None, s, LANES), full),
                  pl.BlockSpec((nkb, tq, tq), lambda bi, hp, qi: (0, 0, 0))],
        out_specs=pl.BlockSpec((None, tq, LANES), blk),
        out_shape=jax.ShapeDtypeStruct((b, s, ATTN_W), F32),
        scratch_shapes=[pltpu.VMEM((nkb, LANES, tq), BF16),
                        pltpu.VMEM((nkb * heads, tq, LANES), BF16),
                        pltpu.VMEM((heads, tq, LANES), F32),
                        pltpu.VMEM((heads, tq, LANES), F32),
                        pltpu.VMEM((tq, LANES), F32)],
        compiler_params=_params(("parallel", "parallel", "arbitrary")),
        name="attn_prompt",
    )(q, k, v, _attn_bias(s, tq))


def _attn_sample_kernel(q_ref, kn_ref, vn_ref, kc_ref, vc_ref, o_ref, *, chunk):
    t = q_ref.shape[0]
    n_past = kc_ref.shape[0]
    rows = N_HEADS * t
    qt = jnp.concatenate([q_ref[...]] * N_HEADS, axis=0)
    r_id = lax.broadcasted_iota(jnp.int32, (rows, ATTN_W), 0)
    c_id = lax.broadcasted_iota(jnp.int32, (rows, ATTN_W), 1)
    own = (r_id // t) == (c_id // HEAD_DIM)
    qbd = jnp.where(own, qt, jnp.zeros_like(qt))

    def scores(kblk, first_key):
        nk = kblk.shape[0]
        s = lax.dot_general(qbd, kblk.astype(BF16), _NT, preferred_element_type=F32)
        qpos = n_past + lax.broadcasted_iota(jnp.int32, (rows, nk), 0) % t
        kpos = first_key + lax.broadcasted_iota(jnp.int32, (rows, nk), 1)
        cnt = _branch_count(qpos - kpos)
        return jnp.where(cnt > 0.0, s, NEG), cnt

    parts = [scores(kc_ref[c * chunk:(c + 1) * chunk, :], c * chunk)
             for c in range(n_past // chunk)]
    parts.append(scores(kn_ref[...], n_past))
    m = functools.reduce(jnp.maximum, [jnp.max(s, axis=-1, keepdims=True) for s, _ in parts])
    den = jnp.zeros((rows, 1), F32)
    acc = jnp.zeros((rows, ATTN_W), F32)
    for c, (s, cnt) in enumerate(parts):
        p = cnt * jnp.exp(s - m)
        den = den + jnp.sum(p, axis=-1, keepdims=True)
        vblk = vn_ref[...] if c == len(parts) - 1 else vc_ref[c * chunk:(c + 1) * chunk, :]
        acc = acc + jnp.dot(p.astype(BF16), vblk.astype(BF16), preferred_element_type=F32)
    o_full = jnp.where(own, acc / den, 0.0)
    out = o_full[0:t, :]
    for h in range(1, N_HEADS):
        out = out + o_full[h * t:(h + 1) * t, :]
    o_ref[...] = out


def _attn_sample(q, k_new, v_new, k_cache, v_cache):
    b, t, _ = q.shape
    n_past = k_cache.shape[1]
    new = lambda bi: (bi, 0, 0)
    return pl.pallas_call(
        functools.partial(_attn_sample_kernel, chunk=512),
        grid=(b,),
        in_specs=[pl.BlockSpec((None, t, ATTN_W), new),
                  pl.BlockSpec((None, t, ATTN_W), new),
                  pl.BlockSpec((None, t, ATTN_W), new),
                  pl.BlockSpec((None, n_past, ATTN_W), new),
                  pl.BlockSpec((None, n_past, ATTN_W), new)],
        out_specs=pl.BlockSpec((None, t, ATTN_W), new),
        out_shape=jax.ShapeDtypeStruct((b, t, ATTN_W), F32),
        compiler_params=_params(("parallel",)),
        name="attn_sample",
    )(q, k_new, v_new, k_cache, v_cache)


def _mix_kernel(attn_ref, cb_ref, z_ref, zprev_ref, state_ref, x_ref, cw_ref, ga_ref, gc_ref,
                wo_ref, g2_ref, h_ref, xn_ref, xnt_ref, zbuf, *, stride, tiles_per_seq):
    tm = z_ref.shape[0]
    hb = state_ref.shape[0]
    if tiles_per_seq > 1:
        first = (pl.program_id(0) % tiles_per_seq) == 0
        zbuf[0:hb, :] = jnp.where(first, state_ref[...], zprev_ref[...])
    else:
        zbuf[0:hb, :] = state_ref[...]
    z = z_ref[...]
    zbuf[hb:hb + tm, :] = z
    cw = cw_ref[...]
    y = (cw[0:1, :] * zbuf[hb - 2 * stride:hb - 2 * stride + tm, :]
         + cw[1:2, :] * zbuf[hb - stride:hb - stride + tm, :]
         + cw[2:3, :] * z)
    conv_out = cb_ref[...] * y
    a = _rms(attn_ref[...], ga_ref[...]).astype(BF16)
    c = _rms(conv_out, gc_ref[...]).astype(BF16)
    mixed = (jnp.dot(a, wo_ref[0:ATTN_W, :], preferred_element_type=F32)
             + jnp.dot(c, wo_ref[ATTN_W:ATTN_W + CONV_W, :], preferred_element_type=F32))
    h = x_ref[...] + mixed
    h_ref[...] = h
    xn = _rms(h, g2_ref[...])
    xn_ref[...] = xn.astype(BF16)
    xnt_ref[...] = xn.T.astype(BF16)


def _mix(attn, cb, z, state, x, conv_w, ga, gc, wo_bf, g2, tm, stride, tiles_per_seq):
    n, d = x.shape
    hb = state.shape[1]
    row = lambda i: (i, 0)
    const = lambda i: (0, 0)
    prev = lambda i: (jnp.maximum(i * (tm // hb) - 1, 0), 0)
    return pl.pallas_call(
        functools.partial(_mix_kernel, stride=stride, tiles_per_seq=tiles_per_seq),
        grid=(n // tm,),
        in_specs=[pl.BlockSpec((tm, ATTN_W), row),
                  pl.BlockSpec((tm, CONV_W), row),
                  pl.BlockSpec((tm, CONV_W), row),
                  pl.BlockSpec((hb, CONV_W), prev),
                  pl.BlockSpec((None, hb, CONV_W), lambda i: (i // tiles_per_seq, 0, 0)),
                  pl.BlockSpec((tm, d), row),
                  pl.BlockSpec((CONV_K, CONV_W), const),
                  pl.BlockSpec((1, ATTN_W), const),
                  pl.BlockSpec((1, CONV_W), const),
                  pl.BlockSpec((d, d), const),
                  pl.BlockSpec((1, d), const)],
        out_specs=[pl.BlockSpec((tm, d), row), pl.BlockSpec((tm, d), row),
                   pl.BlockSpec((d, tm), lambda i: (0, i))],
        out_shape=[jax.ShapeDtypeStruct((n, d), F32), jax.ShapeDtypeStruct((n, d), BF16),
                   jax.ShapeDtypeStruct((d, n), BF16)],
        scratch_shapes=[pltpu.VMEM((hb + tm, CONV_W), F32)],
        compiler_params=_params(("parallel",)),
        name="mix",
    )(attn, cb, z, z, state, x, conv_w, ga, gc, wo_bf, g2)


def _top16(s, order):
    rank = jnp.full(s.shape, float(PEER_TOPK), F32)
    vals = []
    for r in range(PEER_TOPK):
        m = jnp.max(s, axis=0, keepdims=True)
        if order is None:
            sel = s == m
        else:
            big = jnp.int32(2 ** 30)
            sel = order == jnp.min(jnp.where(s == m, order, big), axis=0, keepdims=True)
        rank = jnp.where(sel, float(r), rank)
        s = jnp.where(sel, -jnp.inf, s)
        vals.append(m)
    return rank, jnp.concatenate(vals, axis=0)


_CAND_GROUPS = ((0, 0), (0, 8), (1, 0), (2, 0), (3, 0), (4, 0), (5, 0), (6, 0), (7, 0))


def _route_tile(s1, s2, exact):
    tm = s1.shape[1]
    sub = 8
    iota128 = lax.broadcasted_iota(jnp.int32, s1.shape, 0) if exact else None
    rank1, sv1 = _top16(s1, iota128)
    rank2, sv2 = _top16(s2, iota128)
    groups = [sv1[a:a + 1, :] + sv2[b0:b0 + sub, :] for a, b0 in _CAND_GROUPS]
    groups.append(sv1[sub:2 * sub, :] + sv2[0:1, :])
    cand = jnp.concatenate(groups, axis=0)
    order = None
    if exact:
        i8 = lax.broadcasted_iota(jnp.int32, (sub, tm), 0)
        order = jnp.concatenate([a * PEER_TOPK + b0 + i8 for a, b0 in _CAND_GROUPS]
                                + [(sub + i8) * PEER_TOPK], axis=0)
    crank, _ = _top16(cand, order)
    chosen = crank < float(PEER_TOPK)
    cmax = sv1[0:1, :] + sv2[0:1, :]
    zsum = jnp.sum(jnp.where(chosen, jnp.exp(cand - cmax), 0.0), axis=0, keepdims=True)
    picked = jnp.where(chosen, 1.0, 0.0)
    n_of_a = [jnp.sum(picked[0:2 * sub, :], axis=0, keepdims=True)]
    for a in range(1, sub):
        n_of_a.append(jnp.sum(picked[(a + 1) * sub:(a + 2) * sub, :], axis=0, keepdims=True))
    last = (len(_CAND_GROUPS)) * sub
    for a in range(sub, PEER_TOPK):
        n_of_a.append(picked[last + a - sub:last + a - sub + 1, :])
    lb = jnp.zeros(rank1.shape, F32)
    for a in range(PEER_TOPK):
        lb = jnp.where(rank1 == float(a), n_of_a[a], lb)
    a2 = jnp.exp(s2 - sv2[0:1, :])
    a1 = jnp.exp(s1 - sv1[0:1, :]) / zsum
    tied = None
    if not exact:
        def extra(rk):
            n_sel = jnp.sum(jnp.where(rk < float(PEER_TOPK), 1.0, 0.0), axis=0, keepdims=True)
            return jnp.max(n_sel) > float(PEER_TOPK)
        tied = extra(rank1) | extra(rank2) | extra(crank)
    return rank2, a2, lb, a1, tied


def _route_kernel(xn_ref, wq_ref, keys_ref, r2_ref, a2_ref, lb_ref, a1_ref, q_scr):
    h = pl.program_id(1)
    nsub = 2 * PEER_HEADS

    @pl.when(h == 0)
    def _():
        q = jnp.dot(xn_ref[...], wq_ref[...], preferred_element_type=F32)
        for c in range(nsub):
            q_scr[c] = q[:, c * PEER_NKEYS:(c + 1) * PEER_NKEYS].astype(BF16)

    s1 = lax.dot_general(keys_ref[0], q_scr[2 * h], _NT, preferred_element_type=F32)
    s2 = lax.dot_general(keys_ref[1], q_scr[2 * h + 1], _NT, preferred_element_type=F32)

    def twice(x):
        hi = lax.bitcast_convert_type(x.astype(BF16).astype(F32), jnp.uint32)
        return hi | (hi >> 16)

    def emit(rank2, a2, lb, a1):
        r2_ref[...] = _pack_pairs(rank2)
        a2_ref[...] = _pack_pairs(a2)
        lb_ref[...] = twice(lb)
        a1_ref[...] = twice(a1)

    *outs, tied = _route_tile(s1, s2, exact=False)
    emit(*outs)

    @pl.when(tied)
    def _():
        emit(*_route_tile(s1, s2, exact=True)[:4])


def _route(xn, wq_bf, keys_bf, tm):
    n, d = xn.shape
    qd = wq_bf.shape[1]
    out_spec = lambda rows: pl.BlockSpec((None, rows, tm), lambda i, h: (h, 0, i))
    out_sds = lambda rows: jax.ShapeDtypeStruct((PEER_HEADS, rows, n), jnp.uint32)
    half = PEER_NKEYS // 2
    return pl.pallas_call(
        _route_kernel,
        grid=(n // tm, PEER_HEADS),
        in_specs=[pl.BlockSpec((tm, d), lambda i, h: (i, 0)),
                  pl.BlockSpec((d, qd), lambda i, h: (0, 0)),
                  pl.BlockSpec((None, 2, PEER_NKEYS, PEER_NKEYS), lambda i, h: (h, 0, 0, 0))],
        out_specs=[out_spec(half), out_spec(half), out_spec(PEER_NKEYS), out_spec(PEER_NKEYS)],
        out_shape=[out_sds(half), out_sds(half), out_sds(PEER_NKEYS), out_sds(PEER_NKEYS)],
        scratch_shapes=[pltpu.VMEM((2 * PEER_HEADS, tm, PEER_NKEYS), BF16)],
        compiler_params=_params(("parallel", "arbitrary")),
        name="peer_route",
    )(xn, wq_bf, keys_bf)


def _peer_kernel(xn_ref, u_ref, vt_ref, r2_ref, a2_ref, lb_ref, a1_ref, h_ref, gf_ref,
                 y_ref, acc_ref, *, final_norm):
    j = pl.program_id(1)
    te = vt_ref.shape[1]

    @pl.when(j == 0)
    def _():
        acc_ref[...] = jnp.zeros_like(acc_ref)

    tm = xn_ref.shape[1]

    def row_bf16(ref, hh, blk):
        word = jnp.broadcast_to(ref[hh, blk:blk + 1, :], (8, tm))
        tile = pltpu.bitcast(word, BF16)
        return jnp.concatenate([tile] * (PEER_NKEYS // tile.shape[0]), axis=0)

    sub = min(te, SUB_EXPERTS)
    ps = []
    for s0 in range(0, te, sub):
        u_rows = pltpu.bitcast(u_ref[s0 // 2:(s0 + sub) // 2, :], BF16)
        ht = jnp.dot(u_rows, xn_ref[...], preferred_element_type=F32)
        act = 0.5 * ht * (1.0 + lax.erf(ht * (1.0 / math.sqrt(2.0))))
        gates = []
        for bb in range(s0 // PEER_NKEYS, (s0 + sub) // PEER_NKEYS):
            g = jnp.zeros((PEER_NKEYS, tm), BF16)
            for hh in range(PEER_HEADS):
                lim = row_bf16(lb_ref, hh, bb)
                w1 = row_bf16(a1_ref, hh, bb)
                r2 = pltpu.bitcast(r2_ref[hh], BF16)
                a2 = pltpu.bitcast(a2_ref[hh], BF16)
                g = g + jnp.where(r2 < lim, w1 * a2, jnp.zeros_like(g))
            gates.append(g)
        ps.append(act.astype(BF16) * jnp.concatenate(gates, axis=0))
    p = jnp.concatenate(ps, axis=0)
    acc_ref[...] += jnp.dot(vt_ref[...], p, preferred_element_type=F32)

    @pl.when(j == pl.num_programs(1) - 1)
    def _():
        out = h_ref[...] + acc_ref[...].T
        y_ref[...] = _rms(out, gf_ref[...]) if final_norm else out


def _peer(xnt, u_words, vt_bf, r2, a2, lb, a1, h, gf, final_norm, tm, te):
    d, n = xnt.shape
    ne = 2 * u_words.shape[0]
    gate_spec = pl.BlockSpec((PEER_HEADS, PEER_NKEYS // 2, tm), lambda i, j: (0, 0, i))
    nblk = te // PEER_NKEYS
    assert nblk % 8 == 0, "first-key rows are delivered as whole sublane tiles"
    row_spec = pl.BlockSpec((PEER_HEADS, nblk, tm), lambda i, j: (0, j, i))
    return pl.pallas_call(
        functools.partial(_peer_kernel, final_norm=final_norm),
        grid=(n // tm, ne // te),
        in_specs=[pl.BlockSpec((d, tm), lambda i, j: (0, i)),
                  pl.BlockSpec((te // 2, d), lambda i, j: (j, 0)),
                  pl.BlockSpec((d, te), lambda i, j: (0, j)),
                  gate_spec, gate_spec, row_spec, row_spec,
                  pl.BlockSpec((tm, d), lambda i, j: (i, 0)),
                  pl.BlockSpec((1, d), lambda i, j: (0, 0))],
        out_specs=pl.BlockSpec((tm, d), lambda i, j: (i, 0)),
        out_shape=jax.ShapeDtypeStruct((n, d), F32),
        scratch_shapes=[pltpu.VMEM((d, tm), F32)],
        compiler_params=_params(("parallel", "arbitrary")),
        name="peer_experts",
    )(xnt, u_words, vt_bf, r2, a2, lb, a1, h, gf)


def _tile(n, pref):
    t = min(n, pref)
    assert n % t == 0, (n, t)
    return t


def kernel(x_prompt, x_sample, cache_win_k, cache_win_v, state_conv, norm1_g, w_in, conv_w,
           attn_out_g, conv_out_g, w_out, norm2_g, peer_wq, peer_sub_keys, peer_u, peer_v,
           final_norm_g):
    bp, s, d = x_prompt.shape
    bs, t, _ = x_sample.shape
    depth = w_in.shape[0]
    past_len = 8192
    win_p = min(BRANCHES[-1][0], s)
    np_, ns = bp * s, bs * t

    tm_p = _tile(s, 512)
    tq = _tile(s, 256)
    cos_p, sin_p = _rope_tables(jnp.arange(s, dtype=jnp.int32))
    pos_s = past_len + jnp.repeat(jnp.arange(t, dtype=jnp.int32), bs)
    cos_s, sin_s = _rope_tables(pos_s)

    hp = x_prompt.reshape(np_, d)
    hs = x_sample.transpose(1, 0, 2).reshape(ns, d)
    outs = {k: [] for k in ("kp", "vp", "cp", "ks", "vs", "cs")}
    row = lambda g: g.reshape(1, -1)
    y_p = y_s = None
    for l in range(depth):
        w_in_bf = w_in[l].astype(BF16)
        w_out_bf = w_out[l].astype(BF16)
        wq_bf = peer_wq[l].astype(BF16)
        keys_bf = peer_sub_keys[l].astype(BF16)
        u_words = _pack_pairs(peer_u[l])
        vt_bf = _pair_order(peer_v[l]).T.astype(BF16)
        last = l == depth - 1

        def tail(hres, attn, cb, z, state, stride, tiles_per_seq, tm):
            h_mid, xn2, xn2t = _mix(attn, cb, z, state, hres, conv_w[l], row(attn_out_g[l]),
                                    row(conv_out_g[l]), w_out_bf, row(norm2_g[l]), tm, stride,
                                    tiles_per_seq)
            n = hres.shape[0]
            r2, a2, lb, a1 = _route(xn2, wq_bf, keys_bf, _tile(n, 256))
            return _peer(xn2t, u_words, vt_bf, r2, a2, lb, a1, h_mid, row(final_norm_g), last,
                         _tile(n, 512), 2048)

        q, k, v, cb, z = _inproj(hp, row(norm1_g[l]), w_in_bf, cos_p, sin_p, tm_p)
        attn = _attn_prompt(q.reshape(bp, s, ATTN_W), k.reshape(bp, s, ATTN_W),
                            v.reshape(bp, s, ATTN_W), tq).reshape(np_, ATTN_W)
        k4 = k.reshape(bp, s, N_HEADS, HEAD_DIM)
        v4 = v.reshape(bp, s, N_HEADS, HEAD_DIM)
        outs["kp"].append(k4[:, s - win_p:])
        outs["vp"].append(v4[:, s - win_p:])
        outs["cp"].append(z.reshape(bp, s, CONV_W)[:, s - (CONV_K - 1):])
        zero_state = jnp.zeros((bp, 8, CONV_W), F32)
        y_p = tail(hp, attn, cb, z, zero_state, 1, s // tm_p, tm_p)

        q, k, v, cb, z = _inproj(hs, row(norm1_g[l]), w_in_bf, cos_s, sin_s, _tile(ns, 512))
        to_bt = lambda a: a.reshape(t, bs, -1).transpose(1, 0, 2)
        k_bt, v_bt = to_bt(k), to_bt(v)
        attn = _attn_sample(to_bt(q), k_bt, v_bt,
                            cache_win_k[l].reshape(bs, -1, ATTN_W),
                            cache_win_v[l].reshape(bs, -1, ATTN_W))
        attn = attn.transpose(1, 0, 2).reshape(ns, ATTN_W)
        outs["ks"].append(k_bt.reshape(bs, t, N_HEADS, HEAD_DIM))
        outs["vs"].append(v_bt.reshape(bs, t, N_HEADS, HEAD_DIM))
        outs["cs"].append(to_bt(z)[:, t - (CONV_K - 1):])
        state = state_conv[l].transpose(1, 0, 2).reshape(1, (CONV_K - 1) * bs, CONV_W)
        y_s = tail(hs, attn, cb, z, state, bs, 1, ns)
        hp, hs = y_p, y_s

    y_prompt = y_p.reshape(bp, s, d)
    y_sample = y_s.reshape(t, bs, d).transpose(1, 0, 2)
    st = lambda name: jnp.stack(outs[name])
    return (y_prompt, y_sample, st("kp"), st("vp"), st("cp"), st("ks"), st("vs"), st("cs"))
```

```python
import functools
import math

import jax
import jax.numpy as jnp
from jax import lax
from jax.experimental import pallas as pl
from jax.experimental.pallas import tpu as pltpu

F32 = jnp.float32
BF16 = jnp.bfloat16

HEAD_DIM = 64
N_HEADS = 12
ATTN_W = N_HEADS * HEAD_DIM
CONV_W = 256
CONV_K = 3
BRANCHES = ((128, 1), (512, 4), (2048, 16))
ROPE_THETA = 10000.0
EPS = 1e-6
PEER_HEADS = 8
PEER_NKEYS = 128
PEER_TOPK = 16
NEG = -1e30
LANES = 128
SUB_EXPERTS = 256
VMEM_LIMIT = 56 * 1024 * 1024

_NT = (((1,), (1,)), ((), ()))


def _rms(x, g):
    return x * lax.rsqrt(jnp.mean(x * x, axis=-1, keepdims=True) + EPS) * g


def _pack_pairs(x):
    r, n = x.shape
    bits = lax.bitcast_convert_type(x.astype(BF16).astype(F32), jnp.uint32) >> 16
    bits = bits.reshape(r // 16, 2, 8, n)
    return (bits[:, 0] | (bits[:, 1] << 16)).reshape(r // 2, n)


def _pair_order(x):
    r = x.shape[0]
    return x.reshape(r // 16, 2, 8, -1).transpose(0, 2, 1, 3).reshape(x.shape)


def _params(sem):
    return pltpu.CompilerParams(dimension_semantics=sem, vmem_limit_bytes=VMEM_LIMIT)


def _inproj_kernel(x_ref, g_ref, w_ref, cos_ref, sin_ref,
                   q_ref, k_ref, v_ref, cb_ref, z_ref):
    xb = _rms(x_ref[...], g_ref[...]).astype(BF16)
    tm = xb.shape[0]
    lane = lax.broadcasted_iota(jnp.int32, (tm, LANES), 1)
    low_half = (lane % HEAD_DIM) < (HEAD_DIM // 2)
    cos = cos_ref[...]
    sin = sin_ref[...]

    def rope(t):
        partner = jnp.where(low_half, pltpu.roll(t, LANES - HEAD_DIM // 2, 1),
                            pltpu.roll(t, HEAD_DIM // 2, 1))
        return t * cos + partner * sin

    scale = 1.0 / math.sqrt(HEAD_DIM)
    q = jnp.dot(xb, w_ref[:, 0:ATTN_W], preferred_element_type=F32)
    for c in range(ATTN_W // LANES):
        sl = slice(c * LANES, (c + 1) * LANES)
        q_ref[:, sl] = (rope(q[:, sl]) * scale).astype(BF16)
    k = jnp.dot(xb, w_ref[:, ATTN_W:2 * ATTN_W], preferred_element_type=F32)
    for c in range(ATTN_W // LANES):
        sl = slice(c * LANES, (c + 1) * LANES)
        k_ref[:, sl] = rope(k[:, sl])
    v_ref[...] = jnp.dot(xb, w_ref[:, 2 * ATTN_W:3 * ATTN_W], preferred_element_type=F32)
    c0 = 3 * ATTN_W
    conv = jnp.dot(xb, w_ref[:, c0:c0 + 3 * CONV_W], preferred_element_type=F32)
    cb_ref[...] = conv[:, 0:CONV_W]
    z_ref[...] = conv[:, CONV_W:2 * CONV_W] * conv[:, 2 * CONV_W:3 * CONV_W]


def _inproj(x, g, w_bf, cos, sin, tm):
    n, d = x.shape
    win = w_bf.shape[1]
    period = cos.shape[0] // tm
    row = lambda i: (i, 0)
    const = lambda i: (0, 0)
    return pl.pallas_call(
        _inproj_kernel,
        grid=(n // tm,),
        in_specs=[pl.BlockSpec((tm, d), row),
                  pl.BlockSpec((1, d), const),
                  pl.BlockSpec((d, win), const),
                  pl.BlockSpec((tm, LANES), lambda i: (i % period, 0)),
                  pl.BlockSpec((tm, LANES), lambda i: (i % period, 0))],
        out_specs=[pl.BlockSpec((tm, ATTN_W), row), pl.BlockSpec((tm, ATTN_W), row),
                   pl.BlockSpec((tm, ATTN_W), row), pl.BlockSpec((tm, CONV_W), row),
                   pl.BlockSpec((tm, CONV_W), row)],
        out_shape=[jax.ShapeDtypeStruct((n, ATTN_W), BF16),
                   jax.ShapeDtypeStruct((n, ATTN_W), F32),
                   jax.ShapeDtypeStruct((n, ATTN_W), F32),
                   jax.ShapeDtypeStruct((n, CONV_W), F32),
                   jax.ShapeDtypeStruct((n, CONV_W), F32)],
        compiler_params=_params(("parallel",)),
        name="inproj",
    )(x, g, w_bf, cos, sin)


def _rope_tables(pos):
    half = HEAD_DIM // 2
    inv = jnp.exp(-math.log(ROPE_THETA) * jnp.arange(half, dtype=F32) * (2.0 / HEAD_DIM))
    ang = pos.astype(F32)[:, None] * inv[None, :]
    cos = jnp.cos(ang)
    sin = jnp.sin(ang)
    reps = LANES // HEAD_DIM
    cos_t = jnp.tile(jnp.concatenate([cos, cos], axis=1), (1, reps))
    sin_t = jnp.tile(jnp.concatenate([-sin, sin], axis=1), (1, reps))
    return cos_t, sin_t


def _branch_count(dist):
    cnt = jnp.zeros(dist.shape, F32)
    for window, dil in BRANCHES:
        hit = (dist <= window) & ((dist & (dil - 1)) == 0)
        cnt = cnt + jnp.where(hit, 1.0, 0.0)
    return jnp.where(dist >= 0, cnt, 0.0)


def _attn_prompt_kernel(q_ref, k_ref, v_ref, bias_ref, o_ref,
                        kt_ref, vz_ref, m_ref, l_ref, acc_ref, *, tq):
    qi = pl.program_id(2)
    heads = LANES // HEAD_DIM
    lane = lax.broadcasted_iota(jnp.int32, (tq, LANES), 1)
    head_of_lane = lane // HEAD_DIM

    @pl.when(qi == 0)
    def _():
        for jb in range(k_ref.shape[0] // tq):
            rows = slice(jb * tq, (jb + 1) * tq)
            kt_ref[jb] = k_ref[rows, :].T.astype(BF16)
            vb = v_ref[rows, :]
            for h in range(heads):
                vz_ref[heads * jb + h] = jnp.where(head_of_lane == h, vb, 0.0).astype(BF16)

    q = q_ref[...]
    qz = [jnp.where(head_of_lane == h, q, jnp.zeros_like(q)) for h in range(heads)]
    m_ref[...] = jnp.full(m_ref.shape, NEG, F32)
    l_ref[...] = jnp.zeros(l_ref.shape, F32)
    acc_ref[...] = jnp.zeros(acc_ref.shape, F32)

    def body(j, _):
        bias = bias_ref[qi - j]
        kt = kt_ref[j]
        alphas = []
        pv = None
        for h in range(heads):
            s = jnp.dot(qz[h], kt, preferred_element_type=F32) + bias
            m_prev = m_ref[h]
            m_next = jnp.maximum(m_prev, jnp.max(s, axis=-1, keepdims=True))
            p = jnp.concatenate(
                [jnp.exp(s[:, c * LANES:(c + 1) * LANES] - m_next) for c in range(tq // LANES)],
                axis=-1)
            alpha = jnp.exp(m_prev - m_next)
            l_ref[h] = alpha * l_ref[h] + jnp.sum(p, axis=-1, keepdims=True)
            m_ref[h] = m_next
            alphas.append(alpha)
            d = jnp.dot(p.astype(BF16), vz_ref[heads * j + h], preferred_element_type=F32)
            pv = d if pv is None else pv + d
        alpha_both = jnp.where(head_of_lane == 0, alphas[0], alphas[1])
        acc_ref[...] = alpha_both * acc_ref[...] + pv
        return 0

    lax.fori_loop(0, qi + 1, body, 0)
    l_both = jnp.where(head_of_lane == 0, l_ref[0], l_ref[1])
    o_ref[...] = acc_ref[...] / l_both


def _attn_bias(s, tq):
    d = jnp.arange(s // tq, dtype=jnp.int32)[:, None, None] * tq
    dist = d + jnp.arange(tq, dtype=jnp.int32)[None, :, None] \
        - jnp.arange(tq, dtype=jnp.int32)[None, None, :]
    cnt = _branch_count(dist)
    return jnp.where(cnt > 0.0, jnp.log(jnp.maximum(cnt, 1.0)), NEG)


def _attn_prompt(q, k, v, tq):
    b, s, _ = q.shape
    heads = LANES // HEAD_DIM
    assert heads == 2 and tq % LANES == 0
    nkb = s // tq
    blk = lambda bi, hp, qi: (bi, qi, hp)
    full = lambda bi, hp, qi: (bi, 0, hp)
    return pl.pallas_call(
        functools.partial(_attn_prompt_kernel, tq=tq),
        grid=(b, ATTN_W // LANES, nkb),
        in_specs=[pl.BlockSpec((None, tq, LANES), blk),
                  pl.BlockSpec((None, s, LANES), full),
                  pl.BlockSpec((None, s, LANES), full),
                  pl.BlockSpec((nkb, tq, tq), lambda bi, hp, qi: (0, 0, 0))],
        out_specs=pl.BlockSpec((None, tq, LANES), blk),
        out_shape=jax.ShapeDtypeStruct((b, s, ATTN_W), F32),
        scratch_shapes=[pltpu.VMEM((nkb, LANES, tq), BF16),
                        pltpu.VMEM((nkb * heads, tq, LANES), BF16),
                        pltpu.VMEM((heads, tq, LANES), F32),
                        pltpu.VMEM((heads, tq, LANES), F32),
                        pltpu.VMEM((tq, LANES), F32)],
        compiler_params=_params(("parallel", "parallel", "arbitrary")),
        name="attn_prompt",
    )(q, k, v, _attn_bias(s, tq))


def _attn_sample_kernel(q_ref, kn_ref, vn_ref, kc_ref, vc_ref, o_ref, *, chunk):
    t = q_ref.shape[0]
    n_past = kc_ref.shape[0]
    rows = N_HEADS * t
    qt = jnp.concatenate([q_ref[...]] * N_HEADS, axis=0)
    r_id = lax.broadcasted_iota(jnp.int32, (rows, ATTN_W), 0)
    c_id = lax.broadcasted_iota(jnp.int32, (rows, ATTN_W), 1)
    own = (r_id // t) == (c_id // HEAD_DIM)
    qbd = jnp.where(own, qt, jnp.zeros_like(qt))

    def scores(kblk, first_key):
        nk = kblk.shape[0]
        s = lax.dot_general(qbd, kblk.astype(BF16), _NT, preferred_element_type=F32)
        qpos = n_past + lax.broadcasted_iota(jnp.int32, (rows, nk), 0) % t
        kpos = first_key + lax.broadcasted_iota(jnp.int32, (rows, nk), 1)
        cnt = _branch_count(qpos - kpos)
        return jnp.where(cnt > 0.0, s, NEG), cnt

    parts = [scores(kc_ref[c * chunk:(c + 1) * chunk, :], c * chunk)
             for c in range(n_past // chunk)]
    parts.append(scores(kn_ref[...], n_past))
    m = functools.reduce(jnp.maximum, [jnp.max(s, axis=-1, keepdims=True) for s, _ in parts])
    den = jnp.zeros((rows, 1), F32)
    acc = jnp.zeros((rows, ATTN_W), F32)
    for c, (s, cnt) in enumerate(parts):
        p = cnt * jnp.exp(s - m)
        den = den + jnp.sum(p, axis=-1, keepdims=True)
        vblk = vn_ref[...] if c == len(parts) - 1 else vc_ref[c * chunk:(c + 1) * chunk, :]
        acc = acc + jnp.dot(p.astype(BF16), vblk.astype(BF16), preferred_element_type=F32)
    o_full = jnp.where(own, acc / den, 0.0)
    out = o_full[0:t, :]
    for h in range(1, N_HEADS):
        out = out + o_full[h * t:(h + 1) * t, :]
    o_ref[...] = out


def _attn_sample(q, k_new, v_new, k_cache, v_cache):
    b, t, _ = q.shape
    n_past = k_cache.shape[1]
    new = lambda bi: (bi, 0, 0)
    return pl.pallas_call(
        functools.partial(_attn_sample_kernel, chunk=512),
        grid=(b,),
        in_specs=[pl.BlockSpec((None, t, ATTN_W), new),
                  pl.BlockSpec((None, t, ATTN_W), new),
                  pl.BlockSpec((None, t, ATTN_W), new),
                  pl.BlockSpec((None, n_past, ATTN_W), new),
                  pl.BlockSpec((None, n_past, ATTN_W), new)],
        out_specs=pl.BlockSpec((None, t, ATTN_W), new),
        out_shape=jax.ShapeDtypeStruct((b, t, ATTN_W), F32),
        compiler_params=_params(("parallel",)),
        name="attn_sample",
    )(q, k_new, v_new, k_cache, v_cache)


def _mix_kernel(attn_ref, cb_ref, z_ref, zprev_ref, state_ref, x_ref, cw_ref, ga_ref, gc_ref,
                wo_ref, g2_ref, h_ref, xn_ref, xnt_ref, zbuf, *, stride, tiles_per_seq):
    tm = z_ref.shape[0]
    hb = state_ref.shape[0]
    if tiles_per_seq > 1:
        first = (pl.program_id(0) % tiles_per_seq) == 0
        zbuf[0:hb, :] = jnp.where(first, state_ref[...], zprev_ref[...])
    else:
        zbuf[0:hb, :] = state_ref[...]
    z = z_ref[...]
    zbuf[hb:hb + tm, :] = z
    cw = cw_ref[...]
    y = (cw[0:1, :] * zbuf[hb - 2 * stride:hb - 2 * stride + tm, :]
         + cw[1:2, :] * zbuf[hb - stride:hb - stride + tm, :]
         + cw[2:3, :] * z)
    conv_out = cb_ref[...] * y
    a = _rms(attn_ref[...], ga_ref[...]).astype(BF16)
    c = _rms(conv_out, gc_ref[...]).astype(BF16)
    mixed = (jnp.dot(a, wo_ref[0:ATTN_W, :], preferred_element_type=F32)
             + jnp.dot(c, wo_ref[ATTN_W:ATTN_W + CONV_W, :], preferred_element_type=F32))
    h = x_ref[...] + mixed
    h_ref[...] = h
    xn = _rms(h, g2_ref[...])
    xn_ref[...] = xn.astype(BF16)
    xnt_ref[...] = xn.T.astype(BF16)


def _mix(attn, cb, z, state, x, conv_w, ga, gc, wo_bf, g2, tm, stride, tiles_per_seq):
    n, d = x.shape
    hb = state.shape[1]
    row = lambda i: (i, 0)
    const = lambda i: (0, 0)
    prev = lambda i: (jnp.maximum(i * (tm // hb) - 1, 0), 0)
    return pl.pallas_call(
        functools.partial(_mix_kernel, stride=stride, tiles_per_seq=tiles_per_seq),
        grid=(n // tm,),
        in_specs=[pl.BlockSpec((tm, ATTN_W), row),
                  pl.BlockSpec((tm, CONV_W), row),
                  pl.BlockSpec((tm, CONV_W), row),
                  pl.BlockSpec((hb, CONV_W), prev),
                  pl.BlockSpec((None, hb, CONV_W), lambda i: (i // tiles_per_seq, 0, 0)),
                  pl.BlockSpec((tm, d), row),
                  pl.BlockSpec((CONV_K, CONV_W), const),
                  pl.BlockSpec((1, ATTN_W), const),
                  pl.BlockSpec((1, CONV_W), const),
                  pl.BlockSpec((d, d), const),
                  pl.BlockSpec((1, d), const)],
        out_specs=[pl.BlockSpec((tm, d), row), pl.BlockSpec((tm, d), row),
                   pl.BlockSpec((d, tm), lambda i: (0, i))],
        out_shape=[jax.ShapeDtypeStruct((n, d), F32), jax.ShapeDtypeStruct((n, d), BF16),
                   jax.ShapeDtypeStruct((d, n), BF16)],
        scratch_shapes=[pltpu.VMEM((hb + tm, CONV_W), F32)],
        compiler_params=_params(("parallel",)),
        name="mix",
    )(attn, cb, z, z, state, x, conv_w, ga, gc, wo_bf, g2)


def _top16(s, order):
    unit = 2.0 ** 120
    vals = []
    for r in range(PEER_TOPK):
        m = jnp.max(s, axis=0, keepdims=True)
        if order is None:
            sel = s == m
        else:
            big = jnp.int32(2 ** 30)
            sel = order == jnp.min(jnp.where(s == m, order, big), axis=0, keepdims=True)
        s = jnp.where(sel, -(128.0 + r) * unit, s)
        vals.append(m)
    rank = jnp.where(s <= -128.0 * unit, s * (-1.0 / unit) - 128.0, float(PEER_TOPK))
    return rank, jnp.concatenate(vals, axis=0)


_CAND_GROUPS = ((0, 0), (0, 8), (1, 0), (2, 0), (3, 0), (4, 0), (5, 0), (6, 0), (7, 0))


def _route_tile(s1, s2, exact):
    tm = s1.shape[1]
    sub = 8
    iota128 = lax.broadcasted_iota(jnp.int32, s1.shape, 0) if exact else None
    rank1, sv1 = _top16(s1, iota128)
    rank2, sv2 = _top16(s2, iota128)
    groups = [sv1[a:a + 1, :] + sv2[b0:b0 + sub, :] for a, b0 in _CAND_GROUPS]
    groups.append(sv1[sub:2 * sub, :] + sv2[0:1, :])
    cand = jnp.concatenate(groups, axis=0)
    order = None
    if exact:
        i8 = lax.broadcasted_iota(jnp.int32, (sub, tm), 0)
        order = jnp.concatenate([a * PEER_TOPK + b0 + i8 for a, b0 in _CAND_GROUPS]
                                + [(sub + i8) * PEER_TOPK], axis=0)
    crank, _ = _top16(cand, order)
    chosen = crank < float(PEER_TOPK)
    cmax = sv1[0:1, :] + sv2[0:1, :]
    zsum = jnp.sum(jnp.where(chosen, jnp.exp(cand - cmax), 0.0), axis=0, keepdims=True)
    picked = jnp.where(chosen, 1.0, 0.0)
    n_of_a = [jnp.sum(picked[0:2 * sub, :], axis=0, keepdims=True)]
    for a in range(1, sub):
        n_of_a.append(jnp.sum(picked[(a + 1) * sub:(a + 2) * sub, :], axis=0, keepdims=True))
    last = (len(_CAND_GROUPS)) * sub
    for a in range(sub, PEER_TOPK):
        n_of_a.append(picked[last + a - sub:last + a - sub + 1, :])
    lb = jnp.zeros(rank1.shape, F32)
    for a in range(PEER_TOPK):
        lb = jnp.where(rank1 == float(a), n_of_a[a], lb)
    a2 = jnp.exp(s2 - sv2[0:1, :])
    a1 = jnp.exp(s1 - sv1[0:1, :]) / zsum
    tied = None
    if not exact:
        def extra(rk):
            n_sel = jnp.sum(jnp.where(rk < float(PEER_TOPK), 1.0, 0.0), axis=0, keepdims=True)
            return jnp.max(n_sel) > float(PEER_TOPK)
        tied = extra(rank1) | extra(rank2) | extra(crank)
    return rank2, a2, lb, a1, tied


def _route_kernel(xn_ref, wq_ref, keys_ref, r2_ref, a2_ref, lb_ref, a1_ref, q_scr):
    h = pl.program_id(1)
    nsub = 2 * PEER_HEADS

    @pl.when(h == 0)
    def _():
        q = jnp.dot(xn_ref[...], wq_ref[...], preferred_element_type=F32)
        for c in range(nsub):
            q_scr[c] = q[:, c * PEER_NKEYS:(c + 1) * PEER_NKEYS].astype(BF16)

    s1 = lax.dot_general(keys_ref[0], q_scr[2 * h], _NT, preferred_element_type=F32)
    s2 = lax.dot_general(keys_ref[1], q_scr[2 * h + 1], _NT, preferred_element_type=F32)

    def twice(x):
        hi = lax.bitcast_convert_type(x.astype(BF16).astype(F32), jnp.uint32)
        return hi | (hi >> 16)

    def emit(rank2, a2, lb, a1):
        r2_ref[...] = _pack_pairs(rank2)
        a2_ref[...] = _pack_pairs(a2)
        lb_ref[...] = twice(lb)
        a1_ref[...] = twice(a1)

    *outs, tied = _route_tile(s1, s2, exact=False)
    emit(*outs)

    @pl.when(tied)
    def _():
        emit(*_route_tile(s1, s2, exact=True)[:4])


def _route(xn, wq_bf, keys_bf, tm):
    n, d = xn.shape
    qd = wq_bf.shape[1]
    out_spec = lambda rows: pl.BlockSpec((None, rows, tm), lambda i, h: (h, 0, i))
    out_sds = lambda rows: jax.ShapeDtypeStruct((PEER_HEADS, rows, n), jnp.uint32)
    half = PEER_NKEYS // 2
    return pl.pallas_call(
        _route_kernel,
        grid=(n // tm, PEER_HEADS),
        in_specs=[pl.BlockSpec((tm, d), lambda i, h: (i, 0)),
                  pl.BlockSpec((d, qd), lambda i, h: (0, 0)),
                  pl.BlockSpec((None, 2, PEER_NKEYS, PEER_NKEYS), lambda i, h: (h, 0, 0, 0))],
        out_specs=[out_spec(half), out_spec(half), out_spec(PEER_NKEYS), out_spec(PEER_NKEYS)],
        out_shape=[out_sds(half), out_sds(half), out_sds(PEER_NKEYS), out_sds(PEER_NKEYS)],
        scratch_shapes=[pltpu.VMEM((2 * PEER_HEADS, tm, PEER_NKEYS), BF16)],
        compiler_params=_params(("parallel", "arbitrary")),
        name="peer_route",
    )(xn, wq_bf, keys_bf)


def _peer_kernel(xn_ref, u_ref, vt_ref, r2_ref, a2_ref, lb_ref, a1_ref, h_ref, gf_ref,
                 y_ref, acc_ref, *, final_norm):
    j = pl.program_id(1)
    te = vt_ref.shape[1]

    @pl.when(j == 0)
    def _():
        acc_ref[...] = jnp.zeros_like(acc_ref)

    tm = xn_ref.shape[1]

    def row_bf16(ref, hh, blk):
        word = jnp.broadcast_to(ref[hh, blk:blk + 1, :], (8, tm))
        tile = pltpu.bitcast(word, BF16)
        return jnp.concatenate([tile] * (PEER_NKEYS // tile.shape[0]), axis=0)

    sub = min(te, SUB_EXPERTS)
    ps = []
    for s0 in range(0, te, sub):
        u_rows = pltpu.bitcast(u_ref[s0 // 2:(s0 + sub) // 2, :], BF16)
        ht = jnp.dot(u_rows, xn_ref[...], preferred_element_type=F32)
        act = 0.5 * ht * (1.0 + lax.erf(ht * (1.0 / math.sqrt(2.0))))
        gates = []
        for bb in range(s0 // PEER_NKEYS, (s0 + sub) // PEER_NKEYS):
            g = jnp.zeros((PEER_NKEYS, tm), BF16)
            for hh in range(PEER_HEADS):
                lim = row_bf16(lb_ref, hh, bb)
                w1 = row_bf16(a1_ref, hh, bb)
                r2 = pltpu.bitcast(r2_ref[hh], BF16)
                a2 = pltpu.bitcast(a2_ref[hh], BF16)
                g = g + jnp.where(r2 < lim, w1 * a2, jnp.zeros_like(g))
            gates.append(g)
        ps.append(act.astype(BF16) * jnp.concatenate(gates, axis=0))
    p = jnp.concatenate(ps, axis=0)
    acc_ref[...] += jnp.dot(vt_ref[...], p, preferred_element_type=F32)

    @pl.when(j == pl.num_programs(1) - 1)
    def _():
        out = h_ref[...] + acc_ref[...].T
        y_ref[...] = _rms(out, gf_ref[...]) if final_norm else out


def _peer(xnt, u_words, vt_bf, r2, a2, lb, a1, h, gf, final_norm, tm, te):
    d, n = xnt.shape
    ne = 2 * u_words.shape[0]
    gate_spec = pl.BlockSpec((PEER_HEADS, PEER_NKEYS // 2, tm), lambda i, j: (0, 0, i))
    nblk = te // PEER_NKEYS
    assert nblk % 8 == 0, "first-key rows are delivered as whole sublane tiles"
    row_spec = pl.BlockSpec((PEER_HEADS, nblk, tm), lambda i, j: (0, j, i))
    return pl.pallas_call(
        functools.partial(_peer_kernel, final_norm=final_norm),
        grid=(n // tm, ne // te),
        in_specs=[pl.BlockSpec((d, tm), lambda i, j: (0, i)),
                  pl.BlockSpec((te // 2, d), lambda i, j: (j, 0)),
                  pl.BlockSpec((d, te), lambda i, j: (0, j)),
                  gate_spec, gate_spec, row_spec, row_spec,
                  pl.BlockSpec((tm, d), lambda i, j: (i, 0)),
                  pl.BlockSpec((1, d), lambda i, j: (0, 0))],
        out_specs=pl.BlockSpec((tm, d), lambda i, j: (i, 0)),
        out_shape=jax.ShapeDtypeStruct((n, d), F32),
        scratch_shapes=[pltpu.VMEM((d, tm), F32)],
        compiler_params=_params(("parallel", "arbitrary")),
        name="peer_experts",
    )(xnt, u_words, vt_bf, r2, a2, lb, a1, h, gf)


def _tile(n, pref):
    t = min(n, pref)
    assert n % t == 0, (n, t)
    return t


def kernel(x_prompt, x_sample, cache_win_k, cache_win_v, state_conv, norm1_g, w_in, conv_w,
           attn_out_g, conv_out_g, w_out, norm2_g, peer_wq, peer_sub_keys, peer_u, peer_v,
           final_norm_g):
    bp, s, d = x_prompt.shape
    bs, t, _ = x_sample.shape
    depth = w_in.shape[0]
    past_len = 8192
    win_p = min(BRANCHES[-1][0], s)
    np_, ns = bp * s, bs * t

    tm_p = _tile(s, 512)
    tq = _tile(s, 512)
    cos_p, sin_p = _rope_tables(jnp.arange(s, dtype=jnp.int32))
    pos_s = past_len + jnp.repeat(jnp.arange(t, dtype=jnp.int32), bs)
    cos_s, sin_s = _rope_tables(pos_s)

    hp = x_prompt.reshape(np_, d)
    hs = x_sample.transpose(1, 0, 2).reshape(ns, d)
    outs = {k: [] for k in ("kp", "vp", "cp", "ks", "vs", "cs")}
    row = lambda g: g.reshape(1, -1)
    y_p = y_s = None
    for l in range(depth):
        w_in_bf = w_in[l].astype(BF16)
        w_out_bf = w_out[l].astype(BF16)
        wq_bf = peer_wq[l].astype(BF16)
        keys_bf = peer_sub_keys[l].astype(BF16)
        u_words = _pack_pairs(peer_u[l])
        vt_bf = _pair_order(peer_v[l]).T.astype(BF16)
        last = l == depth - 1

        def tail(hres, attn, cb, z, state, stride, tiles_per_seq, tm):
            h_mid, xn2, xn2t = _mix(attn, cb, z, state, hres, conv_w[l], row(attn_out_g[l]),
                                    row(conv_out_g[l]), w_out_bf, row(norm2_g[l]), tm, stride,
                                    tiles_per_seq)
            n = hres.shape[0]
            r2, a2, lb, a1 = _route(xn2, wq_bf, keys_bf, _tile(n, 256))
            return _peer(xn2t, u_words, vt_bf, r2, a2, lb, a1, h_mid, row(final_norm_g), last,
                         _tile(n, 512), 2048)

        q, k, v, cb, z = _inproj(hp, row(norm1_g[l]), w_in_bf, cos_p, sin_p, tm_p)
        attn = _attn_prompt(q.reshape(bp, s, ATTN_W), k.reshape(bp, s, ATTN_W),
                            v.reshape(bp, s, ATTN_W), tq).reshape(np_, ATTN_W)
        k4 = k.reshape(bp, s, N_HEADS, HEAD_DIM)
        v4 = v.reshape(bp, s, N_HEADS, HEAD_DIM)
        outs["kp"].append(k4[:, s - win_p:])
        outs["vp"].append(v4[:, s - win_p:])
        outs["cp"].append(z.reshape(bp, s, CONV_W)[:, s - (CONV_K - 1):])
        zero_state = jnp.zeros((bp, 8, CONV_W), F32)
        y_p = tail(hp, attn, cb, z, zero_state, 1, s // tm_p, tm_p)

        q, k, v, cb, z = _inproj(hs, row(norm1_g[l]), w_in_bf, cos_s, sin_s, _tile(ns, 512))
        to_bt = lambda a: a.reshape(t, bs, -1).transpose(1, 0, 2)
        k_bt, v_bt = to_bt(k), to_bt(v)
        attn = _attn_sample(to_bt(q), k_bt, v_bt,
                            cache_win_k[l].reshape(bs, -1, ATTN_W),
                            cache_win_v[l].reshape(bs, -1, ATTN_W))
        attn = attn.transpose(1, 0, 2).reshape(ns, ATTN_W)
        outs["ks"].append(k_bt.reshape(bs, t, N_HEADS, HEAD_DIM))
        outs["vs"].append(v_bt.reshape(bs, t, N_HEADS, HEAD_DIM))
        outs["cs"].append(to_bt(z)[:, t - (CONV_K - 1):])
        state = state_conv[l].transpose(1, 0, 2).reshape(1, (CONV_K - 1) * bs, CONV_W)
        y_s = tail(hs, attn, cb, z, state, bs, 1, ns)
        hp, hs = y_p, y_s

    y_prompt = y_p.reshape(bp, s, d)
    y_sample = y_s.reshape(t, bs, d).transpose(1, 0, 2)
    st = lambda name: jnp.stack(outs[name])
    return (y_prompt, y_sample, st("kp"), st("vp"), st("cp"), st("ks"), st("vs"), st("cs"))
```

```python
import functools
import math

import jax
import jax.numpy as jnp
from jax import lax
from jax.experimental import pallas as pl
from jax.experimental.pallas import tpu as pltpu

F32 = jnp.float32
BF16 = jnp.bfloat16

HEAD_DIM = 64
N_HEADS = 12
ATTN_W = N_HEADS * HEAD_DIM
CONV_W = 256
CONV_K = 3
BRANCHES = ((128, 1), (512, 4), (2048, 16))
ROPE_THETA = 10000.0
EPS = 1e-6
PEER_HEADS = 8
PEER_NKEYS = 128
PEER_TOPK = 16
NEG = -1e30
LANES = 128
SUB_EXPERTS = 256
VMEM_LIMIT = 56 * 1024 * 1024

_NT = (((1,), (1,)), ((), ()))


def _rms(x, g):
    return x * lax.rsqrt(jnp.mean(x * x, axis=-1, keepdims=True) + EPS) * g


def _pack_pairs(x):
    r, n = x.shape
    bits = lax.bitcast_convert_type(x.astype(BF16).astype(F32), jnp.uint32) >> 16
    bits = bits.reshape(r // 16, 2, 8, n)
    return (bits[:, 0] | (bits[:, 1] << 16)).reshape(r // 2, n)


def _pair_order(x):
    r = x.shape[0]
    return x.reshape(r // 16, 2, 8, -1).transpose(0, 2, 1, 3).reshape(x.shape)


def _params(sem):
    return pltpu.CompilerParams(dimension_semantics=sem, vmem_limit_bytes=VMEM_LIMIT)


def _inproj_kernel(x_ref, g_ref, w_ref, cos_ref, sin_ref,
                   q_ref, kb_ref, vb_ref, k_ref, v_ref, cb_ref, z_ref):
    xb = _rms(x_ref[...], g_ref[...]).astype(BF16)
    tm = xb.shape[0]
    lane = lax.broadcasted_iota(jnp.int32, (tm, LANES), 1)
    low_half = (lane % HEAD_DIM) < (HEAD_DIM // 2)
    cos = cos_ref[...]
    sin = sin_ref[...]

    def rope(t):
        partner = jnp.where(low_half, pltpu.roll(t, LANES - HEAD_DIM // 2, 1),
                            pltpu.roll(t, HEAD_DIM // 2, 1))
        return t * cos + partner * sin

    scale = 1.0 / math.sqrt(HEAD_DIM)
    q = jnp.dot(xb, w_ref[:, 0:ATTN_W], preferred_element_type=F32)
    for c in range(ATTN_W // LANES):
        sl = slice(c * LANES, (c + 1) * LANES)
        q_ref[:, sl] = (rope(q[:, sl]) * scale).astype(BF16)
    def put_heads(ref, pair, c):
        heads = LANES // HEAD_DIM
        for j in range(heads):
            ref[:, heads * c + j, :] = pair[:, j * HEAD_DIM:(j + 1) * HEAD_DIM]

    k = jnp.dot(xb, w_ref[:, ATTN_W:2 * ATTN_W], preferred_element_type=F32)
    v = jnp.dot(xb, w_ref[:, 2 * ATTN_W:3 * ATTN_W], preferred_element_type=F32)
    for c in range(ATTN_W // LANES):
        sl = slice(c * LANES, (c + 1) * LANES)
        kr = rope(k[:, sl])
        kb_ref[:, sl] = kr.astype(BF16)
        put_heads(k_ref, kr, c)
        put_heads(v_ref, v[:, sl], c)
    vb_ref[...] = v.astype(BF16)
    c0 = 3 * ATTN_W
    conv = jnp.dot(xb, w_ref[:, c0:c0 + 3 * CONV_W], preferred_element_type=F32)
    cb_ref[...] = conv[:, 0:CONV_W]
    z_ref[...] = conv[:, CONV_W:2 * CONV_W] * conv[:, 2 * CONV_W:3 * CONV_W]


def _inproj(x, g, w_bf, cos, sin, tm):
    n, d = x.shape
    win = w_bf.shape[1]
    period = cos.shape[0] // tm
    row = lambda i: (i, 0)
    const = lambda i: (0, 0)
    return pl.pallas_call(
        _inproj_kernel,
        grid=(n // tm,),
        in_specs=[pl.BlockSpec((tm, d), row),
                  pl.BlockSpec((1, d), const),
                  pl.BlockSpec((d, win), const),
                  pl.BlockSpec((tm, LANES), lambda i: (i % period, 0)),
                  pl.BlockSpec((tm, LANES), lambda i: (i % period, 0))],
        out_specs=[pl.BlockSpec((tm, ATTN_W), row), pl.BlockSpec((tm, ATTN_W), row),
                   pl.BlockSpec((tm, ATTN_W), row),
                   pl.BlockSpec((tm, N_HEADS, HEAD_DIM), lambda i: (i, 0, 0)),
                   pl.BlockSpec((tm, N_HEADS, HEAD_DIM), lambda i: (i, 0, 0)),
                   pl.BlockSpec((tm, CONV_W), row), pl.BlockSpec((tm, CONV_W), row)],
        out_shape=[jax.ShapeDtypeStruct((n, ATTN_W), BF16),
                   jax.ShapeDtypeStruct((n, ATTN_W), BF16),
                   jax.ShapeDtypeStruct((n, ATTN_W), BF16),
                   jax.ShapeDtypeStruct((n, N_HEADS, HEAD_DIM), F32),
                   jax.ShapeDtypeStruct((n, N_HEADS, HEAD_DIM), F32),
                   jax.ShapeDtypeStruct((n, CONV_W), F32),
                   jax.ShapeDtypeStruct((n, CONV_W), F32)],
        compiler_params=_params(("parallel",)),
        name="inproj",
    )(x, g, w_bf, cos, sin)


def _rope_tables(pos):
    half = HEAD_DIM // 2
    inv = jnp.exp(-math.log(ROPE_THETA) * jnp.arange(half, dtype=F32) * (2.0 / HEAD_DIM))
    ang = pos.astype(F32)[:, None] * inv[None, :]
    cos = jnp.cos(ang)
    sin = jnp.sin(ang)
    reps = LANES // HEAD_DIM
    cos_t = jnp.tile(jnp.concatenate([cos, cos], axis=1), (1, reps))
    sin_t = jnp.tile(jnp.concatenate([-sin, sin], axis=1), (1, reps))
    return cos_t, sin_t


def _branch_count(dist):
    cnt = jnp.zeros(dist.shape, F32)
    for window, dil in BRANCHES:
        hit = (dist <= window) & ((dist & (dil - 1)) == 0)
        cnt = cnt + jnp.where(hit, 1.0, 0.0)
    return jnp.where(dist >= 0, cnt, 0.0)


def _attn_prompt_kernel(q_ref, k_ref, v_ref, bias_ref, o_ref,
                        kt_ref, vz_ref, m_ref, l_ref, acc_ref, *, tq):
    qi = pl.program_id(2)
    heads = LANES // HEAD_DIM
    lane = lax.broadcasted_iota(jnp.int32, (tq, LANES), 1)
    head_of_lane = lane // HEAD_DIM

    @pl.when(qi == 0)
    def _():
        for jb in range(k_ref.shape[0] // tq):
            rows = slice(jb * tq, (jb + 1) * tq)
            kt_ref[jb] = k_ref[rows, :].astype(F32).T.astype(BF16)
            vb = v_ref[rows, :]
            for h in range(heads):
                vz_ref[heads * jb + h] = jnp.where(head_of_lane == h, vb, jnp.zeros_like(vb))

    q = q_ref[...]
    qz = [jnp.where(head_of_lane == h, q, jnp.zeros_like(q)) for h in range(heads)]
    m_ref[...] = jnp.full(m_ref.shape, NEG, F32)
    l_ref[...] = jnp.zeros(l_ref.shape, F32)
    acc_ref[...] = jnp.zeros(acc_ref.shape, F32)

    def body(j, _):
        bias = bias_ref[qi - j]
        kt = kt_ref[j]
        alphas = []
        pv = None
        for h in range(heads):
            s = jnp.dot(qz[h], kt, preferred_element_type=F32) + bias
            m_prev = m_ref[h]
            m_next = jnp.maximum(m_prev, jnp.max(s, axis=-1, keepdims=True))
            p = jnp.concatenate(
                [jnp.exp(s[:, c * LANES:(c + 1) * LANES] - m_next) for c in range(tq // LANES)],
                axis=-1)
            alpha = jnp.exp(m_prev - m_next)
            l_ref[h] = alpha * l_ref[h] + jnp.sum(p, axis=-1, keepdims=True)
            m_ref[h] = m_next
            alphas.append(alpha)
            d = jnp.dot(p.astype(BF16), vz_ref[heads * j + h], preferred_element_type=F32)
            pv = d if pv is None else pv + d
        alpha_both = jnp.where(head_of_lane == 0, alphas[0], alphas[1])
        acc_ref[...] = alpha_both * acc_ref[...] + pv
        return 0

    lax.fori_loop(0, qi + 1, body, 0)
    l_both = jnp.where(head_of_lane == 0, l_ref[0], l_ref[1])
    o_ref[...] = acc_ref[...] / l_both


def _attn_bias(s, tq):
    d = jnp.arange(s // tq, dtype=jnp.int32)[:, None, None] * tq
    dist = d + jnp.arange(tq, dtype=jnp.int32)[None, :, None] \
        - jnp.arange(tq, dtype=jnp.int32)[None, None, :]
    cnt = _branch_count(dist)
    return jnp.where(cnt > 0.0, jnp.log(jnp.maximum(cnt, 1.0)), NEG)


def _attn_prompt(q, k, v, tq):
    b, s, _ = q.shape
    heads = LANES // HEAD_DIM
    assert heads == 2 and tq % LANES == 0
    nkb = s // tq
    blk = lambda bi, hp, qi: (bi, qi, hp)
    full = lambda bi, hp, qi: (bi, 0, hp)
    return pl.pallas_call(
        functools.partial(_attn_prompt_kernel, tq=tq),
        grid=(b, ATTN_W // LANES, nkb),
        in_specs=[pl.BlockSpec((None, tq, LANES), blk),
                  pl.BlockSpec((None, s, LANES), full),
                  pl.BlockSpec((None, s, LANES), full),
                  pl.BlockSpec((nkb, tq, tq), lambda bi, hp, qi: (0, 0, 0))],
        out_specs=pl.BlockSpec((None, tq, LANES), blk),
        out_shape=jax.ShapeDtypeStruct((b, s, ATTN_W), F32),
        scratch_shapes=[pltpu.VMEM((nkb, LANES, tq), BF16),
                        pltpu.VMEM((nkb * heads, tq, LANES), BF16),
                        pltpu.VMEM((heads, tq, LANES), F32),
                        pltpu.VMEM((heads, tq, LANES), F32),
                        pltpu.VMEM((tq, LANES), F32)],
        compiler_params=_params(("parallel", "parallel", "arbitrary")),
        name="attn_prompt",
    )(q, k, v, _attn_bias(s, tq))


def _attn_sample_kernel(q_ref, kn_ref, vn_ref, kc_ref, vc_ref, o_ref, *, chunk):
    t = q_ref.shape[0]
    n_past = kc_ref.shape[0]
    rows = N_HEADS * t
    qt = jnp.concatenate([q_ref[...]] * N_HEADS, axis=0)
    r_id = lax.broadcasted_iota(jnp.int32, (rows, ATTN_W), 0)
    c_id = lax.broadcasted_iota(jnp.int32, (rows, ATTN_W), 1)
    own = (r_id // t) == (c_id // HEAD_DIM)
    qbd = jnp.where(own, qt, jnp.zeros_like(qt))

    def scores(kblk, first_key):
        nk = kblk.shape[0]
        s = lax.dot_general(qbd, kblk.astype(BF16), _NT, preferred_element_type=F32)
        qpos = n_past + lax.broadcasted_iota(jnp.int32, (rows, nk), 0) % t
        kpos = first_key + lax.broadcasted_iota(jnp.int32, (rows, nk), 1)
        cnt = _branch_count(qpos - kpos)
        return jnp.where(cnt > 0.0, s, NEG), cnt

    parts = [scores(kc_ref[c * chunk:(c + 1) * chunk, :], c * chunk)
             for c in range(n_past // chunk)]
    parts.append(scores(kn_ref[...], n_past))
    m = functools.reduce(jnp.maximum, [jnp.max(s, axis=-1, keepdims=True) for s, _ in parts])
    den = jnp.zeros((rows, 1), F32)
    acc = jnp.zeros((rows, ATTN_W), F32)
    for c, (s, cnt) in enumerate(parts):
        p = cnt * jnp.exp(s - m)
        den = den + jnp.sum(p, axis=-1, keepdims=True)
        vblk = vn_ref[...] if c == len(parts) - 1 else vc_ref[c * chunk:(c + 1) * chunk, :]
        acc = acc + jnp.dot(p.astype(BF16), vblk.astype(BF16), preferred_element_type=F32)
    o_full = jnp.where(own, acc / den, 0.0)
    out = o_full[0:t, :]
    for h in range(1, N_HEADS):
        out = out + o_full[h * t:(h + 1) * t, :]
    o_ref[...] = out


def _attn_sample(q, k_new, v_new, k_cache, v_cache):
    b, t, _ = q.shape
    n_past = k_cache.shape[1]
    new = lambda bi: (bi, 0, 0)
    return pl.pallas_call(
        functools.partial(_attn_sample_kernel, chunk=512),
        grid=(b,),
        in_specs=[pl.BlockSpec((None, t, ATTN_W), new),
                  pl.BlockSpec((None, t, ATTN_W), new),
                  pl.BlockSpec((None, t, ATTN_W), new),
                  pl.BlockSpec((None, n_past, ATTN_W), new),
                  pl.BlockSpec((None, n_past, ATTN_W), new)],
        out_specs=pl.BlockSpec((None, t, ATTN_W), new),
        out_shape=jax.ShapeDtypeStruct((b, t, ATTN_W), F32),
        compiler_params=_params(("parallel",)),
        name="attn_sample",
    )(q, k_new, v_new, k_cache, v_cache)


def _mix_kernel(attn_ref, cb_ref, z_ref, zprev_ref, state_ref, x_ref, cw_ref, ga_ref, gc_ref,
                wo_ref, g2_ref, h_ref, xn_ref, xnt_ref, zbuf, *, stride, tiles_per_seq):
    tm = z_ref.shape[0]
    hb = state_ref.shape[0]
    if tiles_per_seq > 1:
        first = (pl.program_id(0) % tiles_per_seq) == 0
        zbuf[0:hb, :] = jnp.where(first, state_ref[...], zprev_ref[...])
    else:
        zbuf[0:hb, :] = state_ref[...]
    z = z_ref[...]
    zbuf[hb:hb + tm, :] = z
    cw = cw_ref[...]
    y = (cw[0:1, :] * zbuf[hb - 2 * stride:hb - 2 * stride + tm, :]
         + cw[1:2, :] * zbuf[hb - stride:hb - stride + tm, :]
         + cw[2:3, :] * z)
    conv_out = cb_ref[...] * y
    a = _rms(attn_ref[...], ga_ref[...]).astype(BF16)
    c = _rms(conv_out, gc_ref[...]).astype(BF16)
    mixed = (jnp.dot(a, wo_ref[0:ATTN_W, :], preferred_element_type=F32)
             + jnp.dot(c, wo_ref[ATTN_W:ATTN_W + CONV_W, :], preferred_element_type=F32))
    h = x_ref[...] + mixed
    h_ref[...] = h
    xn = _rms(h, g2_ref[...])
    xn_ref[...] = xn.astype(BF16)
    xnt_ref[...] = xn.T.astype(BF16)


def _mix(attn, cb, z, state, x, conv_w, ga, gc, wo_bf, g2, tm, stride, tiles_per_seq):
    n, d = x.shape
    hb = state.shape[1]
    row = lambda i: (i, 0)
    const = lambda i: (0, 0)
    prev = lambda i: (jnp.maximum(i * (tm // hb) - 1, 0), 0)
    return pl.pallas_call(
        functools.partial(_mix_kernel, stride=stride, tiles_per_seq=tiles_per_seq),
        grid=(n // tm,),
        in_specs=[pl.BlockSpec((tm, ATTN_W), row),
                  pl.BlockSpec((tm, CONV_W), row),
                  pl.BlockSpec((tm, CONV_W), row),
                  pl.BlockSpec((hb, CONV_W), prev),
                  pl.BlockSpec((None, hb, CONV_W), lambda i: (i // tiles_per_seq, 0, 0)),
                  pl.BlockSpec((tm, d), row),
                  pl.BlockSpec((CONV_K, CONV_W), const),
                  pl.BlockSpec((1, ATTN_W), const),
                  pl.BlockSpec((1, CONV_W), const),
                  pl.BlockSpec((d, d), const),
                  pl.BlockSpec((1, d), const)],
        out_specs=[pl.BlockSpec((tm, d), row), pl.BlockSpec((tm, d), row),
                   pl.BlockSpec((d, tm), lambda i: (0, i))],
        out_shape=[jax.ShapeDtypeStruct((n, d), F32), jax.ShapeDtypeStruct((n, d), BF16),
                   jax.ShapeDtypeStruct((d, n), BF16)],
        scratch_shapes=[pltpu.VMEM((hb + tm, CONV_W), F32)],
        compiler_params=_params(("parallel",)),
        name="mix",
    )(attn, cb, z, z, state, x, conv_w, ga, gc, wo_bf, g2)


def _top16(s, order):
    unit = 2.0 ** 120
    vals = []
    for r in range(PEER_TOPK):
        m = jnp.max(s, axis=0, keepdims=True)
        if order is None:
            sel = s == m
        else:
            big = jnp.int32(2 ** 30)
            sel = order == jnp.min(jnp.where(s == m, order, big), axis=0, keepdims=True)
        s = jnp.where(sel, -(128.0 + r) * unit, s)
        vals.append(m)
    rank = jnp.where(s <= -128.0 * unit, s * (-1.0 / unit) - 128.0, float(PEER_TOPK))
    return rank, jnp.concatenate(vals, axis=0)


_CAND_GROUPS = ((0, 0), (0, 8), (1, 0), (2, 0), (3, 0), (4, 0), (5, 0), (6, 0), (7, 0))


def _route_tile(s1, s2, exact):
    tm = s1.shape[1]
    sub = 8
    iota128 = lax.broadcasted_iota(jnp.int32, s1.shape, 0) if exact else None
    rank1, sv1 = _top16(s1, iota128)
    rank2, sv2 = _top16(s2, iota128)
    groups = [sv1[a:a + 1, :] + sv2[b0:b0 + sub, :] for a, b0 in _CAND_GROUPS]
    groups.append(sv1[sub:2 * sub, :] + sv2[0:1, :])
    cand = jnp.concatenate(groups, axis=0)
    order = None
    if exact:
        i8 = lax.broadcasted_iota(jnp.int32, (sub, tm), 0)
        order = jnp.concatenate([a * PEER_TOPK + b0 + i8 for a, b0 in _CAND_GROUPS]
                                + [(sub + i8) * PEER_TOPK], axis=0)
    crank, _ = _top16(cand, order)
    chosen = crank < float(PEER_TOPK)
    cmax = sv1[0:1, :] + sv2[0:1, :]
    zsum = jnp.sum(jnp.where(chosen, jnp.exp(cand - cmax), 0.0), axis=0, keepdims=True)
    picked = jnp.where(chosen, 1.0, 0.0)
    n_of_a = [jnp.sum(picked[0:2 * sub, :], axis=0, keepdims=True)]
    for a in range(1, sub):
        n_of_a.append(jnp.sum(picked[(a + 1) * sub:(a + 2) * sub, :], axis=0, keepdims=True))
    last = (len(_CAND_GROUPS)) * sub
    for a in range(sub, PEER_TOPK):
        n_of_a.append(picked[last + a - sub:last + a - sub + 1, :])
    lb = jnp.zeros(rank1.shape, F32)
    for a in range(PEER_TOPK):
        lb = jnp.where(rank1 == float(a), n_of_a[a], lb)
    a2 = jnp.exp(s2 - sv2[0:1, :])
    a1 = jnp.exp(s1 - sv1[0:1, :]) / zsum
    tied = None
    if not exact:
        def extra(rk):
            n_sel = jnp.sum(jnp.where(rk < float(PEER_TOPK), 1.0, 0.0), axis=0, keepdims=True)
            return jnp.max(n_sel) > float(PEER_TOPK)
        tied = extra(rank1) | extra(rank2) | extra(crank)
    return rank2, a2, lb, a1, tied


def _route_kernel(xn_ref, wq_ref, keys_ref, r2_ref, a2_ref, lb_ref, a1_ref, q_scr):
    h = pl.program_id(1)
    nsub = 2 * PEER_HEADS

    @pl.when(h == 0)
    def _():
        q = jnp.dot(xn_ref[...], wq_ref[...], preferred_element_type=F32)
        for c in range(nsub):
            q_scr[c] = q[:, c * PEER_NKEYS:(c + 1) * PEER_NKEYS].astype(BF16)

    s1 = lax.dot_general(keys_ref[0], q_scr[2 * h], _NT, preferred_element_type=F32)
    s2 = lax.dot_general(keys_ref[1], q_scr[2 * h + 1], _NT, preferred_element_type=F32)

    def twice(x):
        hi = lax.bitcast_convert_type(x.astype(BF16).astype(F32), jnp.uint32)
        return hi | (hi >> 16)

    def emit(rank2, a2, lb, a1):
        r2_ref[...] = _pack_pairs(rank2)
        a2_ref[...] = _pack_pairs(a2)
        lb_ref[...] = twice(lb)
        a1_ref[...] = twice(a1)

    *outs, tied = _route_tile(s1, s2, exact=False)
    emit(*outs)

    @pl.when(tied)
    def _():
        emit(*_route_tile(s1, s2, exact=True)[:4])


def _route(xn, wq_bf, keys_bf, tm):
    n, d = xn.shape
    qd = wq_bf.shape[1]
    out_spec = lambda rows: pl.BlockSpec((None, rows, tm), lambda i, h: (h, 0, i))
    out_sds = lambda rows: jax.ShapeDtypeStruct((PEER_HEADS, rows, n), jnp.uint32)
    half = PEER_NKEYS // 2
    return pl.pallas_call(
        _route_kernel,
        grid=(n // tm, PEER_HEADS),
        in_specs=[pl.BlockSpec((tm, d), lambda i, h: (i, 0)),
                  pl.BlockSpec((d, qd), lambda i, h: (0, 0)),
                  pl.BlockSpec((None, 2, PEER_NKEYS, PEER_NKEYS), lambda i, h: (h, 0, 0, 0))],
        out_specs=[out_spec(half), out_spec(half), out_spec(PEER_NKEYS), out_spec(PEER_NKEYS)],
        out_shape=[out_sds(half), out_sds(half), out_sds(PEER_NKEYS), out_sds(PEER_NKEYS)],
        scratch_shapes=[pltpu.VMEM((2 * PEER_HEADS, tm, PEER_NKEYS), BF16)],
        compiler_params=_params(("parallel", "arbitrary")),
        name="peer_route",
    )(xn, wq_bf, keys_bf)


def _peer_kernel(xn_ref, u_ref, vt_ref, r2_ref, a2_ref, lb_ref, a1_ref, h_ref, gf_ref,
                 y_ref, acc_ref, *, final_norm):
    j = pl.program_id(1)
    te = vt_ref.shape[1]

    @pl.when(j == 0)
    def _():
        acc_ref[...] = jnp.zeros_like(acc_ref)

    tm = xn_ref.shape[1]

    def row_bf16(ref, hh, blk):
        word = jnp.broadcast_to(ref[hh, blk:blk + 1, :], (8, tm))
        tile = pltpu.bitcast(word, BF16)
        return jnp.concatenate([tile] * (PEER_NKEYS // tile.shape[0]), axis=0)

    sub = min(te, SUB_EXPERTS)
    ps = []
    for s0 in range(0, te, sub):
        u_rows = pltpu.bitcast(u_ref[s0 // 2:(s0 + sub) // 2, :], BF16)
        ht = jnp.dot(u_rows, xn_ref[...], preferred_element_type=F32)
        hb = ht.astype(BF16)
        act = (0.5 * hb) * (1.0 + lax.erf(hb * (1.0 / math.sqrt(2.0))))
        gates = []
        for bb in range(s0 // PEER_NKEYS, (s0 + sub) // PEER_NKEYS):
            g = jnp.zeros((PEER_NKEYS, tm), BF16)
            for hh in range(PEER_HEADS):
                lim = row_bf16(lb_ref, hh, bb)
                w1 = row_bf16(a1_ref, hh, bb)
                r2 = pltpu.bitcast(r2_ref[hh], BF16)
                a2 = pltpu.bitcast(a2_ref[hh], BF16)
                g = g + jnp.where(r2 < lim, w1 * a2, jnp.zeros_like(g))
            gates.append(g)
        ps.append(act.astype(BF16) * jnp.concatenate(gates, axis=0))
    p = jnp.concatenate(ps, axis=0)
    acc_ref[...] += jnp.dot(vt_ref[...], p, preferred_element_type=F32)

    @pl.when(j == pl.num_programs(1) - 1)
    def _():
        out = h_ref[...] + acc_ref[...].T
        y_ref[...] = _rms(out, gf_ref[...]) if final_norm else out


def _peer(xnt, u_words, vt_bf, r2, a2, lb, a1, h, gf, final_norm, tm, te):
    d, n = xnt.shape
    ne = 2 * u_words.shape[0]
    gate_spec = pl.BlockSpec((PEER_HEADS, PEER_NKEYS // 2, tm), lambda i, j: (0, 0, i))
    nblk = te // PEER_NKEYS
    assert nblk % 8 == 0, "first-key rows are delivered as whole sublane tiles"
    row_spec = pl.BlockSpec((PEER_HEADS, nblk, tm), lambda i, j: (0, j, i))
    return pl.pallas_call(
        functools.partial(_peer_kernel, final_norm=final_norm),
        grid=(n // tm, ne // te),
        in_specs=[pl.BlockSpec((d, tm), lambda i, j: (0, i)),
                  pl.BlockSpec((te // 2, d), lambda i, j: (j, 0)),
                  pl.BlockSpec((d, te), lambda i, j: (0, j)),
                  gate_spec, gate_spec, row_spec, row_spec,
                  pl.BlockSpec((tm, d), lambda i, j: (i, 0)),
                  pl.BlockSpec((1, d), lambda i, j: (0, 0))],
        out_specs=pl.BlockSpec((tm, d), lambda i, j: (i, 0)),
        out_shape=jax.ShapeDtypeStruct((n, d), F32),
        scratch_shapes=[pltpu.VMEM((d, tm), F32)],
        compiler_params=_params(("parallel", "arbitrary")),
        name="peer_experts",
    )(xnt, u_words, vt_bf, r2, a2, lb, a1, h, gf)


def _tile(n, pref):
    t = min(n, pref)
    assert n % t == 0, (n, t)
    return t


def kernel(x_prompt, x_sample, cache_win_k, cache_win_v, state_conv, norm1_g, w_in, conv_w,
           attn_out_g, conv_out_g, w_out, norm2_g, peer_wq, peer_sub_keys, peer_u, peer_v,
           final_norm_g):
    bp, s, d = x_prompt.shape
    bs, t, _ = x_sample.shape
    depth = w_in.shape[0]
    past_len = 8192
    win_p = min(BRANCHES[-1][0], s)
    np_, ns = bp * s, bs * t

    tm_p = _tile(s, 512)
    tq = _tile(s, 512)
    cos_p, sin_p = _rope_tables(jnp.arange(s, dtype=jnp.int32))
    pos_s = past_len + jnp.repeat(jnp.arange(t, dtype=jnp.int32), bs)
    cos_s, sin_s = _rope_tables(pos_s)

    hp = x_prompt.reshape(np_, d)
    hs = x_sample.transpose(1, 0, 2).reshape(ns, d)
    outs = {k: [] for k in ("kp", "vp", "cp", "ks", "vs", "cs")}
    row = lambda g: g.reshape(1, -1)
    y_p = y_s = None
    for l in range(depth):
        w_in_bf = w_in[l].astype(BF16)
        w_out_bf = w_out[l].astype(BF16)
        wq_bf = peer_wq[l].astype(BF16)
        keys_bf = peer_sub_keys[l].astype(BF16)
        u_words = _pack_pairs(peer_u[l])
        vt_bf = _pair_order(peer_v[l]).T.astype(BF16)
        last = l == depth - 1

        def tail(hres, attn, cb, z, state, stride, tiles_per_seq, tm):
            h_mid, xn2, xn2t = _mix(attn, cb, z, state, hres, conv_w[l], row(attn_out_g[l]),
                                    row(conv_out_g[l]), w_out_bf, row(norm2_g[l]), tm, stride,
                                    tiles_per_seq)
            n = hres.shape[0]
            r2, a2, lb, a1 = _route(xn2, wq_bf, keys_bf, _tile(n, 256))
            return _peer(xn2t, u_words, vt_bf, r2, a2, lb, a1, h_mid, row(final_norm_g), last,
                         _tile(n, 512), 2048)

        q, kb, vb, k4, v4, cb, z = _inproj(hp, row(norm1_g[l]), w_in_bf, cos_p, sin_p, tm_p)
        attn = _attn_prompt(q.reshape(bp, s, ATTN_W), kb.reshape(bp, s, ATTN_W),
                            vb.reshape(bp, s, ATTN_W), tq).reshape(np_, ATTN_W)
        outs["kp"].append(k4.reshape(bp, s, N_HEADS, HEAD_DIM)[:, s - win_p:])
        outs["vp"].append(v4.reshape(bp, s, N_HEADS, HEAD_DIM)[:, s - win_p:])
        outs["cp"].append(z.reshape(bp, s, CONV_W)[:, s - (CONV_K - 1):])
        zero_state = jnp.zeros((bp, 8, CONV_W), F32)
        y_p = tail(hp, attn, cb, z, zero_state, 1, s // tm_p, tm_p)

        q, kb, vb, k4, v4, cb, z = _inproj(hs, row(norm1_g[l]), w_in_bf, cos_s, sin_s,
                                           _tile(ns, 512))
        to_bt = lambda a: a.reshape(t, bs, -1).transpose(1, 0, 2)
        attn = _attn_sample(to_bt(q), to_bt(kb), to_bt(vb),
                            cache_win_k[l].reshape(bs, -1, ATTN_W),
                            cache_win_v[l].reshape(bs, -1, ATTN_W))
        attn = attn.transpose(1, 0, 2).reshape(ns, ATTN_W)
        to_bthd = lambda a: a.reshape(t, bs, N_HEADS, HEAD_DIM).transpose(1, 0, 2, 3)
        outs["ks"].append(to_bthd(k4))
        outs["vs"].append(to_bthd(v4))
        outs["cs"].append(to_bt(z)[:, t - (CONV_K - 1):])
        state = state_conv[l].transpose(1, 0, 2).reshape(1, (CONV_K - 1) * bs, CONV_W)
        y_s = tail(hs, attn, cb, z, state, bs, 1, ns)
        hp, hs = y_p, y_s

    y_prompt = y_p.reshape(bp, s, d)
    y_sample = y_s.reshape(t, bs, d).transpose(1, 0, 2)
    st = lambda name: jnp.stack(outs[name])
    return (y_prompt, y_sample, st("kp"), st("vp"), st("cp"), st("ks"), st("vs"), st("cs"))
```

```python
import functools
import math

import jax
import jax.numpy as jnp
from jax import lax
from jax.experimental import pallas as pl
from jax.experimental.pallas import tpu as pltpu

F32 = jnp.float32
BF16 = jnp.bfloat16

HEAD_DIM = 64
N_HEADS = 12
ATTN_W = N_HEADS * HEAD_DIM
CONV_W = 256
CONV_K = 3
BRANCHES = ((128, 1), (512, 4), (2048, 16))
MID_WINDOW = BRANCHES[1][0]
FAR_DIL = BRANCHES[2][1]
ROPE_THETA = 10000.0
EPS = 1e-6
PEER_HEADS = 8
PEER_NKEYS = 128
PEER_TOPK = 16
NEG = -1e30
LANES = 128
SUB_EXPERTS = 256
VMEM_LIMIT = 56 * 1024 * 1024

_NT = (((1,), (1,)), ((), ()))


def _rms(x, g):
    return x * lax.rsqrt(jnp.mean(x * x, axis=-1, keepdims=True) + EPS) * g


def _pack_pairs(x):
    r, n = x.shape
    bits = lax.bitcast_convert_type(x.astype(BF16).astype(F32), jnp.uint32) >> 16
    bits = bits.reshape(r // 16, 2, 8, n)
    return (bits[:, 0] | (bits[:, 1] << 16)).reshape(r // 2, n)


def _pair_order(x):
    r = x.shape[0]
    return x.reshape(r // 16, 2, 8, -1).transpose(0, 2, 1, 3).reshape(x.shape)


def _params(sem):
    return pltpu.CompilerParams(dimension_semantics=sem, vmem_limit_bytes=VMEM_LIMIT)


def _inproj_kernel(x_ref, g_ref, w_ref, cos_ref, sin_ref,
                   q_ref, k_ref, v_ref, cb_ref, z_ref):
    xb = _rms(x_ref[...], g_ref[...]).astype(BF16)
    tm = xb.shape[0]
    lane = lax.broadcasted_iota(jnp.int32, (tm, LANES), 1)
    low_half = (lane % HEAD_DIM) < (HEAD_DIM // 2)
    cos = cos_ref[...]
    sin = sin_ref[...]

    def rope(t):
        partner = jnp.where(low_half, pltpu.roll(t, LANES - HEAD_DIM // 2, 1),
                            pltpu.roll(t, HEAD_DIM // 2, 1))
        return t * cos + partner * sin

    scale = 1.0 / math.sqrt(HEAD_DIM)
    q = jnp.dot(xb, w_ref[:, 0:ATTN_W], preferred_element_type=F32)
    for c in range(ATTN_W // LANES):
        sl = slice(c * LANES, (c + 1) * LANES)
        q_ref[:, sl] = (rope(q[:, sl]) * scale).astype(BF16)
    k = jnp.dot(xb, w_ref[:, ATTN_W:2 * ATTN_W], preferred_element_type=F32)
    for c in range(ATTN_W // LANES):
        sl = slice(c * LANES, (c + 1) * LANES)
        k_ref[:, sl] = rope(k[:, sl])
    v_ref[...] = jnp.dot(xb, w_ref[:, 2 * ATTN_W:3 * ATTN_W], preferred_element_type=F32)
    c0 = 3 * ATTN_W
    conv = jnp.dot(xb, w_ref[:, c0:c0 + 3 * CONV_W], preferred_element_type=F32)
    cb_ref[...] = conv[:, 0:CONV_W]
    z_ref[...] = conv[:, CONV_W:2 * CONV_W] * conv[:, 2 * CONV_W:3 * CONV_W]


def _inproj(x, g, w_bf, cos, sin, tm):
    n, d = x.shape
    win = w_bf.shape[1]
    period = cos.shape[0] // tm
    row = lambda i: (i, 0)
    const = lambda i: (0, 0)
    return pl.pallas_call(
        _inproj_kernel,
        grid=(n // tm,),
        in_specs=[pl.BlockSpec((tm, d), row),
                  pl.BlockSpec((1, d), const),
                  pl.BlockSpec((d, win), const),
                  pl.BlockSpec((tm, LANES), lambda i: (i % period, 0)),
                  pl.BlockSpec((tm, LANES), lambda i: (i % period, 0))],
        out_specs=[pl.BlockSpec((tm, ATTN_W), row), pl.BlockSpec((tm, ATTN_W), row),
                   pl.BlockSpec((tm, ATTN_W), row), pl.BlockSpec((tm, CONV_W), row),
                   pl.BlockSpec((tm, CONV_W), row)],
        out_shape=[jax.ShapeDtypeStruct((n, ATTN_W), BF16),
                   jax.ShapeDtypeStruct((n, ATTN_W), F32),
                   jax.ShapeDtypeStruct((n, ATTN_W), F32),
                   jax.ShapeDtypeStruct((n, CONV_W), F32),
                   jax.ShapeDtypeStruct((n, CONV_W), F32)],
        compiler_params=_params(("parallel",)),
        name="inproj",
    )(x, g, w_bf, cos, sin)


def _rope_tables(pos):
    half = HEAD_DIM // 2
    inv = jnp.exp(-math.log(ROPE_THETA) * jnp.arange(half, dtype=F32) * (2.0 / HEAD_DIM))
    ang = pos.astype(F32)[:, None] * inv[None, :]
    cos = jnp.cos(ang)
    sin = jnp.sin(ang)
    reps = LANES // HEAD_DIM
    cos_t = jnp.tile(jnp.concatenate([cos, cos], axis=1), (1, reps))
    sin_t = jnp.tile(jnp.concatenate([-sin, sin], axis=1), (1, reps))
    return cos_t, sin_t


def _branch_count(dist):
    cnt = jnp.zeros(dist.shape, F32)
    for window, dil in BRANCHES:
        hit = (dist <= window) & ((dist & (dil - 1)) == 0)
        cnt = cnt + jnp.where(hit, 1.0, 0.0)
    return jnp.where(dist >= 0, cnt, 0.0)


def _attn_prompt_kernel(q_ref, k_ref, v_ref, bias_ref, o_ref,
                        kt_ref, vz_ref, m_ref, l_ref, acc_ref, *, tq):
    qi = pl.program_id(2)
    heads = LANES // HEAD_DIM
    lane = lax.broadcasted_iota(jnp.int32, (tq, LANES), 1)
    head_of_lane = lane // HEAD_DIM

    @pl.when(qi == 0)
    def _():
        for jb in range(k_ref.shape[0] // tq):
            rows = slice(jb * tq, (jb + 1) * tq)
            kt_ref[jb] = k_ref[rows, :].T.astype(BF16)
            vb = v_ref[rows, :]
            for h in range(heads):
                vz_ref[heads * jb + h] = jnp.where(head_of_lane == h, vb, 0.0).astype(BF16)

    q = q_ref[...]
    qz = [jnp.where(head_of_lane == h, q, jnp.zeros_like(q)) for h in range(heads)]
    m_ref[...] = jnp.full(m_ref.shape, NEG, F32)
    l_ref[...] = jnp.zeros(l_ref.shape, F32)
    acc_ref[...] = jnp.zeros(acc_ref.shape, F32)

    def body(j, _):
        bias = bias_ref[qi - j]
        kt = kt_ref[j]
        alphas = []
        pv = None
        for h in range(heads):
            s = jnp.dot(qz[h], kt, preferred_element_type=F32) + bias
            m_prev = m_ref[h]
            m_next = jnp.maximum(m_prev, jnp.max(s, axis=-1, keepdims=True))
            p = jnp.concatenate(
                [jnp.exp(s[:, c * LANES:(c + 1) * LANES] - m_next) for c in range(tq // LANES)],
                axis=-1)
            alpha = jnp.exp(m_prev - m_next)
            l_ref[h] = alpha * l_ref[h] + jnp.sum(p, axis=-1, keepdims=True)
            m_ref[h] = m_next
            alphas.append(alpha)
            d = jnp.dot(p.astype(BF16), vz_ref[heads * j + h], preferred_element_type=F32)
            pv = d if pv is None else pv + d
        alpha_both = jnp.where(head_of_lane == 0, alphas[0], alphas[1])
        acc_ref[...] = alpha_both * acc_ref[...] + pv
        return 0

    lax.fori_loop(0, qi + 1, body, 0)
    l_both = jnp.where(head_of_lane == 0, l_ref[0], l_ref[1])
    o_ref[...] = acc_ref[...] / l_both


def _attn_bias(s, tq):
    d = jnp.arange(s // tq, dtype=jnp.int32)[:, None, None] * tq
    dist = d + jnp.arange(tq, dtype=jnp.int32)[None, :, None] \
        - jnp.arange(tq, dtype=jnp.int32)[None, None, :]
    cnt = _branch_count(dist)
    return jnp.where(cnt > 0.0, jnp.log(jnp.maximum(cnt, 1.0)), NEG)


def _attn_prompt(q, k, v, tq):
    b, s, _ = q.shape
    heads = LANES // HEAD_DIM
    assert heads == 2 and tq % LANES == 0
    nkb = s // tq
    blk = lambda bi, hp, qi: (bi, qi, hp)
    full = lambda bi, hp, qi: (bi, 0, hp)
    return pl.pallas_call(
        functools.partial(_attn_prompt_kernel, tq=tq),
        grid=(b, ATTN_W // LANES, nkb),
        in_specs=[pl.BlockSpec((None, tq, LANES), blk),
                  pl.BlockSpec((None, s, LANES), full),
                  pl.BlockSpec((None, s, LANES), full),
                  pl.BlockSpec((nkb, tq, tq), lambda bi, hp, qi: (0, 0, 0))],
        out_specs=pl.BlockSpec((None, tq, LANES), blk),
        out_shape=jax.ShapeDtypeStruct((b, s, ATTN_W), F32),
        scratch_shapes=[pltpu.VMEM((nkb, LANES, tq), BF16),
                        pltpu.VMEM((nkb * heads, tq, LANES), BF16),
                        pltpu.VMEM((heads, tq, LANES), F32),
                        pltpu.VMEM((heads, tq, LANES), F32),
                        pltpu.VMEM((tq, LANES), F32)],
        compiler_params=_params(("parallel", "parallel", "arbitrary")),
        name="attn_prompt",
    )(q, k, v, _attn_bias(s, tq))


def _attn_sample_kernel(q_ref, kn_ref, vn_ref, kf_ref, vf_ref, kt_ref, vt_ref, o_ref, *,
                        chunk, n_past):
    t = q_ref.shape[0]
    tail_start = n_past - kt_ref.shape[0]
    rows = N_HEADS * t
    qt = jnp.concatenate([q_ref[...]] * N_HEADS, axis=0)
    r_id = lax.broadcasted_iota(jnp.int32, (rows, ATTN_W), 0)
    c_id = lax.broadcasted_iota(jnp.int32, (rows, ATTN_W), 1)
    own = (r_id // t) == (c_id // HEAD_DIM)
    qbd = jnp.where(own, qt, jnp.zeros_like(qt))

    def scores(kblk, first_row, far):
        nk = kblk.shape[0]
        s = lax.dot_general(qbd, kblk.astype(BF16), _NT, preferred_element_type=F32)
        qpos = n_past + lax.broadcasted_iota(jnp.int32, (rows, nk), 0) % t
        i = first_row + lax.broadcasted_iota(jnp.int32, (rows, nk), 1)
        kpos = (i // t) * FAR_DIL + i % t if far else i
        cnt = _branch_count(qpos - kpos)
        return jnp.where(cnt > 0.0, s, NEG), cnt

    pieces = []
    for ref_k, ref_v, base, far in ((kf_ref, vf_ref, 0, True), (kt_ref, vt_ref, tail_start, False)):
        for c0 in range(0, ref_k.shape[0], chunk):
            sl = slice(c0, min(c0 + chunk, ref_k.shape[0]))
            pieces.append((ref_k, ref_v, sl, base + c0, far))
    pieces.append((kn_ref, vn_ref, slice(0, t), n_past, False))
    parts = [scores(rk[sl, :], first, far) for rk, _, sl, first, far in pieces]
    m = functools.reduce(jnp.maximum, [jnp.max(s, axis=-1, keepdims=True) for s, _ in parts])
    den = jnp.zeros((rows, 1), F32)
    acc = jnp.zeros((rows, ATTN_W), F32)
    for (s, cnt), (_, rv, sl, _, _) in zip(parts, pieces):
        p = cnt * jnp.exp(s - m)
        den = den + jnp.sum(p, axis=-1, keepdims=True)
        acc = acc + jnp.dot(p.astype(BF16), rv[sl, :].astype(BF16), preferred_element_type=F32)
    o_full = jnp.where(own, acc / den, 0.0)
    out = o_full[0:t, :]
    for h in range(1, N_HEADS):
        out = out + o_full[h * t:(h + 1) * t, :]
    o_ref[...] = out


def _attn_sample(q, k_new, v_new, k_cache, v_cache):
    b, t, _ = q.shape
    n_past = k_cache.shape[1]
    tail_start = n_past - MID_WINDOW
    assert t <= FAR_DIL and n_past % FAR_DIL == 0 and tail_start >= FAR_DIL \
        and tail_start % FAR_DIL == 0, (t, n_past)

    def split(c):
        far = c[:, :tail_start].reshape(b, tail_start // FAR_DIL, FAR_DIL, ATTN_W)[:, :, :t]
        return far.reshape(b, -1, ATTN_W), c[:, tail_start:].reshape(b, -1, ATTN_W)

    kf, kt = split(k_cache)
    vf, vt = split(v_cache)
    per_b = lambda bi: (bi, 0, 0)
    blk = lambda a: pl.BlockSpec((None,) + a.shape[1:], per_b)
    args = (q, k_new, v_new, kf, vf, kt, vt)
    return pl.pallas_call(
        functools.partial(_attn_sample_kernel, chunk=512, n_past=n_past),
        grid=(b,),
        in_specs=[blk(a) for a in args],
        out_specs=pl.BlockSpec((None, t, ATTN_W), per_b),
        out_shape=jax.ShapeDtypeStruct((b, t, ATTN_W), F32),
        compiler_params=_params(("parallel",)),
        name="attn_sample",
    )(*args)


def _mix_kernel(attn_ref, cb_ref, z_ref, zprev_ref, state_ref, x_ref, cw_ref, ga_ref, gc_ref,
                wo_ref, g2_ref, h_ref, xn_ref, xnt_ref, zbuf, *, stride, tiles_per_seq):
    tm = z_ref.shape[0]
    hb = state_ref.shape[0]
    if tiles_per_seq > 1:
        first = (pl.program_id(0) % tiles_per_seq) == 0
        zbuf[0:hb, :] = jnp.where(first, state_ref[...], zprev_ref[...])
    else:
        zbuf[0:hb, :] = state_ref[...]
    z = z_ref[...]
    zbuf[hb:hb + tm, :] = z
    cw = cw_ref[...]
    y = (cw[0:1, :] * zbuf[hb - 2 * stride:hb - 2 * stride + tm, :]
         + cw[1:2, :] * zbuf[hb - stride:hb - stride + tm, :]
         + cw[2:3, :] * z)
    conv_out = cb_ref[...] * y
    a = _rms(attn_ref[...], ga_ref[...]).astype(BF16)
    c = _rms(conv_out, gc_ref[...]).astype(BF16)
    mixed = (jnp.dot(a, wo_ref[0:ATTN_W, :], preferred_element_type=F32)
             + jnp.dot(c, wo_ref[ATTN_W:ATTN_W + CONV_W, :], preferred_element_type=F32))
    h = x_ref[...] + mixed
    h_ref[...] = h
    xn = _rms(h, g2_ref[...])
    xn_ref[...] = xn.astype(BF16)
    xnt_ref[...] = xn.T.astype(BF16)


def _mix(attn, cb, z, state, x, conv_w, ga, gc, wo_bf, g2, tm, stride, tiles_per_seq):
    n, d = x.shape
    hb = state.shape[1]
    row = lambda i: (i, 0)
    const = lambda i: (0, 0)
    prev = lambda i: (jnp.maximum(i * (tm // hb) - 1, 0), 0)
    return pl.pallas_call(
        functools.partial(_mix_kernel, stride=stride, tiles_per_seq=tiles_per_seq),
        grid=(n // tm,),
        in_specs=[pl.BlockSpec((tm, ATTN_W), row),
                  pl.BlockSpec((tm, CONV_W), row),
                  pl.BlockSpec((tm, CONV_W), row),
                  pl.BlockSpec((hb, CONV_W), prev),
                  pl.BlockSpec((None, hb, CONV_W), lambda i: (i // tiles_per_seq, 0, 0)),
                  pl.BlockSpec((tm, d), row),
                  pl.BlockSpec((CONV_K, CONV_W), const),
                  pl.BlockSpec((1, ATTN_W), const),
                  pl.BlockSpec((1, CONV_W), const),
                  pl.BlockSpec((d, d), const),
                  pl.BlockSpec((1, d), const)],
        out_specs=[pl.BlockSpec((tm, d), row), pl.BlockSpec((tm, d), row),
                   pl.BlockSpec((d, tm), lambda i: (0, i))],
        out_shape=[jax.ShapeDtypeStruct((n, d), F32), jax.ShapeDtypeStruct((n, d), BF16),
                   jax.ShapeDtypeStruct((d, n), BF16)],
        scratch_shapes=[pltpu.VMEM((hb + tm, CONV_W), F32)],
        compiler_params=_params(("parallel",)),
        name="mix",
    )(attn, cb, z, z, state, x, conv_w, ga, gc, wo_bf, g2)


def _top16(s, order):
    unit = 2.0 ** 120
    vals = []
    for r in range(PEER_TOPK):
        m = jnp.max(s, axis=0, keepdims=True)
        if order is None:
            sel = s == m
        else:
            big = jnp.int32(2 ** 30)
            sel = order == jnp.min(jnp.where(s == m, order, big), axis=0, keepdims=True)
        s = jnp.where(sel, -(128.0 + r) * unit, s)
        vals.append(m)
    rank = jnp.where(s <= -128.0 * unit, s * (-1.0 / unit) - 128.0, float(PEER_TOPK))
    return rank, jnp.concatenate(vals, axis=0)


_CAND_GROUPS = ((0, 0), (0, 8), (1, 0), (2, 0), (3, 0), (4, 0), (5, 0), (6, 0), (7, 0))


def _route_tile(s1, s2, exact):
    tm = s1.shape[1]
    sub = 8
    iota128 = lax.broadcasted_iota(jnp.int32, s1.shape, 0) if exact else None
    rank1, sv1 = _top16(s1, iota128)
    rank2, sv2 = _top16(s2, iota128)
    groups = [sv1[a:a + 1, :] + sv2[b0:b0 + sub, :] for a, b0 in _CAND_GROUPS]
    groups.append(sv1[sub:2 * sub, :] + sv2[0:1, :])
    cand = jnp.concatenate(groups, axis=0)
    order = None
    if exact:
        i8 = lax.broadcasted_iota(jnp.int32, (sub, tm), 0)
        order = jnp.concatenate([a * PEER_TOPK + b0 + i8 for a, b0 in _CAND_GROUPS]
                                + [(sub + i8) * PEER_TOPK], axis=0)
    crank, _ = _top16(cand, order)
    chosen = crank < float(PEER_TOPK)
    cmax = sv1[0:1, :] + sv2[0:1, :]
    zsum = jnp.sum(jnp.where(chosen, jnp.exp(cand - cmax), 0.0), axis=0, keepdims=True)
    picked = jnp.where(chosen, 1.0, 0.0)
    n_of_a = [jnp.sum(picked[0:2 * sub, :], axis=0, keepdims=True)]
    for a in range(1, sub):
        n_of_a.append(jnp.sum(picked[(a + 1) * sub:(a + 2) * sub, :], axis=0, keepdims=True))
    last = (len(_CAND_GROUPS)) * sub
    for a in range(sub, PEER_TOPK):
        n_of_a.append(picked[last + a - sub:last + a - sub + 1, :])
    lb = jnp.zeros(rank1.shape, F32)
    for a in range(PEER_TOPK):
        lb = jnp.where(rank1 == float(a), n_of_a[a], lb)
    a2 = jnp.exp(s2 - sv2[0:1, :])
    a1 = jnp.exp(s1 - sv1[0:1, :]) / zsum
    tied = None
    if not exact:
        def extra(rk):
            n_sel = jnp.sum(jnp.where(rk < float(PEER_TOPK), 1.0, 0.0), axis=0, keepdims=True)
            return jnp.max(n_sel) > float(PEER_TOPK)
        tied = extra(rank1) | extra(rank2) | extra(crank)
    return rank2, a2, lb, a1, tied


def _route_kernel(xn_ref, wq_ref, keys_ref, r2_ref, a2_ref, lb_ref, a1_ref, q_scr):
    h = pl.program_id(1)
    nsub = 2 * PEER_HEADS

    @pl.when(h == 0)
    def _():
        q = jnp.dot(xn_ref[...], wq_ref[...], preferred_element_type=F32)
        for c in range(nsub):
            q_scr[c] = q[:, c * PEER_NKEYS:(c + 1) * PEER_NKEYS].astype(BF16)

    s1 = lax.dot_general(keys_ref[0], q_scr[2 * h], _NT, preferred_element_type=F32)
    s2 = lax.dot_general(keys_ref[1], q_scr[2 * h + 1], _NT, preferred_element_type=F32)

    def twice(x):
        hi = lax.bitcast_convert_type(x.astype(BF16).astype(F32), jnp.uint32)
        return hi | (hi >> 16)

    def emit(rank2, a2, lb, a1):
        r2_ref[...] = _pack_pairs(rank2)
        a2_ref[...] = _pack_pairs(a2)
        lb_ref[...] = twice(lb)
        a1_ref[...] = twice(a1)

    *outs, tied = _route_tile(s1, s2, exact=False)
    emit(*outs)

    @pl.when(tied)
    def _():
        emit(*_route_tile(s1, s2, exact=True)[:4])


def _route(xn, wq_bf, keys_bf, tm):
    n, d = xn.shape
    qd = wq_bf.shape[1]
    out_spec = lambda rows: pl.BlockSpec((None, rows, tm), lambda i, h: (h, 0, i))
    out_sds = lambda rows: jax.ShapeDtypeStruct((PEER_HEADS, rows, n), jnp.uint32)
    half = PEER_NKEYS // 2
    return pl.pallas_call(
        _route_kernel,
        grid=(n // tm, PEER_HEADS),
        in_specs=[pl.BlockSpec((tm, d), lambda i, h: (i, 0)),
                  pl.BlockSpec((d, qd), lambda i, h: (0, 0)),
                  pl.BlockSpec((None, 2, PEER_NKEYS, PEER_NKEYS), lambda i, h: (h, 0, 0, 0))],
        out_specs=[out_spec(half), out_spec(half), out_spec(PEER_NKEYS), out_spec(PEER_NKEYS)],
        out_shape=[out_sds(half), out_sds(half), out_sds(PEER_NKEYS), out_sds(PEER_NKEYS)],
        scratch_shapes=[pltpu.VMEM((2 * PEER_HEADS, tm, PEER_NKEYS), BF16)],
        compiler_params=_params(("parallel", "arbitrary")),
        name="peer_route",
    )(xn, wq_bf, keys_bf)


def _peer_kernel(xn_ref, u_ref, vt_ref, r2_ref, a2_ref, lb_ref, a1_ref, h_ref, gf_ref,
                 y_ref, acc_ref, *, final_norm):
    j = pl.program_id(1)
    te = vt_ref.shape[1]

    @pl.when(j == 0)
    def _():
        acc_ref[...] = jnp.zeros_like(acc_ref)

    tm = xn_ref.shape[1]

    def row_bf16(ref, hh, blk):
        word = jnp.broadcast_to(ref[hh, blk:blk + 1, :], (8, tm))
        tile = pltpu.bitcast(word, BF16)
        return jnp.concatenate([tile] * (PEER_NKEYS // tile.shape[0]), axis=0)

    sub = min(te, SUB_EXPERTS)
    ps = []
    for s0 in range(0, te, sub):
        u_rows = pltpu.bitcast(u_ref[s0 // 2:(s0 + sub) // 2, :], BF16)
        ht = jnp.dot(u_rows, xn_ref[...], preferred_element_type=F32)
        hb = ht.astype(BF16)
        act = (0.5 * hb) * (1.0 + lax.erf(hb * (1.0 / math.sqrt(2.0))))
        gates = []
        for bb in range(s0 // PEER_NKEYS, (s0 + sub) // PEER_NKEYS):
            g = jnp.zeros((PEER_NKEYS, tm), BF16)
            for hh in range(PEER_HEADS):
                lim = row_bf16(lb_ref, hh, bb)
                w1 = row_bf16(a1_ref, hh, bb)
                r2 = pltpu.bitcast(r2_ref[hh], BF16)
                a2 = pltpu.bitcast(a2_ref[hh], BF16)
                g = g + jnp.where(r2 < lim, w1 * a2, jnp.zeros_like(g))
            gates.append(g)
        ps.append(act.astype(BF16) * jnp.concatenate(gates, axis=0))
    p = jnp.concatenate(ps, axis=0)
    acc_ref[...] += jnp.dot(vt_ref[...], p, preferred_element_type=F32)

    @pl.when(j == pl.num_programs(1) - 1)
    def _():
        out = h_ref[...] + acc_ref[...].T
        y_ref[...] = _rms(out, gf_ref[...]) if final_norm else out


def _peer(xnt, u_words, vt_bf, r2, a2, lb, a1, h, gf, final_norm, tm, te):
    d, n = xnt.shape
    ne = 2 * u_words.shape[0]
    gate_spec = pl.BlockSpec((PEER_HEADS, PEER_NKEYS // 2, tm), lambda i, j: (0, 0, i))
    nblk = te // PEER_NKEYS
    assert nblk % 8 == 0, "first-key rows are delivered as whole sublane tiles"
    row_spec = pl.BlockSpec((PEER_HEADS, nblk, tm), lambda i, j: (0, j, i))
    return pl.pallas_call(
        functools.partial(_peer_kernel, final_norm=final_norm),
        grid=(n // tm, ne // te),
        in_specs=[pl.BlockSpec((d, tm), lambda i, j: (0, i)),
                  pl.BlockSpec((te // 2, d), lambda i, j: (j, 0)),
                  pl.BlockSpec((d, te), lambda i, j: (0, j)),
                  gate_spec, gate_spec, row_spec, row_spec,
                  pl.BlockSpec((tm, d), lambda i, j: (i, 0)),
                  pl.BlockSpec((1, d), lambda i, j: (0, 0))],
        out_specs=pl.BlockSpec((tm, d), lambda i, j: (i, 0)),
        out_shape=jax.ShapeDtypeStruct((n, d), F32),
        scratch_shapes=[pltpu.VMEM((d, tm), F32)],
        compiler_params=_params(("parallel", "arbitrary")),
        name="peer_experts",
    )(xnt, u_words, vt_bf, r2, a2, lb, a1, h, gf)


def _tile(n, pref):
    t = min(n, pref)
    assert n % t == 0, (n, t)
    return t


def kernel(x_prompt, x_sample, cache_win_k, cache_win_v, state_conv, norm1_g, w_in, conv_w,
           attn_out_g, conv_out_g, w_out, norm2_g, peer_wq, peer_sub_keys, peer_u, peer_v,
           final_norm_g):
    bp, s, d = x_prompt.shape
    bs, t, _ = x_sample.shape
    depth = w_in.shape[0]
    past_len = 8192
    win_p = min(BRANCHES[-1][0], s)
    np_, ns = bp * s, bs * t

    tm_p = _tile(s, 512)
    tq = _tile(s, 512)
    cos_p, sin_p = _rope_tables(jnp.arange(s, dtype=jnp.int32))
    pos_s = past_len + jnp.repeat(jnp.arange(t, dtype=jnp.int32), bs)
    cos_s, sin_s = _rope_tables(pos_s)

    hp = x_prompt.reshape(np_, d)
    hs = x_sample.transpose(1, 0, 2).reshape(ns, d)
    outs = {k: [] for k in ("kp", "vp", "cp", "ks", "vs", "cs")}
    row = lambda g: g.reshape(1, -1)
    y_p = y_s = None
    for l in range(depth):
        w_in_bf = w_in[l].astype(BF16)
        w_out_bf = w_out[l].astype(BF16)
        wq_bf = peer_wq[l].astype(BF16)
        keys_bf = peer_sub_keys[l].astype(BF16)
        u_words = _pack_pairs(peer_u[l])
        vt_bf = _pair_order(peer_v[l]).T.astype(BF16)
        last = l == depth - 1

        def tail(hres, attn, cb, z, state, stride, tiles_per_seq, tm):
            h_mid, xn2, xn2t = _mix(attn, cb, z, state, hres, conv_w[l], row(attn_out_g[l]),
                                    row(conv_out_g[l]), w_out_bf, row(norm2_g[l]), tm, stride,
                                    tiles_per_seq)
            n = hres.shape[0]
            r2, a2, lb, a1 = _route(xn2, wq_bf, keys_bf, _tile(n, 256))
            return _peer(xn2t, u_words, vt_bf, r2, a2, lb, a1, h_mid, row(final_norm_g), last,
                         _tile(n, 512), 2048)

        q, k, v, cb, z = _inproj(hp, row(norm1_g[l]), w_in_bf, cos_p, sin_p, tm_p)
        attn = _attn_prompt(q.reshape(bp, s, ATTN_W), k.reshape(bp, s, ATTN_W),
                            v.reshape(bp, s, ATTN_W), tq).reshape(np_, ATTN_W)
        outs["kp"].append(k.reshape(bp, s, N_HEADS, HEAD_DIM)[:, s - win_p:])
        outs["vp"].append(v.reshape(bp, s, N_HEADS, HEAD_DIM)[:, s - win_p:])
        outs["cp"].append(z.reshape(bp, s, CONV_W)[:, s - (CONV_K - 1):])
        zero_state = jnp.zeros((bp, 8, CONV_W), F32)
        y_p = tail(hp, attn, cb, z, zero_state, 1, s // tm_p, tm_p)

        q, k, v, cb, z = _inproj(hs, row(norm1_g[l]), w_in_bf, cos_s, sin_s, _tile(ns, 512))
        to_bt = lambda a: a.reshape(t, bs, -1).transpose(1, 0, 2)
        k_bt, v_bt = to_bt(k), to_bt(v)
        attn = _attn_sample(to_bt(q), k_bt, v_bt, cache_win_k[l], cache_win_v[l])
        attn = attn.transpose(1, 0, 2).reshape(ns, ATTN_W)
        outs["ks"].append(k_bt.reshape(bs, t, N_HEADS, HEAD_DIM))
        outs["vs"].append(v_bt.reshape(bs, t, N_HEADS, HEAD_DIM))
        outs["cs"].append(to_bt(z)[:, t - (CONV_K - 1):])
        state = state_conv[l].transpose(1, 0, 2).reshape(1, (CONV_K - 1) * bs, CONV_W)
        y_s = tail(hs, attn, cb, z, state, bs, 1, ns)
        hp, hs = y_p, y_s

    y_prompt = y_p.reshape(bp, s, d)
    y_sample = y_s.reshape(t, bs, d).transpose(1, 0, 2)
    st = lambda name: jnp.stack(outs[name])
    return (y_prompt, y_sample, st("kp"), st("vp"), st("cp"), st("ks"), st("vs"), st("cs"))
```

```python
import functools
import math

import jax
import jax.numpy as jnp
from jax import lax
from jax.experimental import pallas as pl
from jax.experimental.pallas import tpu as pltpu

F32 = jnp.float32
BF16 = jnp.bfloat16

HEAD_DIM = 64
N_HEADS = 12
ATTN_W = N_HEADS * HEAD_DIM
CONV_W = 256
CONV_K = 3
BRANCHES = ((128, 1), (512, 4), (2048, 16))
ROPE_THETA = 10000.0
EPS = 1e-6
PEER_HEADS = 8
PEER_NKEYS = 128
PEER_TOPK = 16
NEG = -1e30
LANES = 128
SUB_EXPERTS = 256
GATE_COLS = 512
VMEM_LIMIT = 56 * 1024 * 1024

_NT = (((1,), (1,)), ((), ()))


def _rms(x, g):
    return x * lax.rsqrt(jnp.mean(x * x, axis=-1, keepdims=True) + EPS) * g


def _pack_pairs(x):
    r, n = x.shape
    bits = lax.bitcast_convert_type(x.astype(BF16).astype(F32), jnp.uint32) >> 16
    bits = bits.reshape(r // 16, 2, 8, n)
    return (bits[:, 0] | (bits[:, 1] << 16)).reshape(r // 2, n)


def _pair_order(x):
    r = x.shape[0]
    return x.reshape(r // 16, 2, 8, -1).transpose(0, 2, 1, 3).reshape(x.shape)


def _params(sem):
    return pltpu.CompilerParams(dimension_semantics=sem, vmem_limit_bytes=VMEM_LIMIT)


def _inproj_kernel(x_ref, g_ref, w_ref, cos_ref, sin_ref,
                   q_ref, k_ref, v_ref, cb_ref, z_ref):
    xb = _rms(x_ref[...], g_ref[...]).astype(BF16)
    tm = xb.shape[0]
    lane = lax.broadcasted_iota(jnp.int32, (tm, LANES), 1)
    low_half = (lane % HEAD_DIM) < (HEAD_DIM // 2)
    cos = cos_ref[...]
    sin = sin_ref[...]

    def rope(t):
        partner = jnp.where(low_half, pltpu.roll(t, LANES - HEAD_DIM // 2, 1),
                            pltpu.roll(t, HEAD_DIM // 2, 1))
        return t * cos + partner * sin

    scale = 1.0 / math.sqrt(HEAD_DIM)
    q = jnp.dot(xb, w_ref[:, 0:ATTN_W], preferred_element_type=F32)
    for c in range(ATTN_W // LANES):
        sl = slice(c * LANES, (c + 1) * LANES)
        q_ref[:, sl] = (rope(q[:, sl]) * scale).astype(BF16)
    k = jnp.dot(xb, w_ref[:, ATTN_W:2 * ATTN_W], preferred_element_type=F32)
    for c in range(ATTN_W // LANES):
        sl = slice(c * LANES, (c + 1) * LANES)
        k_ref[:, sl] = rope(k[:, sl])
    v_ref[...] = jnp.dot(xb, w_ref[:, 2 * ATTN_W:3 * ATTN_W], preferred_element_type=F32)
    c0 = 3 * ATTN_W
    conv = jnp.dot(xb, w_ref[:, c0:c0 + 3 * CONV_W], preferred_element_type=F32)
    cb_ref[...] = conv[:, 0:CONV_W]
    z_ref[...] = conv[:, CONV_W:2 * CONV_W] * conv[:, 2 * CONV_W:3 * CONV_W]


def _inproj(x, g, w_bf, cos, sin, tm):
    n, d = x.shape
    win = w_bf.shape[1]
    period = cos.shape[0] // tm
    row = lambda i: (i, 0)
    const = lambda i: (0, 0)
    return pl.pallas_call(
        _inproj_kernel,
        grid=(n // tm,),
        in_specs=[pl.BlockSpec((tm, d), row),
                  pl.BlockSpec((1, d), const),
                  pl.BlockSpec((d, win), const),
                  pl.BlockSpec((tm, LANES), lambda i: (i % period, 0)),
                  pl.BlockSpec((tm, LANES), lambda i: (i % period, 0))],
        out_specs=[pl.BlockSpec((tm, ATTN_W), row), pl.BlockSpec((tm, ATTN_W), row),
                   pl.BlockSpec((tm, ATTN_W), row), pl.BlockSpec((tm, CONV_W), row),
                   pl.BlockSpec((tm, CONV_W), row)],
        out_shape=[jax.ShapeDtypeStruct((n, ATTN_W), BF16),
                   jax.ShapeDtypeStruct((n, ATTN_W), F32),
                   jax.ShapeDtypeStruct((n, ATTN_W), F32),
                   jax.ShapeDtypeStruct((n, CONV_W), F32),
                   jax.ShapeDtypeStruct((n, CONV_W), F32)],
        compiler_params=_params(("parallel",)),
        name="inproj",
    )(x, g, w_bf, cos, sin)


def _rope_tables(pos):
    half = HEAD_DIM // 2
    inv = jnp.exp(-math.log(ROPE_THETA) * jnp.arange(half, dtype=F32) * (2.0 / HEAD_DIM))
    ang = pos.astype(F32)[:, None] * inv[None, :]
    cos = jnp.cos(ang)
    sin = jnp.sin(ang)
    reps = LANES // HEAD_DIM
    cos_t = jnp.tile(jnp.concatenate([cos, cos], axis=1), (1, reps))
    sin_t = jnp.tile(jnp.concatenate([-sin, sin], axis=1), (1, reps))
    return cos_t, sin_t


def _branch_count(dist):
    cnt = jnp.zeros(dist.shape, F32)
    for window, dil in BRANCHES:
        hit = (dist <= window) & ((dist & (dil - 1)) == 0)
        cnt = cnt + jnp.where(hit, 1.0, 0.0)
    return jnp.where(dist >= 0, cnt, 0.0)


def _attn_prompt_kernel(q_ref, k_ref, v_ref, bias_ref, o_ref,
                        kt_ref, vz_ref, m_ref, l_ref, acc_ref, *, tq):
    qi = pl.program_id(2)
    heads = LANES // HEAD_DIM
    lane = lax.broadcasted_iota(jnp.int32, (tq, LANES), 1)
    head_of_lane = lane // HEAD_DIM

    @pl.when(qi == 0)
    def _():
        for jb in range(k_ref.shape[0] // tq):
            rows = slice(jb * tq, (jb + 1) * tq)
            kt_ref[jb] = k_ref[rows, :].T.astype(BF16)
            vb = v_ref[rows, :]
            for h in range(heads):
                vz_ref[heads * jb + h] = jnp.where(head_of_lane == h, vb, 0.0).astype(BF16)

    q = q_ref[...]
    qz = [jnp.where(head_of_lane == h, q, jnp.zeros_like(q)) for h in range(heads)]
    m_ref[...] = jnp.full(m_ref.shape, NEG, F32)
    l_ref[...] = jnp.zeros(l_ref.shape, F32)
    acc_ref[...] = jnp.zeros(acc_ref.shape, F32)

    def body(j, _):
        bias = bias_ref[qi - j]
        kt = kt_ref[j]
        alphas = []
        pv = None
        for h in range(heads):
            s = jnp.dot(qz[h], kt, preferred_element_type=F32) + bias
            m_prev = m_ref[h]
            m_next = jnp.maximum(m_prev, jnp.max(s, axis=-1, keepdims=True))
            p = jnp.concatenate(
                [jnp.exp(s[:, c * LANES:(c + 1) * LANES] - m_next) for c in range(tq // LANES)],
                axis=-1)
            alpha = jnp.exp(m_prev - m_next)
            l_ref[h] = alpha * l_ref[h] + jnp.sum(p, axis=-1, keepdims=True)
            m_ref[h] = m_next
            alphas.append(alpha)
            d = jnp.dot(p.astype(BF16), vz_ref[heads * j + h], preferred_element_type=F32)
            pv = d if pv is None else pv + d
        alpha_both = jnp.where(head_of_lane == 0, alphas[0], alphas[1])
        acc_ref[...] = alpha_both * acc_ref[...] + pv
        return 0

    lax.fori_loop(0, qi + 1, body, 0)
    l_both = jnp.where(head_of_lane == 0, l_ref[0], l_ref[1])
    o_ref[...] = acc_ref[...] / l_both


def _attn_bias(s, tq):
    d = jnp.arange(s // tq, dtype=jnp.int32)[:, None, None] * tq
    dist = d + jnp.arange(tq, dtype=jnp.int32)[None, :, None] \
        - jnp.arange(tq, dtype=jnp.int32)[None, None, :]
    cnt = _branch_count(dist)
    return jnp.where(cnt > 0.0, jnp.log(jnp.maximum(cnt, 1.0)), NEG)


def _attn_prompt(q, k, v, tq):
    b, s, _ = q.shape
    heads = LANES // HEAD_DIM
    assert heads == 2 and tq % LANES == 0
    nkb = s // tq
    blk = lambda bi, hp, qi: (bi, qi, hp)
    full = lambda bi, hp, qi: (bi, 0, hp)
    return pl.pallas_call(
        functools.partial(_attn_prompt_kernel, tq=tq),
        grid=(b, ATTN_W // LANES, nkb),
        in_specs=[pl.BlockSpec((None, tq, LANES), blk),
                  pl.BlockSpec((None, s, LANES), full),
                  pl.BlockSpec((None, s, LANES), full),
                  pl.BlockSpec((nkb, tq, tq), lambda bi, hp, qi: (0, 0, 0))],
        out_specs=pl.BlockSpec((None, tq, LANES), blk),
        out_shape=jax.ShapeDtypeStruct((b, s, ATTN_W), F32),
        scratch_shapes=[pltpu.VMEM((nkb, LANES, tq), BF16),
                        pltpu.VMEM((nkb * heads, tq, LANES), BF16),
                        pltpu.VMEM((heads, tq, LANES), F32),
                        pltpu.VMEM((heads, tq, LANES), F32),
                        pltpu.VMEM((tq, LANES), F32)],
        compiler_params=_params(("parallel", "parallel", "arbitrary")),
        name="attn_prompt",
    )(q, k, v, _attn_bias(s, tq))


def _attn_sample_kernel(q_ref, kn_ref, vn_ref, kc_ref, vc_ref, o_ref, *, chunk):
    t = q_ref.shape[0]
    n_past = kc_ref.shape[0]
    rows = N_HEADS * t
    qt = jnp.concatenate([q_ref[...]] * N_HEADS, axis=0)
    r_id = lax.broadcasted_iota(jnp.int32, (rows, ATTN_W), 0)
    c_id = lax.broadcasted_iota(jnp.int32, (rows, ATTN_W), 1)
    own = (r_id // t) == (c_id // HEAD_DIM)
    qbd = jnp.where(own, qt, jnp.zeros_like(qt))

    def scores(kblk, first_key):
        nk = kblk.shape[0]
        s = lax.dot_general(qbd, kblk.astype(BF16), _NT, preferred_element_type=F32)
        qpos = n_past + lax.broadcasted_iota(jnp.int32, (rows, nk), 0) % t
        kpos = first_key + lax.broadcasted_iota(jnp.int32, (rows, nk), 1)
        cnt = _branch_count(qpos - kpos)
        return jnp.where(cnt > 0.0, s, NEG), cnt

    parts = [scores(kc_ref[c * chunk:(c + 1) * chunk, :], c * chunk)
             for c in range(n_past // chunk)]
    parts.append(scores(kn_ref[...], n_past))
    m = functools.reduce(jnp.maximum, [jnp.max(s, axis=-1, keepdims=True) for s, _ in parts])
    den = jnp.zeros((rows, 1), F32)
    acc = jnp.zeros((rows, ATTN_W), F32)
    for c, (s, cnt) in enumerate(parts):
        p = cnt * jnp.exp(s - m)
        den = den + jnp.sum(p, axis=-1, keepdims=True)
        vblk = vn_ref[...] if c == len(parts) - 1 else vc_ref[c * chunk:(c + 1) * chunk, :]
        acc = acc + jnp.dot(p.astype(BF16), vblk.astype(BF16), preferred_element_type=F32)
    o_full = jnp.where(own, acc / den, 0.0)
    out = o_full[0:t, :]
    for h in range(1, N_HEADS):
        out = out + o_full[h * t:(h + 1) * t, :]
    o_ref[...] = out


def _attn_sample(q, k_new, v_new, k_cache, v_cache):
    b, t, _ = q.shape
    per_b = lambda bi: (bi, 0, 0)
    blk = lambda a: pl.BlockSpec((None,) + a.shape[1:], per_b)
    args = (q, k_new, v_new, k_cache, v_cache)
    return pl.pallas_call(
        functools.partial(_attn_sample_kernel, chunk=512),
        grid=(b,),
        in_specs=[blk(a) for a in args],
        out_specs=pl.BlockSpec((None, t, ATTN_W), per_b),
        out_shape=jax.ShapeDtypeStruct((b, t, ATTN_W), F32),
        compiler_params=_params(("parallel",)),
        name="attn_sample",
    )(*args)


def _mix_kernel(attn_ref, cb_ref, z_ref, zprev_ref, state_ref, x_ref, cw_ref, ga_ref, gc_ref,
                wo_ref, g2_ref, h_ref, xn_ref, xnt_ref, zbuf, *, stride, tiles_per_seq):
    tm = z_ref.shape[0]
    hb = state_ref.shape[0]
    if tiles_per_seq > 1:
        first = (pl.program_id(0) % tiles_per_seq) == 0
        zbuf[0:hb, :] = jnp.where(first, state_ref[...], zprev_ref[...])
    else:
        zbuf[0:hb, :] = state_ref[...]
    z = z_ref[...]
    zbuf[hb:hb + tm, :] = z
    cw = cw_ref[...]
    y = (cw[0:1, :] * zbuf[hb - 2 * stride:hb - 2 * stride + tm, :]
         + cw[1:2, :] * zbuf[hb - stride:hb - stride + tm, :]
         + cw[2:3, :] * z)
    conv_out = cb_ref[...] * y
    a = _rms(attn_ref[...], ga_ref[...]).astype(BF16)
    c = _rms(conv_out, gc_ref[...]).astype(BF16)
    mixed = (jnp.dot(a, wo_ref[0:ATTN_W, :], preferred_element_type=F32)
             + jnp.dot(c, wo_ref[ATTN_W:ATTN_W + CONV_W, :], preferred_element_type=F32))
    h = x_ref[...] + mixed
    h_ref[...] = h
    xn = _rms(h, g2_ref[...])
    xn_ref[...] = xn.astype(BF16)
    xnt_ref[...] = xn.T.astype(BF16)


def _mix(attn, cb, z, state, x, conv_w, ga, gc, wo_bf, g2, tm, stride, tiles_per_seq):
    n, d = x.shape
    hb = state.shape[1]
    row = lambda i: (i, 0)
    const = lambda i: (0, 0)
    prev = lambda i: (jnp.maximum(i * (tm // hb) - 1, 0), 0)
    return pl.pallas_call(
        functools.partial(_mix_kernel, stride=stride, tiles_per_seq=tiles_per_seq),
        grid=(n // tm,),
        in_specs=[pl.BlockSpec((tm, ATTN_W), row),
                  pl.BlockSpec((tm, CONV_W), row),
                  pl.BlockSpec((tm, CONV_W), row),
                  pl.BlockSpec((hb, CONV_W), prev),
                  pl.BlockSpec((None, hb, CONV_W), lambda i: (i // tiles_per_seq, 0, 0)),
                  pl.BlockSpec((tm, d), row),
                  pl.BlockSpec((CONV_K, CONV_W), const),
                  pl.BlockSpec((1, ATTN_W), const),
                  pl.BlockSpec((1, CONV_W), const),
                  pl.BlockSpec((d, d), const),
                  pl.BlockSpec((1, d), const)],
        out_specs=[pl.BlockSpec((tm, d), row), pl.BlockSpec((tm, d), row),
                   pl.BlockSpec((d, tm), lambda i: (0, i))],
        out_shape=[jax.ShapeDtypeStruct((n, d), F32), jax.ShapeDtypeStruct((n, d), BF16),
                   jax.ShapeDtypeStruct((d, n), BF16)],
        scratch_shapes=[pltpu.VMEM((hb + tm, CONV_W), F32)],
        compiler_params=_params(("parallel",)),
        name="mix",
    )(attn, cb, z, z, state, x, conv_w, ga, gc, wo_bf, g2)


def _top16(s, order):
    unit = 2.0 ** 120
    vals = []
    for r in range(PEER_TOPK):
        m = jnp.max(s, axis=0, keepdims=True)
        if order is None:
            sel = s == m
        else:
            big = jnp.int32(2 ** 30)
            sel = order == jnp.min(jnp.where(s == m, order, big), axis=0, keepdims=True)
        s = jnp.where(sel, -(128.0 + r) * unit, s)
        vals.append(m)
    rank = jnp.where(s <= -128.0 * unit, s * (-1.0 / unit) - 128.0, float(PEER_TOPK))
    return rank, jnp.concatenate(vals, axis=0)


def _top16_paired(s):
    unit = 2.0 ** 120
    mark = -128.0 * unit
    n2 = s.shape[0] // 2
    a, b = s[:n2], s[n2:]
    a_first = a >= b
    hi, lo = jnp.maximum(a, b), jnp.minimum(a, b)
    vals = []
    for r in range(PEER_TOPK):
        m = jnp.max(hi, axis=0, keepdims=True)
        sel = hi == m
        hi = jnp.where(sel, lo, hi)
        lo = jnp.where(sel, -(128.0 + r) * unit, lo)
        vals.append(m)
    decode = lambda x: x * (-1.0 / unit) - 128.0
    none = float(PEER_TOPK)
    twice = hi <= mark
    rank_big = jnp.where(twice, decode(hi), jnp.where(lo <= mark, decode(lo), none))
    rank_small = jnp.where(twice, decode(lo), none)
    rank = jnp.concatenate([jnp.where(a_first, rank_big, rank_small),
                            jnp.where(a_first, rank_small, rank_big)], axis=0)
    return rank, jnp.concatenate(vals, axis=0)


_CAND_GROUPS = ((0, 0), (0, 8), (1, 0), (2, 0), (3, 0), (4, 0), (5, 0), (6, 0), (7, 0))


def _route_tile(s1, s2, exact):
    tm = s1.shape[1]
    sub = 8
    if exact:
        iota128 = lax.broadcasted_iota(jnp.int32, s1.shape, 0)
        top16 = lambda s, order: _top16(s, iota128 if order is None else order)
    else:
        top16 = lambda s, order: _top16_paired(s)
    rank1, sv1 = top16(s1, None)
    rank2, sv2 = top16(s2, None)
    groups = [sv1[a:a + 1, :] + sv2[b0:b0 + sub, :] for a, b0 in _CAND_GROUPS]
    groups.append(sv1[sub:2 * sub, :] + sv2[0:1, :])
    cand = jnp.concatenate(groups, axis=0)
    order = None
    if exact:
        i8 = lax.broadcasted_iota(jnp.int32, (sub, tm), 0)
        order = jnp.concatenate([a * PEER_TOPK + b0 + i8 for a, b0 in _CAND_GROUPS]
                                + [(sub + i8) * PEER_TOPK], axis=0)
    crank, _ = top16(cand, order)
    chosen = crank < float(PEER_TOPK)
    cmax = sv1[0:1, :] + sv2[0:1, :]
    zsum = jnp.sum(jnp.where(chosen, jnp.exp(cand - cmax), 0.0), axis=0, keepdims=True)
    picked = jnp.where(chosen, 1.0, 0.0)
    n_of_a = [jnp.sum(picked[0:2 * sub, :], axis=0, keepdims=True)]
    for a in range(1, sub):
        n_of_a.append(jnp.sum(picked[(a + 1) * sub:(a + 2) * sub, :], axis=0, keepdims=True))
    last = (len(_CAND_GROUPS)) * sub
    for a in range(sub, PEER_TOPK):
        n_of_a.append(picked[last + a - sub:last + a - sub + 1, :])
    lb = jnp.zeros(rank1.shape, F32)
    for a in range(PEER_TOPK):
        lb = jnp.where(rank1 == float(a), n_of_a[a], lb)
    a2 = jnp.exp(s2 - sv2[0:1, :])
    a1 = jnp.exp(s1 - sv1[0:1, :]) / zsum
    tied = None
    if not exact:
        def extra(rk):
            n_sel = jnp.sum(jnp.where(rk < float(PEER_TOPK), 1.0, 0.0), axis=0, keepdims=True)
            return jnp.max(n_sel) > float(PEER_TOPK)
        tied = extra(rank1) | extra(rank2) | extra(crank)
    return rank2, a2, lb, a1, tied


def _route_kernel(xn_ref, wq_ref, keys_ref, r2_ref, a2_ref, lb_ref, a1_ref, q_scr):
    h = pl.program_id(1)
    nsub = 2 * PEER_HEADS

    @pl.when(h == 0)
    def _():
        q = jnp.dot(xn_ref[...], wq_ref[...], preferred_element_type=F32)
        for c in range(nsub):
            q_scr[c] = q[:, c * PEER_NKEYS:(c + 1) * PEER_NKEYS].astype(BF16)

    s1 = lax.dot_general(keys_ref[0], q_scr[2 * h], _NT, preferred_element_type=F32)
    s2 = lax.dot_general(keys_ref[1], q_scr[2 * h + 1], _NT, preferred_element_type=F32)

    def twice(x):
        hi = lax.bitcast_convert_type(x.astype(BF16).astype(F32), jnp.uint32)
        return hi | (hi >> 16)

    def emit(rank2, a2, lb, a1):
        r2_ref[...] = _pack_pairs(rank2)
        a2_ref[...] = _pack_pairs(a2)
        lb_ref[...] = twice(lb)
        a1_ref[...] = twice(a1)

    *outs, tied = _route_tile(s1, s2, exact=False)
    emit(*outs)

    @pl.when(tied)
    def _():
        emit(*_route_tile(s1, s2, exact=True)[:4])


def _route(xn, wq_bf, keys_bf, tm):
    n, d = xn.shape
    qd = wq_bf.shape[1]
    out_spec = lambda rows: pl.BlockSpec((None, rows, tm), lambda i, h: (h, 0, i))
    out_sds = lambda rows: jax.ShapeDtypeStruct((PEER_HEADS, rows, n), jnp.uint32)
    half = PEER_NKEYS // 2
    return pl.pallas_call(
        _route_kernel,
        grid=(n // tm, PEER_HEADS),
        in_specs=[pl.BlockSpec((tm, d), lambda i, h: (i, 0)),
                  pl.BlockSpec((d, qd), lambda i, h: (0, 0)),
                  pl.BlockSpec((None, 2, PEER_NKEYS, PEER_NKEYS), lambda i, h: (h, 0, 0, 0))],
        out_specs=[out_spec(half), out_spec(half), out_spec(PEER_NKEYS), out_spec(PEER_NKEYS)],
        out_shape=[out_sds(half), out_sds(half), out_sds(PEER_NKEYS), out_sds(PEER_NKEYS)],
        scratch_shapes=[pltpu.VMEM((2 * PEER_HEADS, tm, PEER_NKEYS), BF16)],
        compiler_params=_params(("parallel", "arbitrary")),
        name="peer_route",
    )(xn, wq_bf, keys_bf)


def _peer_kernel(xn_ref, u_ref, vt_ref, r2_ref, a2_ref, lb_ref, a1_ref, h_ref, gf_ref,
                 y_ref, acc_ref, *, final_norm):
    j = pl.program_id(1)
    te = vt_ref.shape[1]

    @pl.when(j == 0)
    def _():
        acc_ref[...] = jnp.zeros_like(acc_ref)

    tm = xn_ref.shape[1]

    def row_bf16(ref, hh, blk, cols):
        word = jnp.broadcast_to(ref[hh, blk:blk + 1, cols], (8, cols.stop - cols.start))
        tile = pltpu.bitcast(word, BF16)
        return jnp.concatenate([tile] * (PEER_NKEYS // tile.shape[0]), axis=0)

    sub = min(te, SUB_EXPERTS)
    ps = []
    for s0 in range(0, te, sub):
        u_rows = pltpu.bitcast(u_ref[s0 // 2:(s0 + sub) // 2, :], BF16)
        ht = jnp.dot(u_rows, xn_ref[...], preferred_element_type=F32)
        hb = ht.astype(BF16)
        act = (0.5 * hb) * (1.0 + lax.erf(hb * (1.0 / math.sqrt(2.0))))
        blocks = range(s0 // PEER_NKEYS, (s0 + sub) // PEER_NKEYS)
        gate_cols = []
        for c0 in range(0, tm, GATE_COLS):
            cols = slice(c0, min(c0 + GATE_COLS, tm))
            g = [jnp.zeros((PEER_NKEYS, cols.stop - c0), BF16) for _ in blocks]
            for hh in range(PEER_HEADS):
                r2 = pltpu.bitcast(r2_ref[hh, :, cols], BF16)
                a2 = pltpu.bitcast(a2_ref[hh, :, cols], BF16)
                for n, bb in enumerate(blocks):
                    lim = row_bf16(lb_ref, hh, bb, cols)
                    w1 = row_bf16(a1_ref, hh, bb, cols)
                    g[n] = g[n] + jnp.where(r2 < lim, w1 * a2, jnp.zeros_like(r2))
            gate_cols.append(jnp.concatenate(g, axis=0))
        ps.append(act.astype(BF16) * jnp.concatenate(gate_cols, axis=1))
    p = jnp.concatenate(ps, axis=0)
    acc_ref[...] += jnp.dot(vt_ref[...], p, preferred_element_type=F32)

    @pl.when(j == pl.num_programs(1) - 1)
    def _():
        out = h_ref[...] + acc_ref[...].T
        y_ref[...] = _rms(out, gf_ref[...]) if final_norm else out


def _peer(xnt, u_words, vt_bf, r2, a2, lb, a1, h, gf, final_norm, tm, te):
    d, n = xnt.shape
    ne = 2 * u_words.shape[0]
    gate_spec = pl.BlockSpec((PEER_HEADS, PEER_NKEYS // 2, tm), lambda i, j: (0, 0, i))
    nblk = te // PEER_NKEYS
    assert nblk % 8 == 0, "first-key rows are delivered as whole sublane tiles"
    row_spec = pl.BlockSpec((PEER_HEADS, nblk, tm), lambda i, j: (0, j, i))
    return pl.pallas_call(
        functools.partial(_peer_kernel, final_norm=final_norm),
        grid=(n // tm, ne // te),
        in_specs=[pl.BlockSpec((d, tm), lambda i, j: (0, i)),
                  pl.BlockSpec((te // 2, d), lambda i, j: (j, 0)),
                  pl.BlockSpec((d, te), lambda i, j: (0, j)),
                  gate_spec, gate_spec, row_spec, row_spec,
                  pl.BlockSpec((tm, d), lambda i, j: (i, 0)),
                  pl.BlockSpec((1, d), lambda i, j: (0, 0))],
        out_specs=pl.BlockSpec((tm, d), lambda i, j: (i, 0)),
        out_shape=jax.ShapeDtypeStruct((n, d), F32),
        scratch_shapes=[pltpu.VMEM((d, tm), F32)],
        compiler_params=_params(("parallel", "arbitrary")),
        name="peer_experts",
    )(xnt, u_words, vt_bf, r2, a2, lb, a1, h, gf)


def _tile(n, pref):
    t = min(n, pref)
    assert n % t == 0, (n, t)
    return t


def kernel(x_prompt, x_sample, cache_win_k, cache_win_v, state_conv, norm1_g, w_in, conv_w,
           attn_out_g, conv_out_g, w_out, norm2_g, peer_wq, peer_sub_keys, peer_u, peer_v,
           final_norm_g):
    bp, s, d = x_prompt.shape
    bs, t, _ = x_sample.shape
    depth = w_in.shape[0]
    past_len = 8192
    win_p = min(BRANCHES[-1][0], s)
    np_, ns = bp * s, bs * t

    tm_p = _tile(s, 512)
    tq = _tile(s, 512)
    cos_p, sin_p = _rope_tables(jnp.arange(s, dtype=jnp.int32))
    pos_s = past_len + jnp.repeat(jnp.arange(t, dtype=jnp.int32), bs)
    cos_s, sin_s = _rope_tables(pos_s)

    hp = x_prompt.reshape(np_, d)
    hs = x_sample.transpose(1, 0, 2).reshape(ns, d)
    outs = {k: [] for k in ("kp", "vp", "cp", "ks", "vs", "cs")}
    row = lambda g: g.reshape(1, -1)
    y_p = y_s = None
    for l in range(depth):
        w_in_bf = w_in[l].astype(BF16)
        w_out_bf = w_out[l].astype(BF16)
        wq_bf = peer_wq[l].astype(BF16)
        keys_bf = peer_sub_keys[l].astype(BF16)
        u_words = _pack_pairs(peer_u[l])
        vt_bf = _pair_order(peer_v[l]).T.astype(BF16)
        last = l == depth - 1

        def tail(hres, attn, cb, z, state, stride, tiles_per_seq, tm):
            h_mid, xn2, xn2t = _mix(attn, cb, z, state, hres, conv_w[l], row(attn_out_g[l]),
                                    row(conv_out_g[l]), w_out_bf, row(norm2_g[l]), tm, stride,
                                    tiles_per_seq)
            n = hres.shape[0]
            r2, a2, lb, a1 = _route(xn2, wq_bf, keys_bf, _tile(n, 256))
            return _peer(xn2t, u_words, vt_bf, r2, a2, lb, a1, h_mid, row(final_norm_g), last,
                         _tile(n, 512), 2048)

        q, k, v, cb, z = _inproj(hp, row(norm1_g[l]), w_in_bf, cos_p, sin_p, tm_p)
        attn = _attn_prompt(q.reshape(bp, s, ATTN_W), k.reshape(bp, s, ATTN_W),
                            v.reshape(bp, s, ATTN_W), tq).reshape(np_, ATTN_W)
        outs["kp"].append(k.reshape(bp, s, N_HEADS, HEAD_DIM)[:, s - win_p:])
        outs["vp"].append(v.reshape(bp, s, N_HEADS, HEAD_DIM)[:, s - win_p:])
        outs["cp"].append(z.reshape(bp, s, CONV_W)[:, s - (CONV_K - 1):])
        zero_state = jnp.zeros((bp, 8, CONV_W), F32)
        y_p = tail(hp, attn, cb, z, zero_state, 1, s // tm_p, tm_p)

        q, k, v, cb, z = _inproj(hs, row(norm1_g[l]), w_in_bf, cos_s, sin_s, _tile(ns, 512))
        to_bt = lambda a: a.reshape(t, bs, -1).transpose(1, 0, 2)
        k_bt, v_bt = to_bt(k), to_bt(v)
        attn = _attn_sample(to_bt(q), k_bt, v_bt,
                            cache_win_k[l].reshape(bs, -1, ATTN_W),
                            cache_win_v[l].reshape(bs, -1, ATTN_W))
        attn = attn.transpose(1, 0, 2).reshape(ns, ATTN_W)
        outs["ks"].append(k_bt.reshape(bs, t, N_HEADS, HEAD_DIM))
        outs["vs"].append(v_bt.reshape(bs, t, N_HEADS, HEAD_DIM))
        outs["cs"].append(to_bt(z)[:, t - (CONV_K - 1):])
        state = state_conv[l].transpose(1, 0, 2).reshape(1, (CONV_K - 1) * bs, CONV_W)
        y_s = tail(hs, attn, cb, z, state, bs, 1, ns)
        hp, hs = y_p, y_s

    y_prompt = y_p.reshape(bp, s, d)
    y_sample = y_s.reshape(t, bs, d).transpose(1, 0, 2)
    st = lambda name: jnp.stack(outs[name])
    return (y_prompt, y_sample, st("kp"), st("vp"), st("cp"), st("ks"), st("vs"), st("cs"))
```

```python
import functools
import math

import jax
import jax.numpy as jnp
from jax import lax
from jax.experimental import pallas as pl
from jax.experimental.pallas import tpu as pltpu

F32 = jnp.float32
BF16 = jnp.bfloat16

HEAD_DIM = 64
N_HEADS = 12
ATTN_W = N_HEADS * HEAD_DIM
CONV_W = 256
CONV_K = 3
BRANCHES = ((128, 1), (512, 4), (2048, 16))
ROPE_THETA = 10000.0
EPS = 1e-6
PEER_HEADS = 8
PEER_NKEYS = 128
PEER_TOPK = 16
NEG = -1e30
LANES = 128
SUB_EXPERTS = 256
GATE_COLS = 512
VMEM_LIMIT = 56 * 1024 * 1024

_NT = (((1,), (1,)), ((), ()))


def _rms(x, g):
    return x * lax.rsqrt(jnp.mean(x * x, axis=-1, keepdims=True) + EPS) * g


def _pack_pairs(x):
    r, n = x.shape
    bits = lax.bitcast_convert_type(x.astype(BF16).astype(F32), jnp.uint32) >> 16
    bits = bits.reshape(r // 16, 2, 8, n)
    return (bits[:, 0] | (bits[:, 1] << 16)).reshape(r // 2, n)


def _pair_order(x):
    r = x.shape[0]
    return x.reshape(r // 16, 2, 8, -1).transpose(0, 2, 1, 3).reshape(x.shape)


def _params(sem):
    return pltpu.CompilerParams(dimension_semantics=sem, vmem_limit_bytes=VMEM_LIMIT)


def _inproj_kernel(x_ref, g_ref, w_ref, cos_ref, sin_ref,
                   q_ref, k_ref, v_ref, cb_ref, z_ref):
    xb = _rms(x_ref[...], g_ref[...]).astype(BF16)
    tm = xb.shape[0]
    lane = lax.broadcasted_iota(jnp.int32, (tm, LANES), 1)
    low_half = (lane % HEAD_DIM) < (HEAD_DIM // 2)
    cos = cos_ref[...]
    sin = sin_ref[...]

    def rope(t):
        partner = jnp.where(low_half, pltpu.roll(t, LANES - HEAD_DIM // 2, 1),
                            pltpu.roll(t, HEAD_DIM // 2, 1))
        return t * cos + partner * sin

    scale = 1.0 / math.sqrt(HEAD_DIM)
    q = jnp.dot(xb, w_ref[:, 0:ATTN_W], preferred_element_type=F32)
    for c in range(ATTN_W // LANES):
        sl = slice(c * LANES, (c + 1) * LANES)
        q_ref[:, sl] = (rope(q[:, sl]) * scale).astype(BF16)
    k = jnp.dot(xb, w_ref[:, ATTN_W:2 * ATTN_W], preferred_element_type=F32)
    for c in range(ATTN_W // LANES):
        sl = slice(c * LANES, (c + 1) * LANES)
        k_ref[:, sl] = rope(k[:, sl])
    v_ref[...] = jnp.dot(xb, w_ref[:, 2 * ATTN_W:3 * ATTN_W], preferred_element_type=F32)
    c0 = 3 * ATTN_W
    conv = jnp.dot(xb, w_ref[:, c0:c0 + 3 * CONV_W], preferred_element_type=F32)
    cb_ref[...] = conv[:, 0:CONV_W]
    z_ref[...] = conv[:, CONV_W:2 * CONV_W] * conv[:, 2 * CONV_W:3 * CONV_W]


def _inproj(x, g, w_bf, cos, sin, tm):
    n, d = x.shape
    win = w_bf.shape[1]
    period = cos.shape[0] // tm
    row = lambda i: (i, 0)
    const = lambda i: (0, 0)
    return pl.pallas_call(
        _inproj_kernel,
        grid=(n // tm,),
        in_specs=[pl.BlockSpec((tm, d), row),
                  pl.BlockSpec((1, d), const),
                  pl.BlockSpec((d, win), const),
                  pl.BlockSpec((tm, LANES), lambda i: (i % period, 0)),
                  pl.BlockSpec((tm, LANES), lambda i: (i % period, 0))],
        out_specs=[pl.BlockSpec((tm, ATTN_W), row), pl.BlockSpec((tm, ATTN_W), row),
                   pl.BlockSpec((tm, ATTN_W), row), pl.BlockSpec((tm, CONV_W), row),
                   pl.BlockSpec((tm, CONV_W), row)],
        out_shape=[jax.ShapeDtypeStruct((n, ATTN_W), BF16),
                   jax.ShapeDtypeStruct((n, ATTN_W), F32),
                   jax.ShapeDtypeStruct((n, ATTN_W), F32),
                   jax.ShapeDtypeStruct((n, CONV_W), F32),
                   jax.ShapeDtypeStruct((n, CONV_W), F32)],
        compiler_params=_params(("parallel",)),
        name="inproj",
    )(x, g, w_bf, cos, sin)


def _rope_tables(pos):
    half = HEAD_DIM // 2
    inv = jnp.exp(-math.log(ROPE_THETA) * jnp.arange(half, dtype=F32) * (2.0 / HEAD_DIM))
    ang = pos.astype(F32)[:, None] * inv[None, :]
    cos = jnp.cos(ang)
    sin = jnp.sin(ang)
    reps = LANES // HEAD_DIM
    cos_t = jnp.tile(jnp.concatenate([cos, cos], axis=1), (1, reps))
    sin_t = jnp.tile(jnp.concatenate([-sin, sin], axis=1), (1, reps))
    return cos_t, sin_t


def _branch_count(dist):
    cnt = jnp.zeros(dist.shape, F32)
    for window, dil in BRANCHES:
        hit = (dist <= window) & ((dist & (dil - 1)) == 0)
        cnt = cnt + jnp.where(hit, 1.0, 0.0)
    return jnp.where(dist >= 0, cnt, 0.0)


def _attn_prompt_kernel(q_ref, k_ref, v_ref, bias_ref, o_ref,
                        kt_ref, vz_ref, m_ref, l_ref, acc_ref, *, tq, tk):
    qi = pl.program_id(2)
    heads = LANES // HEAD_DIM
    ratio = tq // tk
    head_of_lane = lax.broadcasted_iota(jnp.int32, (tq, LANES), 1) // HEAD_DIM
    head_of_key_lane = lax.broadcasted_iota(jnp.int32, (tk, LANES), 1) // HEAD_DIM

    @pl.when(qi == 0)
    def _():
        for jb in range(k_ref.shape[0] // tk):
            rows = slice(jb * tk, (jb + 1) * tk)
            kt_ref[jb] = k_ref[rows, :].T.astype(BF16)
            vb = v_ref[rows, :]
            for h in range(heads):
                vz_ref[heads * jb + h] = jnp.where(head_of_key_lane == h, vb, 0.0).astype(BF16)

    q = q_ref[...]
    qz = [jnp.where(head_of_lane == h, q, jnp.zeros_like(q)) for h in range(heads)]
    m_ref[...] = jnp.full(m_ref.shape, NEG, F32)
    l_ref[...] = jnp.zeros(l_ref.shape, F32)
    acc_ref[...] = jnp.zeros(acc_ref.shape, F32)

    def body(j, _):
        bias = bias_ref[qi * ratio + (ratio - 1) - j]
        kt = kt_ref[j]
        alphas = []
        pv = None
        for h in range(heads):
            s = jnp.dot(qz[h], kt, preferred_element_type=F32) + bias
            m_prev = m_ref[h]
            m_next = jnp.maximum(m_prev, jnp.max(s, axis=-1, keepdims=True))
            p = jnp.concatenate(
                [jnp.exp(s[:, c * LANES:(c + 1) * LANES] - m_next) for c in range(tk // LANES)],
                axis=-1)
            alpha = jnp.exp(m_prev - m_next)
            l_ref[h] = alpha * l_ref[h] + jnp.sum(p, axis=-1, keepdims=True)
            m_ref[h] = m_next
            alphas.append(alpha)
            d = jnp.dot(p.astype(BF16), vz_ref[heads * j + h], preferred_element_type=F32)
            pv = d if pv is None else pv + d
        alpha_both = jnp.where(head_of_lane == 0, alphas[0], alphas[1])
        acc_ref[...] = alpha_both * acc_ref[...] + pv
        return 0

    lax.fori_loop(0, (qi + 1) * ratio, body, 0)
    l_both = jnp.where(head_of_lane == 0, l_ref[0], l_ref[1])
    o_ref[...] = acc_ref[...] / l_both


def _attn_bias(s, tq, tk):
    first = jnp.arange(s // tk, dtype=jnp.int32)[:, None, None] - (tq // tk - 1)
    dist = first * tk + jnp.arange(tq, dtype=jnp.int32)[None, :, None] \
        - jnp.arange(tk, dtype=jnp.int32)[None, None, :]
    cnt = _branch_count(dist)
    return jnp.where(cnt > 0.0, jnp.log(jnp.maximum(cnt, 1.0)), NEG)


def _attn_prompt(q, k, v, tq, tk):
    b, s, _ = q.shape
    heads = LANES // HEAD_DIM
    assert heads == 2 and tk % LANES == 0 and tq % tk == 0
    nkb = s // tk
    blk = lambda bi, hp, qi: (bi, qi, hp)
    full = lambda bi, hp, qi: (bi, 0, hp)
    return pl.pallas_call(
        functools.partial(_attn_prompt_kernel, tq=tq, tk=tk),
        grid=(b, ATTN_W // LANES, s // tq),
        in_specs=[pl.BlockSpec((None, tq, LANES), blk),
                  pl.BlockSpec((None, s, LANES), full),
                  pl.BlockSpec((None, s, LANES), full),
                  pl.BlockSpec((nkb, tq, tk), lambda bi, hp, qi: (0, 0, 0))],
        out_specs=pl.BlockSpec((None, tq, LANES), blk),
        out_shape=jax.ShapeDtypeStruct((b, s, ATTN_W), F32),
        scratch_shapes=[pltpu.VMEM((nkb, LANES, tk), BF16),
                        pltpu.VMEM((nkb * heads, tk, LANES), BF16),
                        pltpu.VMEM((heads, tq, LANES), F32),
                        pltpu.VMEM((heads, tq, LANES), F32),
                        pltpu.VMEM((tq, LANES), F32)],
        compiler_params=_params(("parallel", "parallel", "arbitrary")),
        name="attn_prompt",
    )(q, k, v, _attn_bias(s, tq, tk))


def _attn_sample_kernel(q_ref, kn_ref, vn_ref, kc_ref, vc_ref, o_ref, *, chunk):
    t = q_ref.shape[0]
    n_past = kc_ref.shape[0]
    rows = N_HEADS * t
    qt = jnp.concatenate([q_ref[...]] * N_HEADS, axis=0)
    r_id = lax.broadcasted_iota(jnp.int32, (rows, ATTN_W), 0)
    c_id = lax.broadcasted_iota(jnp.int32, (rows, ATTN_W), 1)
    own = (r_id // t) == (c_id // HEAD_DIM)
    qbd = jnp.where(own, qt, jnp.zeros_like(qt))

    def scores(kblk, first_key):
        nk = kblk.shape[0]
        s = lax.dot_general(qbd, kblk.astype(BF16), _NT, preferred_element_type=F32)
        qpos = n_past + lax.broadcasted_iota(jnp.int32, (rows, nk), 0) % t
        kpos = first_key + lax.broadcasted_iota(jnp.int32, (rows, nk), 1)
        cnt = _branch_count(qpos - kpos)
        return jnp.where(cnt > 0.0, s, NEG), cnt

    parts = [scores(kc_ref[c * chunk:(c + 1) * chunk, :], c * chunk)
             for c in range(n_past // chunk)]
    parts.append(scores(kn_ref[...], n_past))
    m = functools.reduce(jnp.maximum, [jnp.max(s, axis=-1, keepdims=True) for s, _ in parts])
    den = jnp.zeros((rows, 1), F32)
    acc = jnp.zeros((rows, ATTN_W), F32)
    for c, (s, cnt) in enumerate(parts):
        p = cnt * jnp.exp(s - m)
        den = den + jnp.sum(p, axis=-1, keepdims=True)
        vblk = vn_ref[...] if c == len(parts) - 1 else vc_ref[c * chunk:(c + 1) * chunk, :]
        acc = acc + jnp.dot(p.astype(BF16), vblk.astype(BF16), preferred_element_type=F32)
    o_full = jnp.where(own, acc / den, 0.0)
    out = o_full[0:t, :]
    for h in range(1, N_HEADS):
        out = out + o_full[h * t:(h + 1) * t, :]
    o_ref[...] = out


def _attn_sample(q, k_new, v_new, k_cache, v_cache):
    b, t, _ = q.shape
    per_b = lambda bi: (bi, 0, 0)
    blk = lambda a: pl.BlockSpec((None,) + a.shape[1:], per_b)
    args = (q, k_new, v_new, k_cache, v_cache)
    return pl.pallas_call(
        functools.partial(_attn_sample_kernel, chunk=512),
        grid=(b,),
        in_specs=[blk(a) for a in args],
        out_specs=pl.BlockSpec((None, t, ATTN_W), per_b),
        out_shape=jax.ShapeDtypeStruct((b, t, ATTN_W), F32),
        compiler_params=_params(("parallel",)),
        name="attn_sample",
    )(*args)


def _mix_kernel(attn_ref, cb_ref, z_ref, zprev_ref, state_ref, x_ref, cw_ref, ga_ref, gc_ref,
                wo_ref, g2_ref, h_ref, xn_ref, xnt_ref, zbuf, *, stride, tiles_per_seq):
    tm = z_ref.shape[0]
    hb = state_ref.shape[0]
    if tiles_per_seq > 1:
        first = (pl.program_id(0) % tiles_per_seq) == 0
        zbuf[0:hb, :] = jnp.where(first, state_ref[...], zprev_ref[...])
    else:
        zbuf[0:hb, :] = state_ref[...]
    z = z_ref[...]
    zbuf[hb:hb + tm, :] = z
    cw = cw_ref[...]
    y = (cw[0:1, :] * zbuf[hb - 2 * stride:hb - 2 * stride + tm, :]
         + cw[1:2, :] * zbuf[hb - stride:hb - stride + tm, :]
         + cw[2:3, :] * z)
    conv_out = cb_ref[...] * y
    a = _rms(attn_ref[...], ga_ref[...]).astype(BF16)
    c = _rms(conv_out, gc_ref[...]).astype(BF16)
    mixed = (jnp.dot(a, wo_ref[0:ATTN_W, :], preferred_element_type=F32)
             + jnp.dot(c, wo_ref[ATTN_W:ATTN_W + CONV_W, :], preferred_element_type=F32))
    h = x_ref[...] + mixed
    h_ref[...] = h
    xn = _rms(h, g2_ref[...])
    xn_ref[...] = xn.astype(BF16)
    xnt_ref[...] = xn.T.astype(BF16)


def _mix(attn, cb, z, state, x, conv_w, ga, gc, wo_bf, g2, tm, stride, tiles_per_seq):
    n, d = x.shape
    hb = state.shape[1]
    row = lambda i: (i, 0)
    const = lambda i: (0, 0)
    prev = lambda i: (jnp.maximum(i * (tm // hb) - 1, 0), 0)
    return pl.pallas_call(
        functools.partial(_mix_kernel, stride=stride, tiles_per_seq=tiles_per_seq),
        grid=(n // tm,),
        in_specs=[pl.BlockSpec((tm, ATTN_W), row),
                  pl.BlockSpec((tm, CONV_W), row),
                  pl.BlockSpec((tm, CONV_W), row),
                  pl.BlockSpec((hb, CONV_W), prev),
                  pl.BlockSpec((None, hb, CONV_W), lambda i: (i // tiles_per_seq, 0, 0)),
                  pl.BlockSpec((tm, d), row),
                  pl.BlockSpec((CONV_K, CONV_W), const),
                  pl.BlockSpec((1, ATTN_W), const),
                  pl.BlockSpec((1, CONV_W), const),
                  pl.BlockSpec((d, d), const),
                  pl.BlockSpec((1, d), const)],
        out_specs=[pl.BlockSpec((tm, d), row), pl.BlockSpec((tm, d), row),
                   pl.BlockSpec((d, tm), lambda i: (0, i))],
        out_shape=[jax.ShapeDtypeStruct((n, d), F32), jax.ShapeDtypeStruct((n, d), BF16),
                   jax.ShapeDtypeStruct((d, n), BF16)],
        scratch_shapes=[pltpu.VMEM((hb + tm, CONV_W), F32)],
        compiler_params=_params(("parallel",)),
        name="mix",
    )(attn, cb, z, z, state, x, conv_w, ga, gc, wo_bf, g2)


def _top16(s, order):
    unit = 2.0 ** 120
    vals = []
    for r in range(PEER_TOPK):
        m = jnp.max(s, axis=0, keepdims=True)
        if order is None:
            sel = s == m
        else:
            big = jnp.int32(2 ** 30)
            sel = order == jnp.min(jnp.where(s == m, order, big), axis=0, keepdims=True)
        s = jnp.where(sel, -(128.0 + r) * unit, s)
        vals.append(m)
    rank = jnp.where(s <= -128.0 * unit, s * (-1.0 / unit) - 128.0, float(PEER_TOPK))
    return rank, jnp.concatenate(vals, axis=0)


def _top16_paired(s):
    unit = 2.0 ** 120
    mark = -128.0 * unit
    n2 = s.shape[0] // 2
    a, b = s[:n2], s[n2:]
    a_first = a >= b
    hi, lo = jnp.maximum(a, b), jnp.minimum(a, b)
    vals = []
    for r in range(PEER_TOPK):
        m = jnp.max(hi, axis=0, keepdims=True)
        sel = hi == m
        hi = jnp.where(sel, lo, hi)
        lo = jnp.where(sel, -(128.0 + r) * unit, lo)
        vals.append(m)
    decode = lambda x: x * (-1.0 / unit) - 128.0
    none = float(PEER_TOPK)
    twice = hi <= mark
    rank_big = jnp.where(twice, decode(hi), jnp.where(lo <= mark, decode(lo), none))
    rank_small = jnp.where(twice, decode(lo), none)
    rank = jnp.concatenate([jnp.where(a_first, rank_big, rank_small),
                            jnp.where(a_first, rank_small, rank_big)], axis=0)
    return rank, jnp.concatenate(vals, axis=0)


_CAND_GROUPS = ((0, 0), (0, 8), (1, 0), (2, 0), (3, 0), (4, 0), (5, 0), (6, 0), (7, 0))


def _route_tile(s1, s2, exact):
    tm = s1.shape[1]
    sub = 8
    if exact:
        iota128 = lax.broadcasted_iota(jnp.int32, s1.shape, 0)
        top16 = lambda s, order: _top16(s, iota128 if order is None else order)
    else:
        top16 = lambda s, order: _top16_paired(s)
    rank1, sv1 = top16(s1, None)
    rank2, sv2 = top16(s2, None)
    groups = [sv1[a:a + 1, :] + sv2[b0:b0 + sub, :] for a, b0 in _CAND_GROUPS]
    groups.append(sv1[sub:2 * sub, :] + sv2[0:1, :])
    cand = jnp.concatenate(groups, axis=0)
    order = None
    if exact:
        i8 = lax.broadcasted_iota(jnp.int32, (sub, tm), 0)
        order = jnp.concatenate([a * PEER_TOPK + b0 + i8 for a, b0 in _CAND_GROUPS]
                                + [(sub + i8) * PEER_TOPK], axis=0)
    crank, _ = top16(cand, order)
    chosen = crank < float(PEER_TOPK)
    cmax = sv1[0:1, :] + sv2[0:1, :]
    zsum = jnp.sum(jnp.where(chosen, jnp.exp(cand - cmax), 0.0), axis=0, keepdims=True)
    picked = jnp.where(chosen, 1.0, 0.0)
    n_of_a = [jnp.sum(picked[0:2 * sub, :], axis=0, keepdims=True)]
    for a in range(1, sub):
        n_of_a.append(jnp.sum(picked[(a + 1) * sub:(a + 2) * sub, :], axis=0, keepdims=True))
    last = (len(_CAND_GROUPS)) * sub
    for a in range(sub, PEER_TOPK):
        n_of_a.append(picked[last + a - sub:last + a - sub + 1, :])
    lb = jnp.zeros(rank1.shape, F32)
    for a in range(PEER_TOPK):
        lb = jnp.where(rank1 == float(a), n_of_a[a], lb)
    a2 = jnp.exp(s2 - sv2[0:1, :])
    a1 = jnp.exp(s1 - sv1[0:1, :]) / zsum
    tied = None
    if not exact:
        def extra(rk):
            n_sel = jnp.sum(jnp.where(rk < float(PEER_TOPK), 1.0, 0.0), axis=0, keepdims=True)
            return jnp.max(n_sel) > float(PEER_TOPK)
        tied = extra(rank1) | extra(rank2) | extra(crank)
    return rank2, a2, lb, a1, tied


def _route_kernel(xn_ref, wq_ref, keys_ref, r2_ref, a2_ref, lb_ref, a1_ref, q_scr):
    h = pl.program_id(1)
    nsub = 2 * PEER_HEADS

    @pl.when(h == 0)
    def _():
        q = jnp.dot(xn_ref[...], wq_ref[...], preferred_element_type=F32)
        for c in range(nsub):
            q_scr[c] = q[:, c * PEER_NKEYS:(c + 1) * PEER_NKEYS].astype(BF16)

    s1 = lax.dot_general(keys_ref[0], q_scr[2 * h], _NT, preferred_element_type=F32)
    s2 = lax.dot_general(keys_ref[1], q_scr[2 * h + 1], _NT, preferred_element_type=F32)

    def twice(x):
        hi = lax.bitcast_convert_type(x.astype(BF16).astype(F32), jnp.uint32)
        return hi | (hi >> 16)

    def emit(rank2, a2, lb, a1):
        r2_ref[...] = _pack_pairs(rank2)
        a2_ref[...] = _pack_pairs(a2)
        lb_ref[...] = twice(lb)
        a1_ref[...] = twice(a1)

    *outs, tied = _route_tile(s1, s2, exact=False)
    emit(*outs)

    @pl.when(tied)
    def _():
        emit(*_route_tile(s1, s2, exact=True)[:4])


def _route(xn, wq_bf, keys_bf, tm):
    n, d = xn.shape
    qd = wq_bf.shape[1]
    out_spec = lambda rows: pl.BlockSpec((None, rows, tm), lambda i, h: (h, 0, i))
    out_sds = lambda rows: jax.ShapeDtypeStruct((PEER_HEADS, rows, n), jnp.uint32)
    half = PEER_NKEYS // 2
    return pl.pallas_call(
        _route_kernel,
        grid=(n // tm, PEER_HEADS),
        in_specs=[pl.BlockSpec((tm, d), lambda i, h: (i, 0)),
                  pl.BlockSpec((d, qd), lambda i, h: (0, 0)),
                  pl.BlockSpec((None, 2, PEER_NKEYS, PEER_NKEYS), lambda i, h: (h, 0, 0, 0))],
        out_specs=[out_spec(half), out_spec(half), out_spec(PEER_NKEYS), out_spec(PEER_NKEYS)],
        out_shape=[out_sds(half), out_sds(half), out_sds(PEER_NKEYS), out_sds(PEER_NKEYS)],
        scratch_shapes=[pltpu.VMEM((2 * PEER_HEADS, tm, PEER_NKEYS), BF16)],
        compiler_params=_params(("parallel", "arbitrary")),
        name="peer_route",
    )(xn, wq_bf, keys_bf)


def _peer_kernel(xn_ref, u_ref, vt_ref, r2_ref, a2_ref, lb_ref, a1_ref, h_ref, gf_ref,
                 y_ref, acc_ref, *, final_norm):
    j = pl.program_id(1)
    te = vt_ref.shape[1]

    @pl.when(j == 0)
    def _():
        acc_ref[...] = jnp.zeros_like(acc_ref)

    tm = xn_ref.shape[1]

    def row_bf16(ref, hh, blk, cols):
        word = jnp.broadcast_to(ref[hh, blk:blk + 1, cols], (8, cols.stop - cols.start))
        tile = pltpu.bitcast(word, BF16)
        return jnp.concatenate([tile] * (PEER_NKEYS // tile.shape[0]), axis=0)

    sub = min(te, SUB_EXPERTS)
    ps = []
    for s0 in range(0, te, sub):
        u_rows = pltpu.bitcast(u_ref[s0 // 2:(s0 + sub) // 2, :], BF16)
        ht = jnp.dot(u_rows, xn_ref[...], preferred_element_type=F32)
        hb = ht.astype(BF16)
        act = (0.5 * hb) * (1.0 + lax.erf(hb * (1.0 / math.sqrt(2.0))))
        blocks = range(s0 // PEER_NKEYS, (s0 + sub) // PEER_NKEYS)
        gate_cols = []
        for c0 in range(0, tm, GATE_COLS):
            cols = slice(c0, min(c0 + GATE_COLS, tm))
            g = [jnp.zeros((PEER_NKEYS, cols.stop - c0), BF16) for _ in blocks]
            for hh in range(PEER_HEADS):
                r2 = pltpu.bitcast(r2_ref[hh, :, cols], BF16)
                a2 = pltpu.bitcast(a2_ref[hh, :, cols], BF16)
                for n, bb in enumerate(blocks):
                    lim = row_bf16(lb_ref, hh, bb, cols)
                    w1 = row_bf16(a1_ref, hh, bb, cols)
                    g[n] = g[n] + jnp.where(r2 < lim, w1 * a2, jnp.zeros_like(r2))
            gate_cols.append(jnp.concatenate(g, axis=0))
        ps.append(act.astype(BF16) * jnp.concatenate(gate_cols, axis=1))
    p = jnp.concatenate(ps, axis=0)
    acc_ref[...] += jnp.dot(vt_ref[...], p, preferred_element_type=F32)

    @pl.when(j == pl.num_programs(1) - 1)
    def _():
        out = h_ref[...] + acc_ref[...].T
        y_ref[...] = _rms(out, gf_ref[...]) if final_norm else out


def _peer(xnt, u_words, vt_bf, r2, a2, lb, a1, h, gf, final_norm, tm, te):
    d, n = xnt.shape
    ne = 2 * u_words.shape[0]
    gate_spec = pl.BlockSpec((PEER_HEADS, PEER_NKEYS // 2, tm), lambda i, j: (0, 0, i))
    nblk = te // PEER_NKEYS
    assert nblk % 8 == 0, "first-key rows are delivered as whole sublane tiles"
    row_spec = pl.BlockSpec((PEER_HEADS, nblk, tm), lambda i, j: (0, j, i))
    return pl.pallas_call(
        functools.partial(_peer_kernel, final_norm=final_norm),
        grid=(n // tm, ne // te),
        in_specs=[pl.BlockSpec((d, tm), lambda i, j: (0, i)),
                  pl.BlockSpec((te // 2, d), lambda i, j: (j, 0)),
                  pl.BlockSpec((d, te), lambda i, j: (0, j)),
                  gate_spec, gate_spec, row_spec, row_spec,
                  pl.BlockSpec((tm, d), lambda i, j: (i, 0)),
                  pl.BlockSpec((1, d), lambda i, j: (0, 0))],
        out_specs=pl.BlockSpec((tm, d), lambda i, j: (i, 0)),
        out_shape=jax.ShapeDtypeStruct((n, d), F32),
        scratch_shapes=[pltpu.VMEM((d, tm), F32)],
        compiler_params=_params(("parallel", "arbitrary")),
        name="peer_experts",
    )(xnt, u_words, vt_bf, r2, a2, lb, a1, h, gf)


def _tile(n, pref):
    t = min(n, pref)
    assert n % t == 0, (n, t)
    return t


def kernel(x_prompt, x_sample, cache_win_k, cache_win_v, state_conv, norm1_g, w_in, conv_w,
           attn_out_g, conv_out_g, w_out, norm2_g, peer_wq, peer_sub_keys, peer_u, peer_v,
           final_norm_g):
    bp, s, d = x_prompt.shape
    bs, t, _ = x_sample.shape
    depth = w_in.shape[0]
    past_len = 8192
    win_p = min(BRANCHES[-1][0], s)
    np_, ns = bp * s, bs * t

    tm_p = _tile(s, 512)
    tq = _tile(s, 512)
    tk = tq
    cos_p, sin_p = _rope_tables(jnp.arange(s, dtype=jnp.int32))
    pos_s = past_len + jnp.repeat(jnp.arange(t, dtype=jnp.int32), bs)
    cos_s, sin_s = _rope_tables(pos_s)

    hp = x_prompt.reshape(np_, d)
    hs = x_sample.transpose(1, 0, 2).reshape(ns, d)
    outs = {k: [] for k in ("kp", "vp", "cp", "ks", "vs", "cs")}
    row = lambda g: g.reshape(1, -1)
    y_p = y_s = None
    for l in range(depth):
        w_in_bf = w_in[l].astype(BF16)
        w_out_bf = w_out[l].astype(BF16)
        wq_bf = peer_wq[l].astype(BF16)
        keys_bf = peer_sub_keys[l].astype(BF16)
        u_words = _pack_pairs(peer_u[l])
        vt_bf = _pair_order(peer_v[l]).T.astype(BF16)
        last = l == depth - 1

        def tail(hres, attn, cb, z, state, stride, tiles_per_seq, tm):
            h_mid, xn2, xn2t = _mix(attn, cb, z, state, hres, conv_w[l], row(attn_out_g[l]),
                                    row(conv_out_g[l]), w_out_bf, row(norm2_g[l]), tm, stride,
                                    tiles_per_seq)
            n = hres.shape[0]
            r2, a2, lb, a1 = _route(xn2, wq_bf, keys_bf, _tile(n, 512))
            return _peer(xn2t, u_words, vt_bf, r2, a2, lb, a1, h_mid, row(final_norm_g), last,
                         _tile(n, 512), 2048)

        q, k, v, cb, z = _inproj(hp, row(norm1_g[l]), w_in_bf, cos_p, sin_p, tm_p)
        attn = _attn_prompt(q.reshape(bp, s, ATTN_W), k.reshape(bp, s, ATTN_W),
                            v.reshape(bp, s, ATTN_W), tq, tk).reshape(np_, ATTN_W)
        outs["kp"].append(k.reshape(bp, s, N_HEADS, HEAD_DIM)[:, s - win_p:])
        outs["vp"].append(v.reshape(bp, s, N_HEADS, HEAD_DIM)[:, s - win_p:])
        outs["cp"].append(z.reshape(bp, s, CONV_W)[:, s - (CONV_K - 1):])
        zero_state = jnp.zeros((bp, 8, CONV_W), F32)
        y_p = tail(hp, attn, cb, z, zero_state, 1, s // tm_p, tm_p)

        q, k, v, cb, z = _inproj(hs, row(norm1_g[l]), w_in_bf, cos_s, sin_s, _tile(ns, 512))
        to_bt = lambda a: a.reshape(t, bs, -1).transpose(1, 0, 2)
        k_bt, v_bt = to_bt(k), to_bt(v)
        attn = _attn_sample(to_bt(q), k_bt, v_bt,
                            cache_win_k[l].reshape(bs, -1, ATTN_W),
                            cache_win_v[l].reshape(bs, -1, ATTN_W))
        attn = attn.transpose(1, 0, 2).reshape(ns, ATTN_W)
        outs["ks"].append(k_bt.reshape(bs, t, N_HEADS, HEAD_DIM))
        outs["vs"].append(v_bt.reshape(bs, t, N_HEADS, HEAD_DIM))
        outs["cs"].append(to_bt(z)[:, t - (CONV_K - 1):])
        state = state_conv[l].transpose(1, 0, 2).reshape(1, (CONV_K - 1) * bs, CONV_W)
        y_s = tail(hs, attn, cb, z, state, bs, 1, ns)
        hp, hs = y_p, y_s

    y_prompt = y_p.reshape(bp, s, d)
    y_sample = y_s.reshape(t, bs, d).transpose(1, 0, 2)
    st = lambda name: jnp.stack(outs[name])
    return (y_prompt, y_sample, st("kp"), st("vp"), st("cp"), st("ks"), st("vs"), st("cs"))
```

```python
import functools
import math

import jax
import jax.numpy as jnp
from jax import lax
from jax.experimental import pallas as pl
from jax.experimental.pallas import tpu as pltpu

F32 = jnp.float32
BF16 = jnp.bfloat16

HEAD_DIM = 64
N_HEADS = 12
ATTN_W = N_HEADS * HEAD_DIM
CONV_W = 256
CONV_K = 3
BRANCHES = ((128, 1), (512, 4), (2048, 16))
ROPE_THETA = 10000.0
EPS = 1e-6
PEER_HEADS = 8
PEER_NKEYS = 128
PEER_TOPK = 16
NEG = -1e30
LANES = 128
SUB_EXPERTS = 256
GATE_COLS = 512
VMEM_LIMIT = 56 * 1024 * 1024

_NT = (((1,), (1,)), ((), ()))


def _rms(x, g):
    return x * lax.rsqrt(jnp.mean(x * x, axis=-1, keepdims=True) + EPS) * g


def _pack_pairs(x):
    r, n = x.shape
    bits = lax.bitcast_convert_type(x.astype(BF16).astype(F32), jnp.uint32) >> 16
    bits = bits.reshape(r // 16, 2, 8, n)
    return (bits[:, 0] | (bits[:, 1] << 16)).reshape(r // 2, n)


def _pair_order(x):
    r = x.shape[0]
    return x.reshape(r // 16, 2, 8, -1).transpose(0, 2, 1, 3).reshape(x.shape)


def _params(sem):
    return pltpu.CompilerParams(dimension_semantics=sem, vmem_limit_bytes=VMEM_LIMIT)


def _inproj_kernel(x_ref, g_ref, w_ref, cos_ref, sin_ref,
                   q_ref, k_ref, v_ref, cb_ref, z_ref):
    xb = _rms(x_ref[...], g_ref[...]).astype(BF16)
    tm = xb.shape[0]
    lane = lax.broadcasted_iota(jnp.int32, (tm, LANES), 1)
    low_half = (lane % HEAD_DIM) < (HEAD_DIM // 2)
    cos = cos_ref[...]
    sin = sin_ref[...]

    def rope(t):
        partner = jnp.where(low_half, pltpu.roll(t, LANES - HEAD_DIM // 2, 1),
                            pltpu.roll(t, HEAD_DIM // 2, 1))
        return t * cos + partner * sin

    scale = 1.0 / math.sqrt(HEAD_DIM)
    q = jnp.dot(xb, w_ref[:, 0:ATTN_W], preferred_element_type=F32)
    for c in range(ATTN_W // LANES):
        sl = slice(c * LANES, (c + 1) * LANES)
        q_ref[:, sl] = (rope(q[:, sl]) * scale).astype(BF16)
    k = jnp.dot(xb, w_ref[:, ATTN_W:2 * ATTN_W], preferred_element_type=F32)
    for c in range(ATTN_W // LANES):
        sl = slice(c * LANES, (c + 1) * LANES)
        k_ref[:, sl] = rope(k[:, sl])
    v_ref[...] = jnp.dot(xb, w_ref[:, 2 * ATTN_W:3 * ATTN_W], preferred_element_type=F32)
    c0 = 3 * ATTN_W
    conv = jnp.dot(xb, w_ref[:, c0:c0 + 3 * CONV_W], preferred_element_type=F32)
    cb_ref[...] = conv[:, 0:CONV_W]
    z_ref[...] = conv[:, CONV_W:2 * CONV_W] * conv[:, 2 * CONV_W:3 * CONV_W]


def _inproj(x, g, w_bf, cos, sin, tm):
    n, d = x.shape
    win = w_bf.shape[1]
    period = cos.shape[0] // tm
    row = lambda i: (i, 0)
    const = lambda i: (0, 0)
    return pl.pallas_call(
        _inproj_kernel,
        grid=(n // tm,),
        in_specs=[pl.BlockSpec((tm, d), row),
                  pl.BlockSpec((1, d), const),
                  pl.BlockSpec((d, win), const),
                  pl.BlockSpec((tm, LANES), lambda i: (i % period, 0)),
                  pl.BlockSpec((tm, LANES), lambda i: (i % period, 0))],
        out_specs=[pl.BlockSpec((tm, ATTN_W), row), pl.BlockSpec((tm, ATTN_W), row),
                   pl.BlockSpec((tm, ATTN_W), row), pl.BlockSpec((tm, CONV_W), row),
                   pl.BlockSpec((tm, CONV_W), row)],
        out_shape=[jax.ShapeDtypeStruct((n, ATTN_W), BF16),
                   jax.ShapeDtypeStruct((n, ATTN_W), F32),
                   jax.ShapeDtypeStruct((n, ATTN_W), F32),
                   jax.ShapeDtypeStruct((n, CONV_W), F32),
                   jax.ShapeDtypeStruct((n, CONV_W), F32)],
        compiler_params=_params(("parallel",)),
        name="inproj",
    )(x, g, w_bf, cos, sin)


def _rope_tables(pos):
    half = HEAD_DIM // 2
    inv = jnp.exp(-math.log(ROPE_THETA) * jnp.arange(half, dtype=F32) * (2.0 / HEAD_DIM))
    ang = pos.astype(F32)[:, None] * inv[None, :]
    cos = jnp.cos(ang)
    sin = jnp.sin(ang)
    reps = LANES // HEAD_DIM
    cos_t = jnp.tile(jnp.concatenate([cos, cos], axis=1), (1, reps))
    sin_t = jnp.tile(jnp.concatenate([-sin, sin], axis=1), (1, reps))
    return cos_t, sin_t


def _branch_count(dist):
    cnt = jnp.zeros(dist.shape, F32)
    for window, dil in BRANCHES:
        hit = (dist <= window) & ((dist & (dil - 1)) == 0)
        cnt = cnt + jnp.where(hit, 1.0, 0.0)
    return jnp.where(dist >= 0, cnt, 0.0)


def _attn_prompt_kernel(q_ref, k_ref, v_ref, bias_ref, o_ref,
                        kt_ref, vz_ref, m_ref, l_ref, acc_ref, *, tq, tk):
    qi = pl.program_id(2)
    heads = LANES // HEAD_DIM
    ratio = tq // tk
    head_of_lane = lax.broadcasted_iota(jnp.int32, (tq, LANES), 1) // HEAD_DIM
    head_of_key_lane = lax.broadcasted_iota(jnp.int32, (tk, LANES), 1) // HEAD_DIM

    @pl.when(qi == 0)
    def _():
        for jb in range(k_ref.shape[0] // tk):
            rows = slice(jb * tk, (jb + 1) * tk)
            kt_ref[jb] = k_ref[rows, :].T.astype(BF16)
            vb = v_ref[rows, :]
            for h in range(heads):
                vz_ref[heads * jb + h] = jnp.where(head_of_key_lane == h, vb, 0.0).astype(BF16)

    q = q_ref[...]
    qz = [jnp.where(head_of_lane == h, q, jnp.zeros_like(q)) for h in range(heads)]
    m_ref[...] = jnp.full(m_ref.shape, NEG, F32)
    l_ref[...] = jnp.zeros(l_ref.shape, F32)
    acc_ref[...] = jnp.zeros(acc_ref.shape, F32)

    def body(j, _):
        bias = bias_ref[qi * ratio + (ratio - 1) - j]
        kt = kt_ref[j]
        alphas = []
        pv = None
        for h in range(heads):
            s = jnp.dot(qz[h], kt, preferred_element_type=F32) + bias
            m_prev = m_ref[h]
            m_next = jnp.maximum(m_prev, jnp.max(s, axis=-1, keepdims=True))
            p = jnp.concatenate(
                [jnp.exp(s[:, c * LANES:(c + 1) * LANES] - m_next) for c in range(tk // LANES)],
                axis=-1)
            alpha = jnp.exp(m_prev - m_next)
            l_ref[h] = alpha * l_ref[h] + jnp.sum(p, axis=-1, keepdims=True)
            m_ref[h] = m_next
            alphas.append(alpha)
            d = jnp.dot(p.astype(BF16), vz_ref[heads * j + h], preferred_element_type=F32)
            pv = d if pv is None else pv + d
        alpha_both = jnp.where(head_of_lane == 0, alphas[0], alphas[1])
        acc_ref[...] = alpha_both * acc_ref[...] + pv
        return 0

    lax.fori_loop(0, (qi + 1) * ratio, body, 0)
    l_both = jnp.where(head_of_lane == 0, l_ref[0], l_ref[1])
    o_ref[...] = acc_ref[...] / l_both


def _attn_bias(s, tq, tk):
    first = jnp.arange(s // tk, dtype=jnp.int32)[:, None, None] - (tq // tk - 1)
    dist = first * tk + jnp.arange(tq, dtype=jnp.int32)[None, :, None] \
        - jnp.arange(tk, dtype=jnp.int32)[None, None, :]
    cnt = _branch_count(dist)
    return jnp.where(cnt > 0.0, jnp.log(jnp.maximum(cnt, 1.0)), NEG)


def _attn_prompt(q, k, v, tq, tk):
    b, s, _ = q.shape
    heads = LANES // HEAD_DIM
    assert heads == 2 and tk % LANES == 0 and tq % tk == 0
    nkb = s // tk
    blk = lambda bi, hp, qi: (bi, qi, hp)
    full = lambda bi, hp, qi: (bi, 0, hp)
    return pl.pallas_call(
        functools.partial(_attn_prompt_kernel, tq=tq, tk=tk),
        grid=(b, ATTN_W // LANES, s // tq),
        in_specs=[pl.BlockSpec((None, tq, LANES), blk),
                  pl.BlockSpec((None, s, LANES), full),
                  pl.BlockSpec((None, s, LANES), full),
                  pl.BlockSpec((nkb, tq, tk), lambda bi, hp, qi: (0, 0, 0))],
        out_specs=pl.BlockSpec((None, tq, LANES), blk),
        out_shape=jax.ShapeDtypeStruct((b, s, ATTN_W), F32),
        scratch_shapes=[pltpu.VMEM((nkb, LANES, tk), BF16),
                        pltpu.VMEM((nkb * heads, tk, LANES), BF16),
                        pltpu.VMEM((heads, tq, LANES), F32),
                        pltpu.VMEM((heads, tq, LANES), F32),
                        pltpu.VMEM((tq, LANES), F32)],
        compiler_params=_params(("parallel", "parallel", "arbitrary")),
        name="attn_prompt",
    )(q, k, v, _attn_bias(s, tq, tk))


def _attn_sample_kernel(q_ref, kn_ref, vn_ref, kc_ref, vc_ref, o_ref, m_ref, l_ref, acc_ref,
                        *, n_past):
    c = pl.program_id(1)
    t = q_ref.shape[0]
    chunk = kc_ref.shape[0]
    rows = N_HEADS * t
    qt = jnp.concatenate([q_ref[...]] * N_HEADS, axis=0)
    r_id = lax.broadcasted_iota(jnp.int32, (rows, ATTN_W), 0)
    c_id = lax.broadcasted_iota(jnp.int32, (rows, ATTN_W), 1)
    own = (r_id // t) == (c_id // HEAD_DIM)
    qbd = jnp.where(own, qt, jnp.zeros_like(qt))

    @pl.when(c == 0)
    def _():
        m_ref[...] = jnp.full(m_ref.shape, NEG, F32)
        l_ref[...] = jnp.zeros(l_ref.shape, F32)
        acc_ref[...] = jnp.zeros(acc_ref.shape, F32)

    def update(kblk, vblk, first_key):
        nk = kblk.shape[0]
        s = lax.dot_general(qbd, kblk.astype(BF16), _NT, preferred_element_type=F32)
        qpos = n_past + lax.broadcasted_iota(jnp.int32, (rows, nk), 0) % t
        kpos = first_key + lax.broadcasted_iota(jnp.int32, (rows, nk), 1)
        cnt = _branch_count(qpos - kpos)
        s = jnp.where(cnt > 0.0, s, NEG)
        m_prev = m_ref[...]
        m_next = jnp.maximum(m_prev, jnp.max(s, axis=-1, keepdims=True))
        alpha = jnp.exp(m_prev - m_next)
        p = cnt * jnp.exp(s - m_next)
        m_ref[...] = m_next
        l_ref[...] = alpha * l_ref[...] + jnp.sum(p, axis=-1, keepdims=True)
        acc_ref[...] = alpha * acc_ref[...] + jnp.dot(p.astype(BF16), vblk.astype(BF16),
                                                      preferred_element_type=F32)

    update(kc_ref[...].reshape(chunk, ATTN_W), vc_ref[...].reshape(chunk, ATTN_W), c * chunk)

    @pl.when(c == pl.num_programs(1) - 1)
    def _():
        update(kn_ref[...], vn_ref[...], n_past)
        o_full = jnp.where(own, acc_ref[...] / l_ref[...], 0.0)
        out = o_full[0:t, :]
        for h in range(1, N_HEADS):
            out = out + o_full[h * t:(h + 1) * t, :]
        o_ref[...] = out


def _attn_sample(q, k_new, v_new, k_cache, v_cache, chunk):
    b, t, _ = q.shape
    n_past = k_cache.shape[1]
    assert n_past % chunk == 0
    per_b = lambda bi, c: (bi, 0, 0)
    new = pl.BlockSpec((None, t, ATTN_W), per_b)
    cache = pl.BlockSpec((None, chunk, N_HEADS, HEAD_DIM), lambda bi, c: (bi, c, 0, 0))
    rows = N_HEADS * t
    return pl.pallas_call(
        functools.partial(_attn_sample_kernel, n_past=n_past),
        grid=(b, n_past // chunk),
        in_specs=[new, new, new, cache, cache],
        out_specs=pl.BlockSpec((None, t, ATTN_W), per_b),
        out_shape=jax.ShapeDtypeStruct((b, t, ATTN_W), F32),
        scratch_shapes=[pltpu.VMEM((rows, 1), F32), pltpu.VMEM((rows, 1), F32),
                        pltpu.VMEM((rows, ATTN_W), F32)],
        compiler_params=_params(("parallel", "arbitrary")),
        name="attn_sample",
    )(q, k_new, v_new, k_cache, v_cache)


def _mix_kernel(attn_ref, cb_ref, z_ref, zprev_ref, state_ref, x_ref, cw_ref, ga_ref, gc_ref,
                wo_ref, g2_ref, h_ref, xn_ref, xnt_ref, zbuf, *, stride, tiles_per_seq):
    tm = z_ref.shape[0]
    hb = state_ref.shape[0]
    if tiles_per_seq > 1:
        first = (pl.program_id(0) % tiles_per_seq) == 0
        zbuf[0:hb, :] = jnp.where(first, state_ref[...], zprev_ref[...])
    else:
        zbuf[0:hb, :] = state_ref[...]
    z = z_ref[...]
    zbuf[hb:hb + tm, :] = z
    cw = cw_ref[...]
    y = (cw[0:1, :] * zbuf[hb - 2 * stride:hb - 2 * stride + tm, :]
         + cw[1:2, :] * zbuf[hb - stride:hb - stride + tm, :]
         + cw[2:3, :] * z)
    conv_out = cb_ref[...] * y
    a = _rms(attn_ref[...], ga_ref[...]).astype(BF16)
    c = _rms(conv_out, gc_ref[...]).astype(BF16)
    mixed = (jnp.dot(a, wo_ref[0:ATTN_W, :], preferred_element_type=F32)
             + jnp.dot(c, wo_ref[ATTN_W:ATTN_W + CONV_W, :], preferred_element_type=F32))
    h = x_ref[...] + mixed
    h_ref[...] = h
    xn = _rms(h, g2_ref[...])
    xn_ref[...] = xn.astype(BF16)
    xnt_ref[...] = xn.T.astype(BF16)


def _mix(attn, cb, z, state, x, conv_w, ga, gc, wo_bf, g2, tm, stride, tiles_per_seq):
    n, d = x.shape
    hb = state.shape[1]
    row = lambda i: (i, 0)
    const = lambda i: (0, 0)
    prev = lambda i: (jnp.maximum(i * (tm // hb) - 1, 0), 0)
    return pl.pallas_call(
        functools.partial(_mix_kernel, stride=stride, tiles_per_seq=tiles_per_seq),
        grid=(n // tm,),
        in_specs=[pl.BlockSpec((tm, ATTN_W), row),
                  pl.BlockSpec((tm, CONV_W), row),
                  pl.BlockSpec((tm, CONV_W), row),
                  pl.BlockSpec((hb, CONV_W), prev),
                  pl.BlockSpec((None, hb, CONV_W), lambda i: (i // tiles_per_seq, 0, 0)),
                  pl.BlockSpec((tm, d), row),
                  pl.BlockSpec((CONV_K, CONV_W), const),
                  pl.BlockSpec((1, ATTN_W), const),
                  pl.BlockSpec((1, CONV_W), const),
                  pl.BlockSpec((d, d), const),
                  pl.BlockSpec((1, d), const)],
        out_specs=[pl.BlockSpec((tm, d), row), pl.BlockSpec((tm, d), row),
                   pl.BlockSpec((d, tm), lambda i: (0, i))],
        out_shape=[jax.ShapeDtypeStruct((n, d), F32), jax.ShapeDtypeStruct((n, d), BF16),
                   jax.ShapeDtypeStruct((d, n), BF16)],
        scratch_shapes=[pltpu.VMEM((hb + tm, CONV_W), F32)],
        compiler_params=_params(("parallel",)),
        name="mix",
    )(attn, cb, z, z, state, x, conv_w, ga, gc, wo_bf, g2)


def _top16(s, order):
    unit = 2.0 ** 120
    vals = []
    for r in range(PEER_TOPK):
        m = jnp.max(s, axis=0, keepdims=True)
        if order is None:
            sel = s == m
        else:
            big = jnp.int32(2 ** 30)
            sel = order == jnp.min(jnp.where(s == m, order, big), axis=0, keepdims=True)
        s = jnp.where(sel, -(128.0 + r) * unit, s)
        vals.append(m)
    rank = jnp.where(s <= -128.0 * unit, s * (-1.0 / unit) - 128.0, float(PEER_TOPK))
    return rank, jnp.concatenate(vals, axis=0)


def _top16_paired(s):
    unit = 2.0 ** 120
    mark = -128.0 * unit
    n2 = s.shape[0] // 2
    a, b = s[:n2], s[n2:]
    a_first = a >= b
    hi, lo = jnp.maximum(a, b), jnp.minimum(a, b)
    vals = []
    for r in range(PEER_TOPK):
        m = jnp.max(hi, axis=0, keepdims=True)
        sel = hi == m
        hi = jnp.where(sel, lo, hi)
        lo = jnp.where(sel, -(128.0 + r) * unit, lo)
        vals.append(m)
    decode = lambda x: x * (-1.0 / unit) - 128.0
    none = float(PEER_TOPK)
    twice = hi <= mark
    rank_big = jnp.where(twice, decode(hi), jnp.where(lo <= mark, decode(lo), none))
    rank_small = jnp.where(twice, decode(lo), none)
    rank = jnp.concatenate([jnp.where(a_first, rank_big, rank_small),
                            jnp.where(a_first, rank_small, rank_big)], axis=0)
    return rank, jnp.concatenate(vals, axis=0)


_CAND_GROUPS = ((0, 0), (0, 8), (1, 0), (2, 0), (3, 0), (4, 0), (5, 0), (6, 0), (7, 0))


def _route_tile(s1, s2, exact):
    tm = s1.shape[1]
    sub = 8
    if exact:
        iota128 = lax.broadcasted_iota(jnp.int32, s1.shape, 0)
        top16 = lambda s, order: _top16(s, iota128 if order is None else order)
    else:
        top16 = lambda s, order: _top16_paired(s)
    rank1, sv1 = top16(s1, None)
    rank2, sv2 = top16(s2, None)
    groups = [sv1[a:a + 1, :] + sv2[b0:b0 + sub, :] for a, b0 in _CAND_GROUPS]
    groups.append(sv1[sub:2 * sub, :] + sv2[0:1, :])
    cand = jnp.concatenate(groups, axis=0)
    order = None
    if exact:
        i8 = lax.broadcasted_iota(jnp.int32, (sub, tm), 0)
        order = jnp.concatenate([a * PEER_TOPK + b0 + i8 for a, b0 in _CAND_GROUPS]
                                + [(sub + i8) * PEER_TOPK], axis=0)
    crank, _ = top16(cand, order)
    chosen = crank < float(PEER_TOPK)
    cmax = sv1[0:1, :] + sv2[0:1, :]
    zsum = jnp.sum(jnp.where(chosen, jnp.exp(cand - cmax), 0.0), axis=0, keepdims=True)
    picked = jnp.where(chosen, 1.0, 0.0)
    n_of_a = [jnp.sum(picked[0:2 * sub, :], axis=0, keepdims=True)]
    for a in range(1, sub):
        n_of_a.append(jnp.sum(picked[(a + 1) * sub:(a + 2) * sub, :], axis=0, keepdims=True))
    last = (len(_CAND_GROUPS)) * sub
    for a in range(sub, PEER_TOPK):
        n_of_a.append(picked[last + a - sub:last + a - sub + 1, :])
    lb = jnp.zeros(rank1.shape, F32)
    for a in range(PEER_TOPK):
        lb = jnp.where(rank1 == float(a), n_of_a[a], lb)
    a2 = jnp.exp(s2 - sv2[0:1, :])
    a1 = jnp.exp(s1 - sv1[0:1, :]) / zsum
    tied = None
    if not exact:
        def extra(rk):
            n_sel = jnp.sum(jnp.where(rk < float(PEER_TOPK), 1.0, 0.0), axis=0, keepdims=True)
            return jnp.max(n_sel) > float(PEER_TOPK)
        tied = extra(rank1) | extra(rank2) | extra(crank)
    return rank2, a2, lb, a1, tied


def _route_kernel(xn_ref, wq_ref, keys_ref, r2_ref, a2_ref, lb_ref, a1_ref, q_scr):
    h = pl.program_id(1)
    nsub = 2 * PEER_HEADS

    @pl.when(h == 0)
    def _():
        q = jnp.dot(xn_ref[...], wq_ref[...], preferred_element_type=F32)
        for c in range(nsub):
            q_scr[c] = q[:, c * PEER_NKEYS:(c + 1) * PEER_NKEYS].astype(BF16)

    s1 = lax.dot_general(keys_ref[0], q_scr[2 * h], _NT, preferred_element_type=F32)
    s2 = lax.dot_general(keys_ref[1], q_scr[2 * h + 1], _NT, preferred_element_type=F32)

    def twice(x):
        hi = lax.bitcast_convert_type(x.astype(BF16).astype(F32), jnp.uint32)
        return hi | (hi >> 16)

    def emit(rank2, a2, lb, a1):
        r2_ref[...] = _pack_pairs(rank2)
        a2_ref[...] = _pack_pairs(a2)
        lb_ref[...] = twice(lb)
        a1_ref[...] = twice(a1)

    *outs, tied = _route_tile(s1, s2, exact=False)
    emit(*outs)

    @pl.when(tied)
    def _():
        emit(*_route_tile(s1, s2, exact=True)[:4])


def _route(xn, wq_bf, keys_bf, tm):
    n, d = xn.shape
    qd = wq_bf.shape[1]
    out_spec = lambda rows: pl.BlockSpec((None, rows, tm), lambda i, h: (h, 0, i))
    out_sds = lambda rows: jax.ShapeDtypeStruct((PEER_HEADS, rows, n), jnp.uint32)
    half = PEER_NKEYS // 2
    return pl.pallas_call(
        _route_kernel,
        grid=(n // tm, PEER_HEADS),
        in_specs=[pl.BlockSpec((tm, d), lambda i, h: (i, 0)),
                  pl.BlockSpec((d, qd), lambda i, h: (0, 0)),
                  pl.BlockSpec((None, 2, PEER_NKEYS, PEER_NKEYS), lambda i, h: (h, 0, 0, 0))],
        out_specs=[out_spec(half), out_spec(half), out_spec(PEER_NKEYS), out_spec(PEER_NKEYS)],
        out_shape=[out_sds(half), out_sds(half), out_sds(PEER_NKEYS), out_sds(PEER_NKEYS)],
        scratch_shapes=[pltpu.VMEM((2 * PEER_HEADS, tm, PEER_NKEYS), BF16)],
        compiler_params=_params(("parallel", "arbitrary")),
        name="peer_route",
    )(xn, wq_bf, keys_bf)


def _peer_kernel(xn_ref, u_ref, vt_ref, r2_ref, a2_ref, lb_ref, a1_ref, h_ref, gf_ref,
                 y_ref, acc_ref, *, final_norm):
    j = pl.program_id(1)
    te = vt_ref.shape[1]

    @pl.when(j == 0)
    def _():
        acc_ref[...] = jnp.zeros_like(acc_ref)

    tm = xn_ref.shape[1]

    def row_bf16(ref, hh, blk, cols):
        word = jnp.broadcast_to(ref[hh, blk:blk + 1, cols], (8, cols.stop - cols.start))
        tile = pltpu.bitcast(word, BF16)
        return jnp.concatenate([tile] * (PEER_NKEYS // tile.shape[0]), axis=0)

    sub = min(te, SUB_EXPERTS)
    ps = []
    for s0 in range(0, te, sub):
        u_rows = pltpu.bitcast(u_ref[s0 // 2:(s0 + sub) // 2, :], BF16)
        ht = jnp.dot(u_rows, xn_ref[...], preferred_element_type=F32)
        hb = ht.astype(BF16)
        act = (0.5 * hb) * (1.0 + lax.erf(hb * (1.0 / math.sqrt(2.0))))
        blocks = range(s0 // PEER_NKEYS, (s0 + sub) // PEER_NKEYS)
        gate_cols = []
        for c0 in range(0, tm, GATE_COLS):
            cols = slice(c0, min(c0 + GATE_COLS, tm))
            g = [jnp.zeros((PEER_NKEYS, cols.stop - c0), BF16) for _ in blocks]
            for hh in range(PEER_HEADS):
                r2 = pltpu.bitcast(r2_ref[hh, :, cols], BF16)
                a2 = pltpu.bitcast(a2_ref[hh, :, cols], BF16)
                for n, bb in enumerate(blocks):
                    lim = row_bf16(lb_ref, hh, bb, cols)
                    w1 = row_bf16(a1_ref, hh, bb, cols)
                    g[n] = g[n] + jnp.where(r2 < lim, w1 * a2, jnp.zeros_like(r2))
            gate_cols.append(jnp.concatenate(g, axis=0))
        ps.append(act.astype(BF16) * jnp.concatenate(gate_cols, axis=1))
    p = jnp.concatenate(ps, axis=0)
    acc_ref[...] += jnp.dot(vt_ref[...], p, preferred_element_type=F32)

    @pl.when(j == pl.num_programs(1) - 1)
    def _():
        out = h_ref[...] + acc_ref[...].T
        y_ref[...] = _rms(out, gf_ref[...]) if final_norm else out


def _peer(xnt, u_words, vt_bf, r2, a2, lb, a1, h, gf, final_norm, tm, te):
    d, n = xnt.shape
    ne = 2 * u_words.shape[0]
    gate_spec = pl.BlockSpec((PEER_HEADS, PEER_NKEYS // 2, tm), lambda i, j: (0, 0, i))
    nblk = te // PEER_NKEYS
    assert nblk % 8 == 0, "first-key rows are delivered as whole sublane tiles"
    row_spec = pl.BlockSpec((PEER_HEADS, nblk, tm), lambda i, j: (0, j, i))
    return pl.pallas_call(
        functools.partial(_peer_kernel, final_norm=final_norm),
        grid=(n // tm, ne // te),
        in_specs=[pl.BlockSpec((d, tm), lambda i, j: (0, i)),
                  pl.BlockSpec((te // 2, d), lambda i, j: (j, 0)),
                  pl.BlockSpec((d, te), lambda i, j: (0, j)),
                  gate_spec, gate_spec, row_spec, row_spec,
                  pl.BlockSpec((tm, d), lambda i, j: (i, 0)),
                  pl.BlockSpec((1, d), lambda i, j: (0, 0))],
        out_specs=pl.BlockSpec((tm, d), lambda i, j: (i, 0)),
        out_shape=jax.ShapeDtypeStruct((n, d), F32),
        scratch_shapes=[pltpu.VMEM((d, tm), F32)],
        compiler_params=_params(("parallel", "arbitrary")),
        name="peer_experts",
    )(xnt, u_words, vt_bf, r2, a2, lb, a1, h, gf)


def _tile(n, pref):
    t = min(n, pref)
    assert n % t == 0, (n, t)
    return t


def kernel(x_prompt, x_sample, cache_win_k, cache_win_v, state_conv, norm1_g, w_in, conv_w,
           attn_out_g, conv_out_g, w_out, norm2_g, peer_wq, peer_sub_keys, peer_u, peer_v,
           final_norm_g):
    bp, s, d = x_prompt.shape
    bs, t, _ = x_sample.shape
    depth = w_in.shape[0]
    past_len = 8192
    win_p = min(BRANCHES[-1][0], s)
    np_, ns = bp * s, bs * t

    tm_p = _tile(s, 512)
    tq = _tile(s, 512)
    tk = tq
    cos_p, sin_p = _rope_tables(jnp.arange(s, dtype=jnp.int32))
    pos_s = past_len + jnp.repeat(jnp.arange(t, dtype=jnp.int32), bs)
    cos_s, sin_s = _rope_tables(pos_s)

    hp = x_prompt.reshape(np_, d)
    hs = x_sample.transpose(1, 0, 2).reshape(ns, d)
    outs = {k: [] for k in ("kp", "vp", "cp", "ks", "vs", "cs")}
    row = lambda g: g.reshape(1, -1)
    y_p = y_s = None
    for l in range(depth):
        w_in_bf = w_in[l].astype(BF16)
        w_out_bf = w_out[l].astype(BF16)
        wq_bf = peer_wq[l].astype(BF16)
        keys_bf = peer_sub_keys[l].astype(BF16)
        u_words = _pack_pairs(peer_u[l])
        vt_bf = _pair_order(peer_v[l]).T.astype(BF16)
        last = l == depth - 1

        def tail(hres, attn, cb, z, state, stride, tiles_per_seq, tm):
            h_mid, xn2, xn2t = _mix(attn, cb, z, state, hres, conv_w[l], row(attn_out_g[l]),
                                    row(conv_out_g[l]), w_out_bf, row(norm2_g[l]), tm, stride,
                                    tiles_per_seq)
            n = hres.shape[0]
            r2, a2, lb, a1 = _route(xn2, wq_bf, keys_bf, _tile(n, 512))
            return _peer(xn2t, u_words, vt_bf, r2, a2, lb, a1, h_mid, row(final_norm_g), last,
                         _tile(n, 512), 2048)

        q, k, v, cb, z = _inproj(hp, row(norm1_g[l]), w_in_bf, cos_p, sin_p, tm_p)
        attn = _attn_prompt(q.reshape(bp, s, ATTN_W), k.reshape(bp, s, ATTN_W),
                            v.reshape(bp, s, ATTN_W), tq, tk).reshape(np_, ATTN_W)
        outs["kp"].append(k.reshape(bp, s, N_HEADS, HEAD_DIM)[:, s - win_p:])
        outs["vp"].append(v.reshape(bp, s, N_HEADS, HEAD_DIM)[:, s - win_p:])
        outs["cp"].append(z.reshape(bp, s, CONV_W)[:, s - (CONV_K - 1):])
        zero_state = jnp.zeros((bp, 8, CONV_W), F32)
        y_p = tail(hp, attn, cb, z, zero_state, 1, s // tm_p, tm_p)

        q, k, v, cb, z = _inproj(hs, row(norm1_g[l]), w_in_bf, cos_s, sin_s, _tile(ns, 512))
        to_bt = lambda a: a.reshape(t, bs, -1).transpose(1, 0, 2)
        k_bt, v_bt = to_bt(k), to_bt(v)
        attn = _attn_sample(to_bt(q), k_bt, v_bt,
                            cache_win_k[l], cache_win_v[l], 512)
        attn = attn.transpose(1, 0, 2).reshape(ns, ATTN_W)
        outs["ks"].append(k_bt.reshape(bs, t, N_HEADS, HEAD_DIM))
        outs["vs"].append(v_bt.reshape(bs, t, N_HEADS, HEAD_DIM))
        outs["cs"].append(to_bt(z)[:, t - (CONV_K - 1):])
        state = state_conv[l].transpose(1, 0, 2).reshape(1, (CONV_K - 1) * bs, CONV_W)
        y_s = tail(hs, attn, cb, z, state, bs, 1, ns)
        hp, hs = y_p, y_s

    y_prompt = y_p.reshape(bp, s, d)
    y_sample = y_s.reshape(t, bs, d).transpose(1, 0, 2)
    st = lambda name: jnp.stack(outs[name])
    return (y_prompt, y_sample, st("kp"), st("vp"), st("cp"), st("ks"), st("vs"), st("cs"))
```

```python
import functools
import math

import jax
import jax.numpy as jnp
from jax import lax
from jax.experimental import pallas as pl
from jax.experimental.pallas import tpu as pltpu

F32 = jnp.float32
BF16 = jnp.bfloat16

HEAD_DIM = 64
N_HEADS = 12
ATTN_W = N_HEADS * HEAD_DIM
CONV_W = 256
CONV_K = 3
BRANCHES = ((128, 1), (512, 4), (2048, 16))
ROPE_THETA = 10000.0
EPS = 1e-6
PEER_HEADS = 8
PEER_NKEYS = 128
PEER_TOPK = 16
NEG = -1e30
LANES = 128
SUB_EXPERTS = 256
GATE_COLS = 512
VMEM_LIMIT = 56 * 1024 * 1024

_NT = (((1,), (1,)), ((), ()))


def _rms(x, g):
    return x * lax.rsqrt(jnp.mean(x * x, axis=-1, keepdims=True) + EPS) * g


def _pack_pairs(x):
    r, n = x.shape
    bits = lax.bitcast_convert_type(x.astype(BF16).astype(F32), jnp.uint32) >> 16
    bits = bits.reshape(r // 16, 2, 8, n)
    return (bits[:, 0] | (bits[:, 1] << 16)).reshape(r // 2, n)


def _pair_order(x):
    r = x.shape[0]
    return x.reshape(r // 16, 2, 8, -1).transpose(0, 2, 1, 3).reshape(x.shape)


def _params(sem):
    return pltpu.CompilerParams(dimension_semantics=sem, vmem_limit_bytes=VMEM_LIMIT)


def _inproj_kernel(x_ref, g_ref, w_ref, cos_ref, sin_ref,
                   q_ref, vb_ref, kt_ref, vt_ref, cb_ref, z_ref):
    xb = _rms(x_ref[...], g_ref[...]).astype(BF16)
    tm = xb.shape[0]
    lane = lax.broadcasted_iota(jnp.int32, (tm, LANES), 1)
    low_half = (lane % HEAD_DIM) < (HEAD_DIM // 2)
    cos = cos_ref[...]
    sin = sin_ref[...]

    def rope(t):
        partner = jnp.where(low_half, pltpu.roll(t, LANES - HEAD_DIM // 2, 1),
                            pltpu.roll(t, HEAD_DIM // 2, 1))
        return t * cos + partner * sin

    scale = 1.0 / math.sqrt(HEAD_DIM)
    q = jnp.dot(xb, w_ref[:, 0:ATTN_W], preferred_element_type=F32)
    for c in range(ATTN_W // LANES):
        sl = slice(c * LANES, (c + 1) * LANES)
        q_ref[:, sl] = (rope(q[:, sl]) * scale).astype(BF16)
    k = jnp.dot(xb, w_ref[:, ATTN_W:2 * ATTN_W], preferred_element_type=F32)
    for c in range(ATTN_W // LANES):
        sl = slice(c * LANES, (c + 1) * LANES)
        kt_ref[sl, :] = rope(k[:, sl]).T
    v = jnp.dot(xb, w_ref[:, 2 * ATTN_W:3 * ATTN_W], preferred_element_type=F32)
    vt_ref[...] = v.T
    vb_ref[...] = v.astype(BF16)
    c0 = 3 * ATTN_W
    conv = jnp.dot(xb, w_ref[:, c0:c0 + 3 * CONV_W], preferred_element_type=F32)
    cb_ref[...] = conv[:, 0:CONV_W]
    z_ref[...] = conv[:, CONV_W:2 * CONV_W] * conv[:, 2 * CONV_W:3 * CONV_W]


def _inproj(x, g, w_bf, cos, sin, tm, seq):
    n, d = x.shape
    win = w_bf.shape[1]
    period = cos.shape[0] // tm
    per_seq = seq // tm
    row = lambda i: (i, 0)
    const = lambda i: (0, 0)
    feat = pl.BlockSpec((None, ATTN_W, tm), lambda i: (i // per_seq, 0, i % per_seq))
    return pl.pallas_call(
        _inproj_kernel,
        grid=(n // tm,),
        in_specs=[pl.BlockSpec((tm, d), row),
                  pl.BlockSpec((1, d), const),
                  pl.BlockSpec((d, win), const),
                  pl.BlockSpec((tm, LANES), lambda i: (i % period, 0)),
                  pl.BlockSpec((tm, LANES), lambda i: (i % period, 0))],
        out_specs=[pl.BlockSpec((tm, ATTN_W), row), pl.BlockSpec((tm, ATTN_W), row),
                   feat, feat, pl.BlockSpec((tm, CONV_W), row),
                   pl.BlockSpec((tm, CONV_W), row)],
        out_shape=[jax.ShapeDtypeStruct((n, ATTN_W), BF16),
                   jax.ShapeDtypeStruct((n, ATTN_W), BF16),
                   jax.ShapeDtypeStruct((n // seq, ATTN_W, seq), F32),
                   jax.ShapeDtypeStruct((n // seq, ATTN_W, seq), F32),
                   jax.ShapeDtypeStruct((n, CONV_W), F32),
                   jax.ShapeDtypeStruct((n, CONV_W), F32)],
        compiler_params=_params(("parallel",)),
        name="inproj",
    )(x, g, w_bf, cos, sin)


def _rope_tables(pos):
    half = HEAD_DIM // 2
    inv = jnp.exp(-math.log(ROPE_THETA) * jnp.arange(half, dtype=F32) * (2.0 / HEAD_DIM))
    ang = pos.astype(F32)[:, None] * inv[None, :]
    cos = jnp.cos(ang)
    sin = jnp.sin(ang)
    reps = LANES // HEAD_DIM
    cos_t = jnp.tile(jnp.concatenate([cos, cos], axis=1), (1, reps))
    sin_t = jnp.tile(jnp.concatenate([-sin, sin], axis=1), (1, reps))
    return cos_t, sin_t


def _branch_count(dist):
    cnt = jnp.zeros(dist.shape, F32)
    for window, dil in BRANCHES:
        hit = (dist <= window) & ((dist & (dil - 1)) == 0)
        cnt = cnt + jnp.where(hit, 1.0, 0.0)
    return jnp.where(dist >= 0, cnt, 0.0)


def _attn_prompt_kernel(q_ref, k_ref, v_ref, bias_ref, o_ref,
                        kt_ref, vz_ref, m_ref, l_ref, acc_ref, *, tq, tk):
    qi = pl.program_id(2)
    heads = LANES // HEAD_DIM
    ratio = tq // tk
    head_of_lane = lax.broadcasted_iota(jnp.int32, (tq, LANES), 1) // HEAD_DIM
    head_of_key_lane = lax.broadcasted_iota(jnp.int32, (tk, LANES), 1) // HEAD_DIM

    @pl.when(qi == 0)
    def _():
        for jb in range(v_ref.shape[0] // tk):
            rows = slice(jb * tk, (jb + 1) * tk)
            kt_ref[jb] = k_ref[:, rows].astype(BF16)
            vb = v_ref[rows, :]
            for h in range(heads):
                vz_ref[heads * jb + h] = jnp.where(head_of_key_lane == h, vb, jnp.zeros_like(vb))

    q = q_ref[...]
    qz = [jnp.where(head_of_lane == h, q, jnp.zeros_like(q)) for h in range(heads)]
    m_ref[...] = jnp.full(m_ref.shape, NEG, F32)
    l_ref[...] = jnp.zeros(l_ref.shape, F32)
    acc_ref[...] = jnp.zeros(acc_ref.shape, F32)

    def body(j, _):
        bias = bias_ref[qi * ratio + (ratio - 1) - j]
        kt = kt_ref[j]
        alphas = []
        pv = None
        for h in range(heads):
            s = jnp.dot(qz[h], kt, preferred_element_type=F32) + bias
            m_prev = m_ref[h]
            m_next = jnp.maximum(m_prev, jnp.max(s, axis=-1, keepdims=True))
            p = jnp.concatenate(
                [jnp.exp(s[:, c * LANES:(c + 1) * LANES] - m_next) for c in range(tk // LANES)],
                axis=-1)
            alpha = jnp.exp(m_prev - m_next)
            l_ref[h] = alpha * l_ref[h] + jnp.sum(p, axis=-1, keepdims=True)
            m_ref[h] = m_next
            alphas.append(alpha)
            d = jnp.dot(p.astype(BF16), vz_ref[heads * j + h], preferred_element_type=F32)
            pv = d if pv is None else pv + d
        alpha_both = jnp.where(head_of_lane == 0, alphas[0], alphas[1])
        acc_ref[...] = alpha_both * acc_ref[...] + pv
        return 0

    lax.fori_loop(0, (qi + 1) * ratio, body, 0)
    l_both = jnp.where(head_of_lane == 0, l_ref[0], l_ref[1])
    o_ref[...] = acc_ref[...] / l_both


def _attn_bias(s, tq, tk):
    first = jnp.arange(s // tk, dtype=jnp.int32)[:, None, None] - (tq // tk - 1)
    dist = first * tk + jnp.arange(tq, dtype=jnp.int32)[None, :, None] \
        - jnp.arange(tk, dtype=jnp.int32)[None, None, :]
    cnt = _branch_count(dist)
    return jnp.where(cnt > 0.0, jnp.log(jnp.maximum(cnt, 1.0)), NEG)


def _attn_prompt(q, k, v, tq, tk):
    b, s, _ = q.shape
    heads = LANES // HEAD_DIM
    assert heads == 2 and tk % LANES == 0 and tq % tk == 0
    nkb = s // tk
    blk = lambda bi, hp, qi: (bi, qi, hp)
    full = lambda bi, hp, qi: (bi, 0, hp)
    return pl.pallas_call(
        functools.partial(_attn_prompt_kernel, tq=tq, tk=tk),
        grid=(b, ATTN_W // LANES, s // tq),
        in_specs=[pl.BlockSpec((None, tq, LANES), blk),
                  pl.BlockSpec((None, LANES, s), lambda bi, hp, qi: (bi, hp, 0)),
                  pl.BlockSpec((None, s, LANES), full),
                  pl.BlockSpec((nkb, tq, tk), lambda bi, hp, qi: (0, 0, 0))],
        out_specs=pl.BlockSpec((None, tq, LANES), blk),
        out_shape=jax.ShapeDtypeStruct((b, s, ATTN_W), F32),
        scratch_shapes=[pltpu.VMEM((nkb, LANES, tk), BF16),
                        pltpu.VMEM((nkb * heads, tk, LANES), BF16),
                        pltpu.VMEM((heads, tq, LANES), F32),
                        pltpu.VMEM((heads, tq, LANES), F32),
                        pltpu.VMEM((tq, LANES), F32)],
        compiler_params=_params(("parallel", "parallel", "arbitrary")),
        name="attn_prompt",
    )(q, k, v, _attn_bias(s, tq, tk))


def _attn_sample_kernel(q_ref, kn_ref, vn_ref, kc_ref, vc_ref, o_ref, *, chunk):
    t = q_ref.shape[0]
    n_past = kc_ref.shape[0]
    rows = N_HEADS * t
    qt = jnp.concatenate([q_ref[...]] * N_HEADS, axis=0)
    r_id = lax.broadcasted_iota(jnp.int32, (rows, ATTN_W), 0)
    c_id = lax.broadcasted_iota(jnp.int32, (rows, ATTN_W), 1)
    own = (r_id // t) == (c_id // HEAD_DIM)
    qbd = jnp.where(own, qt, jnp.zeros_like(qt))

    def scores(kblk, first_key):
        nk = kblk.shape[0]
        s = lax.dot_general(qbd, kblk.astype(BF16), _NT, preferred_element_type=F32)
        qpos = n_past + lax.broadcasted_iota(jnp.int32, (rows, nk), 0) % t
        kpos = first_key + lax.broadcasted_iota(jnp.int32, (rows, nk), 1)
        cnt = _branch_count(qpos - kpos)
        return jnp.where(cnt > 0.0, s, NEG), cnt

    parts = [scores(kc_ref[c * chunk:(c + 1) * chunk, :], c * chunk)
             for c in range(n_past // chunk)]
    parts.append(scores(kn_ref[...], n_past))
    m = functools.reduce(jnp.maximum, [jnp.max(s, axis=-1, keepdims=True) for s, _ in parts])
    den = jnp.zeros((rows, 1), F32)
    acc = jnp.zeros((rows, ATTN_W), F32)
    for c, (s, cnt) in enumerate(parts):
        p = cnt * jnp.exp(s - m)
        den = den + jnp.sum(p, axis=-1, keepdims=True)
        vblk = vn_ref[...] if c == len(parts) - 1 else vc_ref[c * chunk:(c + 1) * chunk, :]
        acc = acc + jnp.dot(p.astype(BF16), vblk.astype(BF16), preferred_element_type=F32)
    o_full = jnp.where(own, acc / den, 0.0)
    out = o_full[0:t, :]
    for h in range(1, N_HEADS):
        out = out + o_full[h * t:(h + 1) * t, :]
    o_ref[...] = out


def _attn_sample(q, k_new, v_new, k_cache, v_cache):
    b, t, _ = q.shape
    per_b = lambda bi: (bi, 0, 0)
    blk = lambda a: pl.BlockSpec((None,) + a.shape[1:], per_b)
    args = (q, k_new, v_new, k_cache, v_cache)
    return pl.pallas_call(
        functools.partial(_attn_sample_kernel, chunk=512),
        grid=(b,),
        in_specs=[blk(a) for a in args],
        out_specs=pl.BlockSpec((None, t, ATTN_W), per_b),
        out_shape=jax.ShapeDtypeStruct((b, t, ATTN_W), F32),
        compiler_params=_params(("parallel",)),
        name="attn_sample",
    )(*args)


def _mix_kernel(attn_ref, cb_ref, z_ref, zprev_ref, state_ref, x_ref, cw_ref, ga_ref, gc_ref,
                wo_ref, g2_ref, h_ref, xn_ref, xnt_ref, zbuf, *, stride, tiles_per_seq):
    tm = z_ref.shape[0]
    hb = state_ref.shape[0]
    if tiles_per_seq > 1:
        first = (pl.program_id(0) % tiles_per_seq) == 0
        zbuf[0:hb, :] = jnp.where(first, state_ref[...], zprev_ref[...])
    else:
        zbuf[0:hb, :] = state_ref[...]
    z = z_ref[...]
    zbuf[hb:hb + tm, :] = z
    cw = cw_ref[...]
    y = (cw[0:1, :] * zbuf[hb - 2 * stride:hb - 2 * stride + tm, :]
         + cw[1:2, :] * zbuf[hb - stride:hb - stride + tm, :]
         + cw[2:3, :] * z)
    conv_out = cb_ref[...] * y
    a = _rms(attn_ref[...], ga_ref[...]).astype(BF16)
    c = _rms(conv_out, gc_ref[...]).astype(BF16)
    mixed = (jnp.dot(a, wo_ref[0:ATTN_W, :], preferred_element_type=F32)
             + jnp.dot(c, wo_ref[ATTN_W:ATTN_W + CONV_W, :], preferred_element_type=F32))
    h = x_ref[...] + mixed
    h_ref[...] = h
    xn = _rms(h, g2_ref[...])
    xn_ref[...] = xn.astype(BF16)
    xnt_ref[...] = xn.T.astype(BF16)


def _mix(attn, cb, z, state, x, conv_w, ga, gc, wo_bf, g2, tm, stride, tiles_per_seq):
    n, d = x.shape
    hb = state.shape[1]
    row = lambda i: (i, 0)
    const = lambda i: (0, 0)
    prev = lambda i: (jnp.maximum(i * (tm // hb) - 1, 0), 0)
    return pl.pallas_call(
        functools.partial(_mix_kernel, stride=stride, tiles_per_seq=tiles_per_seq),
        grid=(n // tm,),
        in_specs=[pl.BlockSpec((tm, ATTN_W), row),
                  pl.BlockSpec((tm, CONV_W), row),
                  pl.BlockSpec((tm, CONV_W), row),
                  pl.BlockSpec((hb, CONV_W), prev),
                  pl.BlockSpec((None, hb, CONV_W), lambda i: (i // tiles_per_seq, 0, 0)),
                  pl.BlockSpec((tm, d), row),
                  pl.BlockSpec((CONV_K, CONV_W), const),
                  pl.BlockSpec((1, ATTN_W), const),
                  pl.BlockSpec((1, CONV_W), const),
                  pl.BlockSpec((d, d), const),
                  pl.BlockSpec((1, d), const)],
        out_specs=[pl.BlockSpec((tm, d), row), pl.BlockSpec((tm, d), row),
                   pl.BlockSpec((d, tm), lambda i: (0, i))],
        out_shape=[jax.ShapeDtypeStruct((n, d), F32), jax.ShapeDtypeStruct((n, d), BF16),
                   jax.ShapeDtypeStruct((d, n), BF16)],
        scratch_shapes=[pltpu.VMEM((hb + tm, CONV_W), F32)],
        compiler_params=_params(("parallel",)),
        name="mix",
    )(attn, cb, z, z, state, x, conv_w, ga, gc, wo_bf, g2)


def _top16(s, order):
    unit = 2.0 ** 120
    vals = []
    for r in range(PEER_TOPK):
        m = jnp.max(s, axis=0, keepdims=True)
        if order is None:
            sel = s == m
        else:
            big = jnp.int32(2 ** 30)
            sel = order == jnp.min(jnp.where(s == m, order, big), axis=0, keepdims=True)
        s = jnp.where(sel, -(128.0 + r) * unit, s)
        vals.append(m)
    rank = jnp.where(s <= -128.0 * unit, s * (-1.0 / unit) - 128.0, float(PEER_TOPK))
    return rank, jnp.concatenate(vals, axis=0)


def _top16_paired(s):
    unit = 2.0 ** 120
    mark = -128.0 * unit
    n2 = s.shape[0] // 2
    a, b = s[:n2], s[n2:]
    a_first = a >= b
    hi, lo = jnp.maximum(a, b), jnp.minimum(a, b)
    vals = []
    for r in range(PEER_TOPK):
        m = jnp.max(hi, axis=0, keepdims=True)
        sel = hi == m
        hi = jnp.where(sel, lo, hi)
        lo = jnp.where(sel, -(128.0 + r) * unit, lo)
        vals.append(m)
    decode = lambda x: x * (-1.0 / unit) - 128.0
    none = float(PEER_TOPK)
    twice = hi <= mark
    rank_big = jnp.where(twice, decode(hi), jnp.where(lo <= mark, decode(lo), none))
    rank_small = jnp.where(twice, decode(lo), none)
    rank = jnp.concatenate([jnp.where(a_first, rank_big, rank_small),
                            jnp.where(a_first, rank_small, rank_big)], axis=0)
    return rank, jnp.concatenate(vals, axis=0)


_CAND_GROUPS = ((0, 0), (0, 8), (1, 0), (2, 0), (3, 0), (4, 0), (5, 0), (6, 0), (7, 0))


def _route_tile(s1, s2, exact):
    tm = s1.shape[1]
    sub = 8
    if exact:
        iota128 = lax.broadcasted_iota(jnp.int32, s1.shape, 0)
        top16 = lambda s, order: _top16(s, iota128 if order is None else order)
    else:
        top16 = lambda s, order: _top16_paired(s)
    rank1, sv1 = top16(s1, None)
    rank2, sv2 = top16(s2, None)
    groups = [sv1[a:a + 1, :] + sv2[b0:b0 + sub, :] for a, b0 in _CAND_GROUPS]
    groups.append(sv1[sub:2 * sub, :] + sv2[0:1, :])
    cand = jnp.concatenate(groups, axis=0)
    order = None
    if exact:
        i8 = lax.broadcasted_iota(jnp.int32, (sub, tm), 0)
        order = jnp.concatenate([a * PEER_TOPK + b0 + i8 for a, b0 in _CAND_GROUPS]
                                + [(sub + i8) * PEER_TOPK], axis=0)
    crank, _ = top16(cand, order)
    chosen = crank < float(PEER_TOPK)
    cmax = sv1[0:1, :] + sv2[0:1, :]
    zsum = jnp.sum(jnp.where(chosen, jnp.exp(cand - cmax), 0.0), axis=0, keepdims=True)
    picked = jnp.where(chosen, 1.0, 0.0)
    n_of_a = [jnp.sum(picked[0:2 * sub, :], axis=0, keepdims=True)]
    for a in range(1, sub):
        n_of_a.append(jnp.sum(picked[(a + 1) * sub:(a + 2) * sub, :], axis=0, keepdims=True))
    last = (len(_CAND_GROUPS)) * sub
    for a in range(sub, PEER_TOPK):
        n_of_a.append(picked[last + a - sub:last + a - sub + 1, :])
    lb = jnp.zeros(rank1.shape, F32)
    for a in range(PEER_TOPK):
        lb = jnp.where(rank1 == float(a), n_of_a[a], lb)
    a2 = jnp.exp(s2 - sv2[0:1, :])
    a1 = jnp.exp(s1 - sv1[0:1, :]) / zsum
    tied = None
    if not exact:
        def extra(rk):
            n_sel = jnp.sum(jnp.where(rk < float(PEER_TOPK), 1.0, 0.0), axis=0, keepdims=True)
            return jnp.max(n_sel) > float(PEER_TOPK)
        tied = extra(rank1) | extra(rank2) | extra(crank)
    return rank2, a2, lb, a1, tied


def _route_kernel(xn_ref, wq_ref, keys_ref, r2_ref, a2_ref, lb_ref, a1_ref, q_scr):
    h = pl.program_id(1)
    nsub = 2 * PEER_HEADS

    @pl.when(h == 0)
    def _():
        q = jnp.dot(xn_ref[...], wq_ref[...], preferred_element_type=F32)
        for c in range(nsub):
            q_scr[c] = q[:, c * PEER_NKEYS:(c + 1) * PEER_NKEYS].astype(BF16)

    s1 = lax.dot_general(keys_ref[0], q_scr[2 * h], _NT, preferred_element_type=F32)
    s2 = lax.dot_general(keys_ref[1], q_scr[2 * h + 1], _NT, preferred_element_type=F32)

    def twice(x):
        hi = lax.bitcast_convert_type(x.astype(BF16).astype(F32), jnp.uint32)
        return hi | (hi >> 16)

    def emit(rank2, a2, lb, a1):
        r2_ref[...] = _pack_pairs(rank2)
        a2_ref[...] = _pack_pairs(a2)
        lb_ref[...] = twice(lb)
        a1_ref[...] = twice(a1)

    *outs, tied = _route_tile(s1, s2, exact=False)
    emit(*outs)

    @pl.when(tied)
    def _():
        emit(*_route_tile(s1, s2, exact=True)[:4])


def _route(xn, wq_bf, keys_bf, tm):
    n, d = xn.shape
    qd = wq_bf.shape[1]
    out_spec = lambda rows: pl.BlockSpec((None, rows, tm), lambda i, h: (h, 0, i))
    out_sds = lambda rows: jax.ShapeDtypeStruct((PEER_HEADS, rows, n), jnp.uint32)
    half = PEER_NKEYS // 2
    return pl.pallas_call(
        _route_kernel,
        grid=(n // tm, PEER_HEADS),
        in_specs=[pl.BlockSpec((tm, d), lambda i, h: (i, 0)),
                  pl.BlockSpec((d, qd), lambda i, h: (0, 0)),
                  pl.BlockSpec((None, 2, PEER_NKEYS, PEER_NKEYS), lambda i, h: (h, 0, 0, 0))],
        out_specs=[out_spec(half), out_spec(half), out_spec(PEER_NKEYS), out_spec(PEER_NKEYS)],
        out_shape=[out_sds(half), out_sds(half), out_sds(PEER_NKEYS), out_sds(PEER_NKEYS)],
        scratch_shapes=[pltpu.VMEM((2 * PEER_HEADS, tm, PEER_NKEYS), BF16)],
        compiler_params=_params(("parallel", "arbitrary")),
        name="peer_route",
    )(xn, wq_bf, keys_bf)


def _peer_kernel(xn_ref, u_ref, vt_ref, r2_ref, a2_ref, lb_ref, a1_ref, h_ref, gf_ref,
                 y_ref, acc_ref, *, final_norm):
    j = pl.program_id(1)
    te = vt_ref.shape[1]

    @pl.when(j == 0)
    def _():
        acc_ref[...] = jnp.zeros_like(acc_ref)

    tm = xn_ref.shape[1]

    def row_bf16(ref, hh, blk, cols):
        word = jnp.broadcast_to(ref[hh, blk:blk + 1, cols], (8, cols.stop - cols.start))
        tile = pltpu.bitcast(word, BF16)
        return jnp.concatenate([tile] * (PEER_NKEYS // tile.shape[0]), axis=0)

    sub = min(te, SUB_EXPERTS)
    ps = []
    for s0 in range(0, te, sub):
        u_rows = pltpu.bitcast(u_ref[s0 // 2:(s0 + sub) // 2, :], BF16)
        ht = jnp.dot(u_rows, xn_ref[...], preferred_element_type=F32)
        hb = ht.astype(BF16)
        act = (0.5 * hb) * (1.0 + lax.erf(hb * (1.0 / math.sqrt(2.0))))
        blocks = range(s0 // PEER_NKEYS, (s0 + sub) // PEER_NKEYS)
        gate_cols = []
        for c0 in range(0, tm, GATE_COLS):
            cols = slice(c0, min(c0 + GATE_COLS, tm))
            g = [jnp.zeros((PEER_NKEYS, cols.stop - c0), BF16) for _ in blocks]
            for hh in range(PEER_HEADS):
                r2 = pltpu.bitcast(r2_ref[hh, :, cols], BF16)
                a2 = pltpu.bitcast(a2_ref[hh, :, cols], BF16)
                for n, bb in enumerate(blocks):
                    lim = row_bf16(lb_ref, hh, bb, cols)
                    w1 = row_bf16(a1_ref, hh, bb, cols)
                    g[n] = g[n] + jnp.where(r2 < lim, w1 * a2, jnp.zeros_like(r2))
            gate_cols.append(jnp.concatenate(g, axis=0))
        ps.append(act.astype(BF16) * jnp.concatenate(gate_cols, axis=1))
    p = jnp.concatenate(ps, axis=0)
    acc_ref[...] += jnp.dot(vt_ref[...], p, preferred_element_type=F32)

    @pl.when(j == pl.num_programs(1) - 1)
    def _():
        out = h_ref[...] + acc_ref[...].T
        y_ref[...] = _rms(out, gf_ref[...]) if final_norm else out


def _peer(xnt, u_words, vt_bf, r2, a2, lb, a1, h, gf, final_norm, tm, te):
    d, n = xnt.shape
    ne = 2 * u_words.shape[0]
    gate_spec = pl.BlockSpec((PEER_HEADS, PEER_NKEYS // 2, tm), lambda i, j: (0, 0, i))
    nblk = te // PEER_NKEYS
    assert nblk % 8 == 0, "first-key rows are delivered as whole sublane tiles"
    row_spec = pl.BlockSpec((PEER_HEADS, nblk, tm), lambda i, j: (0, j, i))
    return pl.pallas_call(
        functools.partial(_peer_kernel, final_norm=final_norm),
        grid=(n // tm, ne // te),
        in_specs=[pl.BlockSpec((d, tm), lambda i, j: (0, i)),
                  pl.BlockSpec((te // 2, d), lambda i, j: (j, 0)),
                  pl.BlockSpec((d, te), lambda i, j: (0, j)),
                  gate_spec, gate_spec, row_spec, row_spec,
                  pl.BlockSpec((tm, d), lambda i, j: (i, 0)),
                  pl.BlockSpec((1, d), lambda i, j: (0, 0))],
        out_specs=pl.BlockSpec((tm, d), lambda i, j: (i, 0)),
        out_shape=jax.ShapeDtypeStruct((n, d), F32),
        scratch_shapes=[pltpu.VMEM((d, tm), F32)],
        compiler_params=_params(("parallel", "arbitrary")),
        name="peer_experts",
    )(xnt, u_words, vt_bf, r2, a2, lb, a1, h, gf)


def _tile(n, pref):
    t = min(n, pref)
    assert n % t == 0, (n, t)
    return t


def kernel(x_prompt, x_sample, cache_win_k, cache_win_v, state_conv, norm1_g, w_in, conv_w,
           attn_out_g, conv_out_g, w_out, norm2_g, peer_wq, peer_sub_keys, peer_u, peer_v,
           final_norm_g):
    bp, s, d = x_prompt.shape
    bs, t, _ = x_sample.shape
    depth = w_in.shape[0]
    past_len = 8192
    win_p = min(BRANCHES[-1][0], s)
    np_, ns = bp * s, bs * t

    tm_p = _tile(s, 512)
    tq = _tile(s, 512)
    tk = tq
    cos_p, sin_p = _rope_tables(jnp.arange(s, dtype=jnp.int32))
    pos_s = past_len + jnp.repeat(jnp.arange(t, dtype=jnp.int32), bs)
    cos_s, sin_s = _rope_tables(pos_s)

    hp = x_prompt.reshape(np_, d)
    hs = x_sample.transpose(1, 0, 2).reshape(ns, d)
    outs = {k: [] for k in ("kp", "vp", "cp", "ks", "vs", "cs")}
    row = lambda g: g.reshape(1, -1)
    y_p = y_s = None
    for l in range(depth):
        w_in_bf = w_in[l].astype(BF16)
        w_out_bf = w_out[l].astype(BF16)
        wq_bf = peer_wq[l].astype(BF16)
        keys_bf = peer_sub_keys[l].astype(BF16)
        u_words = _pack_pairs(peer_u[l])
        vt_bf = _pair_order(peer_v[l]).T.astype(BF16)
        last = l == depth - 1

        def tail(hres, attn, cb, z, state, stride, tiles_per_seq, tm):
            h_mid, xn2, xn2t = _mix(attn, cb, z, state, hres, conv_w[l], row(attn_out_g[l]),
                                    row(conv_out_g[l]), w_out_bf, row(norm2_g[l]), tm, stride,
                                    tiles_per_seq)
            n = hres.shape[0]
            r2, a2, lb, a1 = _route(xn2, wq_bf, keys_bf, _tile(n, 512))
            return _peer(xn2t, u_words, vt_bf, r2, a2, lb, a1, h_mid, row(final_norm_g), last,
                         _tile(n, 512), 2048)

        q, vb, kt, vt, cb, z = _inproj(hp, row(norm1_g[l]), w_in_bf, cos_p, sin_p, tm_p, s)
        attn = _attn_prompt(q.reshape(bp, s, ATTN_W), kt, vb.reshape(bp, s, ATTN_W),
                            tq, tk).reshape(np_, ATTN_W)
        rows_first = lambda a: a.reshape(bp, N_HEADS, HEAD_DIM, s).transpose(0, 3, 1, 2)
        outs["kp"].append(rows_first(kt)[:, s - win_p:])
        outs["vp"].append(rows_first(vt)[:, s - win_p:])
        outs["cp"].append(z.reshape(bp, s, CONV_W)[:, s - (CONV_K - 1):])
        zero_state = jnp.zeros((bp, 8, CONV_W), F32)
        y_p = tail(hp, attn, cb, z, zero_state, 1, s // tm_p, tm_p)

        q, _, kt, vt, cb, z = _inproj(hs, row(norm1_g[l]), w_in_bf, cos_s, sin_s,
                                      _tile(ns, 512), ns)
        to_bt = lambda a: a.reshape(t, bs, -1).transpose(1, 0, 2)
        k_bt, v_bt = to_bt(kt[0].T), to_bt(vt[0].T)
        attn = _attn_sample(to_bt(q), k_bt, v_bt,
                            cache_win_k[l].reshape(bs, -1, ATTN_W),
                            cache_win_v[l].reshape(bs, -1, ATTN_W))
        attn = attn.transpose(1, 0, 2).reshape(ns, ATTN_W)
        outs["ks"].append(k_bt.reshape(bs, t, N_HEADS, HEAD_DIM))
        outs["vs"].append(v_bt.reshape(bs, t, N_HEADS, HEAD_DIM))
        outs["cs"].append(to_bt(z)[:, t - (CONV_K - 1):])
        state = state_conv[l].transpose(1, 0, 2).reshape(1, (CONV_K - 1) * bs, CONV_W)
        y_s = tail(hs, attn, cb, z, state, bs, 1, ns)
        hp, hs = y_p, y_s

    y_prompt = y_p.reshape(bp, s, d)
    y_sample = y_s.reshape(t, bs, d).transpose(1, 0, 2)
    st = lambda name: jnp.stack(outs[name])
    return (y_prompt, y_sample, st("kp"), st("vp"), st("cp"), st("ks"), st("vs"), st("cs"))
```

```python
import functools
import math

import jax
import jax.numpy as jnp
from jax import lax
from jax.experimental import pallas as pl
from jax.experimental.pallas import tpu as pltpu

F32 = jnp.float32
BF16 = jnp.bfloat16

HEAD_DIM = 64
N_HEADS = 12
ATTN_W = N_HEADS * HEAD_DIM
CONV_W = 256
CONV_K = 3
BRANCHES = ((128, 1), (512, 4), (2048, 16))
ROPE_THETA = 10000.0
EPS = 1e-6
PEER_HEADS = 8
PEER_NKEYS = 128
PEER_TOPK = 16
NEG = -1e30
LANES = 128
SUB_EXPERTS = 256
GATE_COLS = 512
VMEM_LIMIT = 56 * 1024 * 1024

_NT = (((1,), (1,)), ((), ()))


def _rms(x, g):
    return x * lax.rsqrt(jnp.mean(x * x, axis=-1, keepdims=True) + EPS) * g


def _pack_pairs(x):
    r, n = x.shape
    bits = lax.bitcast_convert_type(x.astype(BF16).astype(F32), jnp.uint32) >> 16
    bits = bits.reshape(r // 16, 2, 8, n)
    return (bits[:, 0] | (bits[:, 1] << 16)).reshape(r // 2, n)


def _pair_order(x):
    r = x.shape[0]
    return x.reshape(r // 16, 2, 8, -1).transpose(0, 2, 1, 3).reshape(x.shape)


def _params(sem):
    return pltpu.CompilerParams(dimension_semantics=sem, vmem_limit_bytes=VMEM_LIMIT)


def _inproj_kernel(x_ref, g_ref, w_ref, cos_ref, sin_ref,
                   q_ref, vb_ref, kt_ref, vt_ref, cb_ref, z_ref):
    xb = _rms(x_ref[...], g_ref[...]).astype(BF16)
    tm = xb.shape[0]
    lane = lax.broadcasted_iota(jnp.int32, (tm, LANES), 1)
    low_half = (lane % HEAD_DIM) < (HEAD_DIM // 2)
    cos = cos_ref[...]
    sin = sin_ref[...]

    def rope(t):
        partner = jnp.where(low_half, pltpu.roll(t, LANES - HEAD_DIM // 2, 1),
                            pltpu.roll(t, HEAD_DIM // 2, 1))
        return t * cos + partner * sin

    scale = 1.0 / math.sqrt(HEAD_DIM)
    q = jnp.dot(xb, w_ref[:, 0:ATTN_W], preferred_element_type=F32)
    for c in range(ATTN_W // LANES):
        sl = slice(c * LANES, (c + 1) * LANES)
        q_ref[:, sl] = (rope(q[:, sl]) * scale).astype(BF16)
    k = jnp.dot(xb, w_ref[:, ATTN_W:2 * ATTN_W], preferred_element_type=F32)
    for c in range(ATTN_W // LANES):
        sl = slice(c * LANES, (c + 1) * LANES)
        kt_ref[sl, :] = rope(k[:, sl]).T
    v = jnp.dot(xb, w_ref[:, 2 * ATTN_W:3 * ATTN_W], preferred_element_type=F32)
    vt_ref[...] = v.T
    vb_ref[...] = v.astype(BF16)
    c0 = 3 * ATTN_W
    conv = jnp.dot(xb, w_ref[:, c0:c0 + 3 * CONV_W], preferred_element_type=F32)
    cb_ref[...] = conv[:, 0:CONV_W]
    z_ref[...] = conv[:, CONV_W:2 * CONV_W] * conv[:, 2 * CONV_W:3 * CONV_W]


def _inproj(x, g, w_bf, cos, sin, tm, seq):
    n, d = x.shape
    win = w_bf.shape[1]
    period = cos.shape[0] // tm
    per_seq = seq // tm
    row = lambda i: (i, 0)
    const = lambda i: (0, 0)
    feat = pl.BlockSpec((None, ATTN_W, tm), lambda i: (i // per_seq, 0, i % per_seq))
    return pl.pallas_call(
        _inproj_kernel,
        grid=(n // tm,),
        in_specs=[pl.BlockSpec((tm, d), row),
                  pl.BlockSpec((1, d), const),
                  pl.BlockSpec((d, win), const),
                  pl.BlockSpec((tm, LANES), lambda i: (i % period, 0)),
                  pl.BlockSpec((tm, LANES), lambda i: (i % period, 0))],
        out_specs=[pl.BlockSpec((tm, ATTN_W), row), pl.BlockSpec((tm, ATTN_W), row),
                   feat, feat, pl.BlockSpec((tm, CONV_W), row),
                   pl.BlockSpec((tm, CONV_W), row)],
        out_shape=[jax.ShapeDtypeStruct((n, ATTN_W), BF16),
                   jax.ShapeDtypeStruct((n, ATTN_W), BF16),
                   jax.ShapeDtypeStruct((n // seq, ATTN_W, seq), F32),
                   jax.ShapeDtypeStruct((n // seq, ATTN_W, seq), F32),
                   jax.ShapeDtypeStruct((n, CONV_W), F32),
                   jax.ShapeDtypeStruct((n, CONV_W), F32)],
        compiler_params=_params(("parallel",)),
        name="inproj",
    )(x, g, w_bf, cos, sin)


def _rope_tables(pos):
    half = HEAD_DIM // 2
    inv = jnp.exp(-math.log(ROPE_THETA) * jnp.arange(half, dtype=F32) * (2.0 / HEAD_DIM))
    ang = pos.astype(F32)[:, None] * inv[None, :]
    cos = jnp.cos(ang)
    sin = jnp.sin(ang)
    reps = LANES // HEAD_DIM
    cos_t = jnp.tile(jnp.concatenate([cos, cos], axis=1), (1, reps))
    sin_t = jnp.tile(jnp.concatenate([-sin, sin], axis=1), (1, reps))
    return cos_t, sin_t


def _branch_count(dist):
    cnt = jnp.zeros(dist.shape, F32)
    for window, dil in BRANCHES:
        hit = (dist <= window) & ((dist & (dil - 1)) == 0)
        cnt = cnt + jnp.where(hit, 1.0, 0.0)
    return jnp.where(dist >= 0, cnt, 0.0)


def _attn_prompt_kernel(q_ref, k_ref, v_ref, bias_ref, o_ref,
                        kt_ref, vz_ref, m_ref, l_ref, acc_ref, *, tq, tk):
    qi = pl.program_id(2)
    heads = LANES // HEAD_DIM
    ratio = tq // tk
    head_of_lane = lax.broadcasted_iota(jnp.int32, (tq, LANES), 1) // HEAD_DIM
    head_of_key_lane = lax.broadcasted_iota(jnp.int32, (tk, LANES), 1) // HEAD_DIM

    @pl.when(qi == 0)
    def _():
        for jb in range(v_ref.shape[0] // tk):
            rows = slice(jb * tk, (jb + 1) * tk)
            kt_ref[jb] = k_ref[:, rows].astype(BF16)
            vb = v_ref[rows, :]
            for h in range(heads):
                vz_ref[heads * jb + h] = jnp.where(head_of_key_lane == h, vb, jnp.zeros_like(vb))

    q = q_ref[...]
    qz = [jnp.where(head_of_lane == h, q, jnp.zeros_like(q)) for h in range(heads)]
    m_ref[...] = jnp.full(m_ref.shape, NEG, F32)
    l_ref[...] = jnp.zeros(l_ref.shape, F32)
    acc_ref[...] = jnp.zeros(acc_ref.shape, F32)

    def body(j, _):
        bias = bias_ref[qi * ratio + (ratio - 1) - j]
        kt = kt_ref[j]
        alphas = []
        pv = None
        for h in range(heads):
            s = jnp.dot(qz[h], kt, preferred_element_type=F32) + bias
            m_prev = m_ref[h]
            m_next = jnp.maximum(m_prev, jnp.max(s, axis=-1, keepdims=True))
            p = jnp.concatenate(
                [jnp.exp(s[:, c * LANES:(c + 1) * LANES] - m_next) for c in range(tk // LANES)],
                axis=-1)
            alpha = jnp.exp(m_prev - m_next)
            l_ref[h] = alpha * l_ref[h] + jnp.sum(p, axis=-1, keepdims=True)
            m_ref[h] = m_next
            alphas.append(alpha)
            d = jnp.dot(p.astype(BF16), vz_ref[heads * j + h], preferred_element_type=F32)
            pv = d if pv is None else pv + d
        alpha_both = jnp.where(head_of_lane == 0, alphas[0], alphas[1])
        acc_ref[...] = alpha_both * acc_ref[...] + pv
        return 0

    lax.fori_loop(0, (qi + 1) * ratio, body, 0)
    l_both = jnp.where(head_of_lane == 0, l_ref[0], l_ref[1])
    o_ref[...] = acc_ref[...] / l_both


def _attn_bias(s, tq, tk):
    first = jnp.arange(s // tk, dtype=jnp.int32)[:, None, None] - (tq // tk - 1)
    dist = first * tk + jnp.arange(tq, dtype=jnp.int32)[None, :, None] \
        - jnp.arange(tk, dtype=jnp.int32)[None, None, :]
    cnt = _branch_count(dist)
    return jnp.where(cnt > 0.0, jnp.log(jnp.maximum(cnt, 1.0)), NEG)


def _attn_prompt(q, k, v, tq, tk):
    b, s, _ = q.shape
    heads = LANES // HEAD_DIM
    assert heads == 2 and tk % LANES == 0 and tq % tk == 0
    nkb = s // tk
    blk = lambda bi, hp, qi: (bi, qi, hp)
    full = lambda bi, hp, qi: (bi, 0, hp)
    return pl.pallas_call(
        functools.partial(_attn_prompt_kernel, tq=tq, tk=tk),
        grid=(b, ATTN_W // LANES, s // tq),
        in_specs=[pl.BlockSpec((None, tq, LANES), blk),
                  pl.BlockSpec((None, LANES, s), lambda bi, hp, qi: (bi, hp, 0)),
                  pl.BlockSpec((None, s, LANES), full),
                  pl.BlockSpec((nkb, tq, tk), lambda bi, hp, qi: (0, 0, 0))],
        out_specs=pl.BlockSpec((None, tq, LANES), blk),
        out_shape=jax.ShapeDtypeStruct((b, s, ATTN_W), F32),
        scratch_shapes=[pltpu.VMEM((nkb, LANES, tk), BF16),
                        pltpu.VMEM((nkb * heads, tk, LANES), BF16),
                        pltpu.VMEM((heads, tq, LANES), F32),
                        pltpu.VMEM((heads, tq, LANES), F32),
                        pltpu.VMEM((tq, LANES), F32)],
        compiler_params=_params(("parallel", "parallel", "arbitrary")),
        name="attn_prompt",
    )(q, k, v, _attn_bias(s, tq, tk))


def _attn_sample_kernel(q_ref, kn_ref, vn_ref, kc_ref, vc_ref, o_ref, *, chunk):
    t = q_ref.shape[0]
    n_past = kc_ref.shape[1]
    rows = N_HEADS * t
    qt = jnp.concatenate([q_ref[...]] * N_HEADS, axis=0)
    r_id = lax.broadcasted_iota(jnp.int32, (rows, ATTN_W), 0)
    c_id = lax.broadcasted_iota(jnp.int32, (rows, ATTN_W), 1)
    own = (r_id // t) == (c_id // HEAD_DIM)
    qbd = jnp.where(own, qt, jnp.zeros_like(qt))

    def masked(s, first_key):
        nk = s.shape[1]
        qpos = n_past + lax.broadcasted_iota(jnp.int32, (rows, nk), 0) % t
        kpos = first_key + lax.broadcasted_iota(jnp.int32, (rows, nk), 1)
        cnt = _branch_count(qpos - kpos)
        return jnp.where(cnt > 0.0, s, NEG), cnt

    cols = [slice(c, c + chunk) for c in range(0, n_past, chunk)]
    parts = [masked(jnp.dot(qbd, kc_ref[:, sl].astype(BF16), preferred_element_type=F32), sl.start)
             for sl in cols]
    parts.append(masked(lax.dot_general(qbd, kn_ref[...].astype(BF16), _NT,
                                        preferred_element_type=F32), n_past))
    m = functools.reduce(jnp.maximum, [jnp.max(s, axis=-1, keepdims=True) for s, _ in parts])
    den = jnp.zeros((rows, 1), F32)
    acc = jnp.zeros((rows, ATTN_W), F32)
    for c, (s, cnt) in enumerate(parts):
        p = cnt * jnp.exp(s - m)
        den = den + jnp.sum(p, axis=-1, keepdims=True)
        pb = p.astype(BF16)
        if c < len(cols):
            acc = acc + lax.dot_general(pb, vc_ref[:, cols[c]].astype(BF16), _NT,
                                        preferred_element_type=F32)
        else:
            acc = acc + jnp.dot(pb, vn_ref[...].astype(BF16), preferred_element_type=F32)
    o_full = jnp.where(own, acc / den, 0.0)
    out = o_full[0:t, :]
    for h in range(1, N_HEADS):
        out = out + o_full[h * t:(h + 1) * t, :]
    o_ref[...] = out


def _attn_sample(q, k_new, v_new, k_cache, v_cache):
    b, t, _ = q.shape
    per_b = lambda bi: (bi, 0, 0)
    blk = lambda a: pl.BlockSpec((None,) + a.shape[1:], per_b)
    args = (q, k_new, v_new, k_cache, v_cache)
    return pl.pallas_call(
        functools.partial(_attn_sample_kernel, chunk=512),
        grid=(b,),
        in_specs=[blk(a) for a in args],
        out_specs=pl.BlockSpec((None, t, ATTN_W), per_b),
        out_shape=jax.ShapeDtypeStruct((b, t, ATTN_W), F32),
        compiler_params=_params(("parallel",)),
        name="attn_sample",
    )(*args)


def _mix_kernel(attn_ref, cb_ref, z_ref, zprev_ref, state_ref, x_ref, cw_ref, ga_ref, gc_ref,
                wo_ref, g2_ref, h_ref, xn_ref, xnt_ref, zbuf, *, stride, tiles_per_seq):
    tm = z_ref.shape[0]
    hb = state_ref.shape[0]
    if tiles_per_seq > 1:
        first = (pl.program_id(0) % tiles_per_seq) == 0
        zbuf[0:hb, :] = jnp.where(first, state_ref[...], zprev_ref[...])
    else:
        zbuf[0:hb, :] = state_ref[...]
    z = z_ref[...]
    zbuf[hb:hb + tm, :] = z
    cw = cw_ref[...]
    y = (cw[0:1, :] * zbuf[hb - 2 * stride:hb - 2 * stride + tm, :]
         + cw[1:2, :] * zbuf[hb - stride:hb - stride + tm, :]
         + cw[2:3, :] * z)
    conv_out = cb_ref[...] * y
    a = _rms(attn_ref[...], ga_ref[...]).astype(BF16)
    c = _rms(conv_out, gc_ref[...]).astype(BF16)
    mixed = (jnp.dot(a, wo_ref[0:ATTN_W, :], preferred_element_type=F32)
             + jnp.dot(c, wo_ref[ATTN_W:ATTN_W + CONV_W, :], preferred_element_type=F32))
    h = x_ref[...] + mixed
    h_ref[...] = h
    xn = _rms(h, g2_ref[...])
    xn_ref[...] = xn.astype(BF16)
    xnt_ref[...] = xn.T.astype(BF16)


def _mix(attn, cb, z, state, x, conv_w, ga, gc, wo_bf, g2, tm, stride, tiles_per_seq):
    n, d = x.shape
    hb = state.shape[1]
    row = lambda i: (i, 0)
    const = lambda i: (0, 0)
    prev = lambda i: (jnp.maximum(i * (tm // hb) - 1, 0), 0)
    return pl.pallas_call(
        functools.partial(_mix_kernel, stride=stride, tiles_per_seq=tiles_per_seq),
        grid=(n // tm,),
        in_specs=[pl.BlockSpec((tm, ATTN_W), row),
                  pl.BlockSpec((tm, CONV_W), row),
                  pl.BlockSpec((tm, CONV_W), row),
                  pl.BlockSpec((hb, CONV_W), prev),
                  pl.BlockSpec((None, hb, CONV_W), lambda i: (i // tiles_per_seq, 0, 0)),
                  pl.BlockSpec((tm, d), row),
                  pl.BlockSpec((CONV_K, CONV_W), const),
                  pl.BlockSpec((1, ATTN_W), const),
                  pl.BlockSpec((1, CONV_W), const),
                  pl.BlockSpec((d, d), const),
                  pl.BlockSpec((1, d), const)],
        out_specs=[pl.BlockSpec((tm, d), row), pl.BlockSpec((tm, d), row),
                   pl.BlockSpec((d, tm), lambda i: (0, i))],
        out_shape=[jax.ShapeDtypeStruct((n, d), F32), jax.ShapeDtypeStruct((n, d), BF16),
                   jax.ShapeDtypeStruct((d, n), BF16)],
        scratch_shapes=[pltpu.VMEM((hb + tm, CONV_W), F32)],
        compiler_params=_params(("parallel",)),
        name="mix",
    )(attn, cb, z, z, state, x, conv_w, ga, gc, wo_bf, g2)


def _top16(s, order):
    unit = 2.0 ** 120
    vals = []
    for r in range(PEER_TOPK):
        m = jnp.max(s, axis=0, keepdims=True)
        if order is None:
            sel = s == m
        else:
            big = jnp.int32(2 ** 30)
            sel = order == jnp.min(jnp.where(s == m, order, big), axis=0, keepdims=True)
        s = jnp.where(sel, -(128.0 + r) * unit, s)
        vals.append(m)
    rank = jnp.where(s <= -128.0 * unit, s * (-1.0 / unit) - 128.0, float(PEER_TOPK))
    return rank, jnp.concatenate(vals, axis=0)


def _top16_paired(s):
    unit = 2.0 ** 120
    mark = -128.0 * unit
    n2 = s.shape[0] // 2
    a, b = s[:n2], s[n2:]
    a_first = a >= b
    hi, lo = jnp.maximum(a, b), jnp.minimum(a, b)
    vals = []
    for r in range(PEER_TOPK):
        m = jnp.max(hi, axis=0, keepdims=True)
        sel = hi == m
        hi = jnp.where(sel, lo, hi)
        lo = jnp.where(sel, -(128.0 + r) * unit, lo)
        vals.append(m)
    decode = lambda x: x * (-1.0 / unit) - 128.0
    none = float(PEER_TOPK)
    twice = hi <= mark
    rank_big = jnp.where(twice, decode(hi), jnp.where(lo <= mark, decode(lo), none))
    rank_small = jnp.where(twice, decode(lo), none)
    rank = jnp.concatenate([jnp.where(a_first, rank_big, rank_small),
                            jnp.where(a_first, rank_small, rank_big)], axis=0)
    return rank, jnp.concatenate(vals, axis=0)


_CAND_GROUPS = ((0, 0), (0, 8), (1, 0), (2, 0), (3, 0), (4, 0), (5, 0), (6, 0), (7, 0))


def _route_tile(s1, s2, exact):
    tm = s1.shape[1]
    sub = 8
    if exact:
        iota128 = lax.broadcasted_iota(jnp.int32, s1.shape, 0)
        top16 = lambda s, order: _top16(s, iota128 if order is None else order)
    else:
        top16 = lambda s, order: _top16_paired(s)
    rank1, sv1 = top16(s1, None)
    rank2, sv2 = top16(s2, None)
    groups = [sv1[a:a + 1, :] + sv2[b0:b0 + sub, :] for a, b0 in _CAND_GROUPS]
    groups.append(sv1[sub:2 * sub, :] + sv2[0:1, :])
    cand = jnp.concatenate(groups, axis=0)
    order = None
    if exact:
        i8 = lax.broadcasted_iota(jnp.int32, (sub, tm), 0)
        order = jnp.concatenate([a * PEER_TOPK + b0 + i8 for a, b0 in _CAND_GROUPS]
                                + [(sub + i8) * PEER_TOPK], axis=0)
    crank, _ = top16(cand, order)
    chosen = crank < float(PEER_TOPK)
    cmax = sv1[0:1, :] + sv2[0:1, :]
    zsum = jnp.sum(jnp.where(chosen, jnp.exp(cand - cmax), 0.0), axis=0, keepdims=True)
    picked = jnp.where(chosen, 1.0, 0.0)
    n_of_a = [jnp.sum(picked[0:2 * sub, :], axis=0, keepdims=True)]
    for a in range(1, sub):
        n_of_a.append(jnp.sum(picked[(a + 1) * sub:(a + 2) * sub, :], axis=0, keepdims=True))
    last = (len(_CAND_GROUPS)) * sub
    for a in range(sub, PEER_TOPK):
        n_of_a.append(picked[last + a - sub:last + a - sub + 1, :])
    lb = jnp.zeros(rank1.shape, F32)
    for a in range(PEER_TOPK):
        lb = jnp.where(rank1 == float(a), n_of_a[a], lb)
    a2 = jnp.exp(s2 - sv2[0:1, :])
    a1 = jnp.exp(s1 - sv1[0:1, :]) / zsum
    tied = None
    if not exact:
        def extra(rk):
            n_sel = jnp.sum(jnp.where(rk < float(PEER_TOPK), 1.0, 0.0), axis=0, keepdims=True)
            return jnp.max(n_sel) > float(PEER_TOPK)
        tied = extra(rank1) | extra(rank2) | extra(crank)
    return rank2, a2, lb, a1, tied


def _route_kernel(xn_ref, wq_ref, keys_ref, r2_ref, a2_ref, lb_ref, a1_ref, q_scr):
    h = pl.program_id(1)
    nsub = 2 * PEER_HEADS

    @pl.when(h == 0)
    def _():
        q = jnp.dot(xn_ref[...], wq_ref[...], preferred_element_type=F32)
        for c in range(nsub):
            q_scr[c] = q[:, c * PEER_NKEYS:(c + 1) * PEER_NKEYS].astype(BF16)

    s1 = lax.dot_general(keys_ref[0], q_scr[2 * h], _NT, preferred_element_type=F32)
    s2 = lax.dot_general(keys_ref[1], q_scr[2 * h + 1], _NT, preferred_element_type=F32)

    def twice(x):
        hi = lax.bitcast_convert_type(x.astype(BF16).astype(F32), jnp.uint32)
        return hi | (hi >> 16)

    def emit(rank2, a2, lb, a1):
        r2_ref[...] = _pack_pairs(rank2)
        a2_ref[...] = _pack_pairs(a2)
        lb_ref[...] = twice(lb)
        a1_ref[...] = twice(a1)

    *outs, tied = _route_tile(s1, s2, exact=False)
    emit(*outs)

    @pl.when(tied)
    def _():
        emit(*_route_tile(s1, s2, exact=True)[:4])


def _route(xn, wq_bf, keys_bf, tm):
    n, d = xn.shape
    qd = wq_bf.shape[1]
    out_spec = lambda rows: pl.BlockSpec((None, rows, tm), lambda i, h: (h, 0, i))
    out_sds = lambda rows: jax.ShapeDtypeStruct((PEER_HEADS, rows, n), jnp.uint32)
    half = PEER_NKEYS // 2
    return pl.pallas_call(
        _route_kernel,
        grid=(n // tm, PEER_HEADS),
        in_specs=[pl.BlockSpec((tm, d), lambda i, h: (i, 0)),
                  pl.BlockSpec((d, qd), lambda i, h: (0, 0)),
                  pl.BlockSpec((None, 2, PEER_NKEYS, PEER_NKEYS), lambda i, h: (h, 0, 0, 0))],
        out_specs=[out_spec(half), out_spec(half), out_spec(PEER_NKEYS), out_spec(PEER_NKEYS)],
        out_shape=[out_sds(half), out_sds(half), out_sds(PEER_NKEYS), out_sds(PEER_NKEYS)],
        scratch_shapes=[pltpu.VMEM((2 * PEER_HEADS, tm, PEER_NKEYS), BF16)],
        compiler_params=_params(("parallel", "arbitrary")),
        name="peer_route",
    )(xn, wq_bf, keys_bf)


def _peer_kernel(xn_ref, u_ref, vt_ref, r2_ref, a2_ref, lb_ref, a1_ref, h_ref, gf_ref,
                 y_ref, acc_ref, *, final_norm):
    j = pl.program_id(1)
    te = vt_ref.shape[1]

    @pl.when(j == 0)
    def _():
        acc_ref[...] = jnp.zeros_like(acc_ref)

    tm = xn_ref.shape[1]

    def row_bf16(ref, hh, blk, cols):
        word = jnp.broadcast_to(ref[hh, blk:blk + 1, cols], (8, cols.stop - cols.start))
        tile = pltpu.bitcast(word, BF16)
        return jnp.concatenate([tile] * (PEER_NKEYS // tile.shape[0]), axis=0)

    sub = min(te, SUB_EXPERTS)
    ps = []
    for s0 in range(0, te, sub):
        u_rows = pltpu.bitcast(u_ref[s0 // 2:(s0 + sub) // 2, :], BF16)
        ht = jnp.dot(u_rows, xn_ref[...], preferred_element_type=F32)
        hb = ht.astype(BF16)
        act = (0.5 * hb) * (1.0 + lax.erf(hb * (1.0 / math.sqrt(2.0))))
        blocks = range(s0 // PEER_NKEYS, (s0 + sub) // PEER_NKEYS)
        gate_cols = []
        for c0 in range(0, tm, GATE_COLS):
            cols = slice(c0, min(c0 + GATE_COLS, tm))
            g = [jnp.zeros((PEER_NKEYS, cols.stop - c0), BF16) for _ in blocks]
            for hh in range(PEER_HEADS):
                r2 = pltpu.bitcast(r2_ref[hh, :, cols], BF16)
                a2 = pltpu.bitcast(a2_ref[hh, :, cols], BF16)
                for n, bb in enumerate(blocks):
                    lim = row_bf16(lb_ref, hh, bb, cols)
                    w1 = row_bf16(a1_ref, hh, bb, cols)
                    g[n] = g[n] + jnp.where(r2 < lim, w1 * a2, jnp.zeros_like(r2))
            gate_cols.append(jnp.concatenate(g, axis=0))
        ps.append(act.astype(BF16) * jnp.concatenate(gate_cols, axis=1))
    p = jnp.concatenate(ps, axis=0)
    acc_ref[...] += jnp.dot(vt_ref[...], p, preferred_element_type=F32)

    @pl.when(j == pl.num_programs(1) - 1)
    def _():
        out = h_ref[...] + acc_ref[...].T
        y_ref[...] = _rms(out, gf_ref[...]) if final_norm else out


def _peer(xnt, u_words, vt_bf, r2, a2, lb, a1, h, gf, final_norm, tm, te):
    d, n = xnt.shape
    ne = 2 * u_words.shape[0]
    gate_spec = pl.BlockSpec((PEER_HEADS, PEER_NKEYS // 2, tm), lambda i, j: (0, 0, i))
    nblk = te // PEER_NKEYS
    assert nblk % 8 == 0, "first-key rows are delivered as whole sublane tiles"
    row_spec = pl.BlockSpec((PEER_HEADS, nblk, tm), lambda i, j: (0, j, i))
    return pl.pallas_call(
        functools.partial(_peer_kernel, final_norm=final_norm),
        grid=(n // tm, ne // te),
        in_specs=[pl.BlockSpec((d, tm), lambda i, j: (0, i)),
                  pl.BlockSpec((te // 2, d), lambda i, j: (j, 0)),
                  pl.BlockSpec((d, te), lambda i, j: (0, j)),
                  gate_spec, gate_spec, row_spec, row_spec,
                  pl.BlockSpec((tm, d), lambda i, j: (i, 0)),
                  pl.BlockSpec((1, d), lambda i, j: (0, 0))],
        out_specs=pl.BlockSpec((tm, d), lambda i, j: (i, 0)),
        out_shape=jax.ShapeDtypeStruct((n, d), F32),
        scratch_shapes=[pltpu.VMEM((d, tm), F32)],
        compiler_params=_params(("parallel", "arbitrary")),
        name="peer_experts",
    )(xnt, u_words, vt_bf, r2, a2, lb, a1, h, gf)


def _tile(n, pref):
    t = min(n, pref)
    assert n % t == 0, (n, t)
    return t


def kernel(x_prompt, x_sample, cache_win_k, cache_win_v, state_conv, norm1_g, w_in, conv_w,
           attn_out_g, conv_out_g, w_out, norm2_g, peer_wq, peer_sub_keys, peer_u, peer_v,
           final_norm_g):
    bp, s, d = x_prompt.shape
    bs, t, _ = x_sample.shape
    depth = w_in.shape[0]
    past_len = 8192
    win_p = min(BRANCHES[-1][0], s)
    np_, ns = bp * s, bs * t

    tm_p = _tile(s, 512)
    tq = _tile(s, 512)
    tk = tq
    cos_p, sin_p = _rope_tables(jnp.arange(s, dtype=jnp.int32))
    pos_s = past_len + jnp.repeat(jnp.arange(t, dtype=jnp.int32), bs)
    cos_s, sin_s = _rope_tables(pos_s)

    hp = x_prompt.reshape(np_, d)
    hs = x_sample.transpose(1, 0, 2).reshape(ns, d)
    outs = {k: [] for k in ("kp", "vp", "cp", "ks", "vs", "cs")}
    row = lambda g: g.reshape(1, -1)
    y_p = y_s = None
    for l in range(depth):
        w_in_bf = w_in[l].astype(BF16)
        w_out_bf = w_out[l].astype(BF16)
        wq_bf = peer_wq[l].astype(BF16)
        keys_bf = peer_sub_keys[l].astype(BF16)
        u_words = _pack_pairs(peer_u[l])
        vt_bf = _pair_order(peer_v[l]).T.astype(BF16)
        last = l == depth - 1

        def tail(hres, attn, cb, z, state, stride, tiles_per_seq, tm):
            h_mid, xn2, xn2t = _mix(attn, cb, z, state, hres, conv_w[l], row(attn_out_g[l]),
                                    row(conv_out_g[l]), w_out_bf, row(norm2_g[l]), tm, stride,
                                    tiles_per_seq)
            n = hres.shape[0]
            r2, a2, lb, a1 = _route(xn2, wq_bf, keys_bf, _tile(n, 512))
            return _peer(xn2t, u_words, vt_bf, r2, a2, lb, a1, h_mid, row(final_norm_g), last,
                         _tile(n, 512), 2048)

        q, vb, kt, vt, cb, z = _inproj(hp, row(norm1_g[l]), w_in_bf, cos_p, sin_p, tm_p, s)
        attn = _attn_prompt(q.reshape(bp, s, ATTN_W), kt, vb.reshape(bp, s, ATTN_W),
                            tq, tk).reshape(np_, ATTN_W)
        rows_first = lambda a: a.reshape(bp, N_HEADS, HEAD_DIM, s).transpose(0, 3, 1, 2)
        outs["kp"].append(rows_first(kt)[:, s - win_p:])
        outs["vp"].append(rows_first(vt)[:, s - win_p:])
        outs["cp"].append(z.reshape(bp, s, CONV_W)[:, s - (CONV_K - 1):])
        zero_state = jnp.zeros((bp, 8, CONV_W), F32)
        y_p = tail(hp, attn, cb, z, zero_state, 1, s // tm_p, tm_p)

        q, _, kt, vt, cb, z = _inproj(hs, row(norm1_g[l]), w_in_bf, cos_s, sin_s,
                                      _tile(ns, 512), ns)
        to_bt = lambda a: a.reshape(t, bs, -1).transpose(1, 0, 2)
        k_bt, v_bt = to_bt(kt[0].T), to_bt(vt[0].T)
        feat_major = lambda c: c.transpose(0, 2, 3, 1).reshape(bs, ATTN_W, c.shape[1])
        attn = _attn_sample(to_bt(q), k_bt, v_bt,
                            feat_major(cache_win_k[l]), feat_major(cache_win_v[l]))
        attn = attn.transpose(1, 0, 2).reshape(ns, ATTN_W)
        outs["ks"].append(k_bt.reshape(bs, t, N_HEADS, HEAD_DIM))
        outs["vs"].append(v_bt.reshape(bs, t, N_HEADS, HEAD_DIM))
        outs["cs"].append(to_bt(z)[:, t - (CONV_K - 1):])
        state = state_conv[l].transpose(1, 0, 2).reshape(1, (CONV_K - 1) * bs, CONV_W)
        y_s = tail(hs, attn, cb, z, state, bs, 1, ns)
        hp, hs = y_p, y_s

    y_prompt = y_p.reshape(bp, s, d)
    y_sample = y_s.reshape(t, bs, d).transpose(1, 0, 2)
    st = lambda name: jnp.stack(outs[name])
    return (y_prompt, y_sample, st("kp"), st("vp"), st("cp"), st("ks"), st("vs"), st("cs"))
```

```python
import functools
import math

import jax
import jax.numpy as jnp
from jax import lax
from jax.experimental import pallas as pl
from jax.experimental.pallas import tpu as pltpu

F32 = jnp.float32
BF16 = jnp.bfloat16

HEAD_DIM = 64
N_HEADS = 12
ATTN_W = N_HEADS * HEAD_DIM
CONV_W = 256
CONV_K = 3
BRANCHES = ((128, 1), (512, 4), (2048, 16))
ROPE_THETA = 10000.0
EPS = 1e-6
PEER_HEADS = 8
PEER_NKEYS = 128
PEER_TOPK = 16
NEG = -1e30
LANES = 128
SUB_EXPERTS = 256
VMEM_LIMIT = 56 * 1024 * 1024

_NT = (((1,), (1,)), ((), ()))


def _rms(x, g):
    return x * lax.rsqrt(jnp.mean(x * x, axis=-1, keepdims=True) + EPS) * g


def _pack_pairs(x):
    r, n = x.shape
    bits = lax.bitcast_convert_type(x.astype(BF16).astype(F32), jnp.uint32) >> 16
    bits = bits.reshape(r // 16, 2, 8, n)
    return (bits[:, 0] | (bits[:, 1] << 16)).reshape(r // 2, n)


def _pair_order(x):
    r = x.shape[0]
    return x.reshape(r // 16, 2, 8, -1).transpose(0, 2, 1, 3).reshape(x.shape)


def _params(sem):
    return pltpu.CompilerParams(dimension_semantics=sem, vmem_limit_bytes=VMEM_LIMIT)


def _inproj_kernel(x_ref, g_ref, w_ref, cos_ref, sin_ref,
                   q_ref, vb_ref, kt_ref, vt_ref, cb_ref, z_ref):
    xb = _rms(x_ref[...], g_ref[...]).astype(BF16)
    tm = xb.shape[0]
    lane = lax.broadcasted_iota(jnp.int32, (tm, LANES), 1)
    low_half = (lane % HEAD_DIM) < (HEAD_DIM // 2)
    cos = cos_ref[...]
    sin = sin_ref[...]

    def rope(t):
        partner = jnp.where(low_half, pltpu.roll(t, LANES - HEAD_DIM // 2, 1),
                            pltpu.roll(t, HEAD_DIM // 2, 1))
        return t * cos + partner * sin

    scale = 1.0 / math.sqrt(HEAD_DIM)
    q = jnp.dot(xb, w_ref[:, 0:ATTN_W], preferred_element_type=F32)
    for c in range(ATTN_W // LANES):
        sl = slice(c * LANES, (c + 1) * LANES)
        q_ref[:, sl] = (rope(q[:, sl]) * scale).astype(BF16)
    k = jnp.dot(xb, w_ref[:, ATTN_W:2 * ATTN_W], preferred_element_type=F32)
    for c in range(ATTN_W // LANES):
        sl = slice(c * LANES, (c + 1) * LANES)
        kt_ref[sl, :] = rope(k[:, sl]).T
    v = jnp.dot(xb, w_ref[:, 2 * ATTN_W:3 * ATTN_W], preferred_element_type=F32)
    vt_ref[...] = v.T
    vb_ref[...] = v.astype(BF16)
    c0 = 3 * ATTN_W
    conv = jnp.dot(xb, w_ref[:, c0:c0 + 3 * CONV_W], preferred_element_type=F32)
    cb_ref[...] = conv[:, 0:CONV_W]
    z_ref[...] = conv[:, CONV_W:2 * CONV_W] * conv[:, 2 * CONV_W:3 * CONV_W]


def _inproj(x, g, w_bf, cos, sin, tm, seq):
    n, d = x.shape
    win = w_bf.shape[1]
    period = cos.shape[0] // tm
    per_seq = seq // tm
    row = lambda i: (i, 0)
    const = lambda i: (0, 0)
    feat = pl.BlockSpec((None, ATTN_W, tm), lambda i: (i // per_seq, 0, i % per_seq))
    return pl.pallas_call(
        _inproj_kernel,
        grid=(n // tm,),
        in_specs=[pl.BlockSpec((tm, d), row),
                  pl.BlockSpec((1, d), const),
                  pl.BlockSpec((d, win), const),
                  pl.BlockSpec((tm, LANES), lambda i: (i % period, 0)),
                  pl.BlockSpec((tm, LANES), lambda i: (i % period, 0))],
        out_specs=[pl.BlockSpec((tm, ATTN_W), row), pl.BlockSpec((tm, ATTN_W), row),
                   feat, feat, pl.BlockSpec((tm, CONV_W), row),
                   pl.BlockSpec((tm, CONV_W), row)],
        out_shape=[jax.ShapeDtypeStruct((n, ATTN_W), BF16),
                   jax.ShapeDtypeStruct((n, ATTN_W), BF16),
                   jax.ShapeDtypeStruct((n // seq, ATTN_W, seq), F32),
                   jax.ShapeDtypeStruct((n // seq, ATTN_W, seq), F32),
                   jax.ShapeDtypeStruct((n, CONV_W), F32),
                   jax.ShapeDtypeStruct((n, CONV_W), F32)],
        compiler_params=_params(("parallel",)),
        name="inproj",
    )(x, g, w_bf, cos, sin)


def _rope_tables(pos):
    half = HEAD_DIM // 2
    inv = jnp.exp(-math.log(ROPE_THETA) * jnp.arange(half, dtype=F32) * (2.0 / HEAD_DIM))
    ang = pos.astype(F32)[:, None] * inv[None, :]
    cos = jnp.cos(ang)
    sin = jnp.sin(ang)
    reps = LANES // HEAD_DIM
    cos_t = jnp.tile(jnp.concatenate([cos, cos], axis=1), (1, reps))
    sin_t = jnp.tile(jnp.concatenate([-sin, sin], axis=1), (1, reps))
    return cos_t, sin_t


def _branch_count(dist):
    cnt = jnp.zeros(dist.shape, F32)
    for window, dil in BRANCHES:
        hit = (dist <= window) & ((dist & (dil - 1)) == 0)
        cnt = cnt + jnp.where(hit, 1.0, 0.0)
    return jnp.where(dist >= 0, cnt, 0.0)


def _attn_prompt_kernel(q_ref, k_ref, v_ref, bias_ref, o_ref,
                        kt_ref, vz_ref, m_ref, l_ref, acc_ref, *, tq, tk):
    qi = pl.program_id(2)
    heads = LANES // HEAD_DIM
    ratio = tq // tk
    head_of_lane = lax.broadcasted_iota(jnp.int32, (tq, LANES), 1) // HEAD_DIM
    head_of_key_lane = lax.broadcasted_iota(jnp.int32, (tk, LANES), 1) // HEAD_DIM

    @pl.when(qi == 0)
    def _():
        for jb in range(v_ref.shape[0] // tk):
            rows = slice(jb * tk, (jb + 1) * tk)
            kt_ref[jb] = k_ref[:, rows].astype(BF16)
            vb = v_ref[rows, :]
            for h in range(heads):
                vz_ref[heads * jb + h] = jnp.where(head_of_key_lane == h, vb, jnp.zeros_like(vb))

    q = q_ref[...]
    qz = [jnp.where(head_of_lane == h, q, jnp.zeros_like(q)) for h in range(heads)]
    m_ref[...] = jnp.full(m_ref.shape, NEG, F32)
    l_ref[...] = jnp.zeros(l_ref.shape, F32)
    acc_ref[...] = jnp.zeros(acc_ref.shape, F32)

    def body(j, _):
        bias = bias_ref[qi * ratio + (ratio - 1) - j]
        kt = kt_ref[j]
        alphas = []
        pv = None
        for h in range(heads):
            s = jnp.dot(qz[h], kt, preferred_element_type=F32) + bias
            m_prev = m_ref[h]
            m_next = jnp.maximum(m_prev, jnp.max(s, axis=-1, keepdims=True))
            p = jnp.concatenate(
                [jnp.exp(s[:, c * LANES:(c + 1) * LANES] - m_next) for c in range(tk // LANES)],
                axis=-1)
            alpha = jnp.exp(m_prev - m_next)
            l_ref[h] = alpha * l_ref[h] + jnp.sum(p, axis=-1, keepdims=True)
            m_ref[h] = m_next
            alphas.append(alpha)
            d = jnp.dot(p.astype(BF16), vz_ref[heads * j + h], preferred_element_type=F32)
            pv = d if pv is None else pv + d
        alpha_both = jnp.where(head_of_lane == 0, alphas[0], alphas[1])
        acc_ref[...] = alpha_both * acc_ref[...] + pv
        return 0

    lax.fori_loop(0, (qi + 1) * ratio, body, 0)
    l_both = jnp.where(head_of_lane == 0, l_ref[0], l_ref[1])
    o_ref[...] = acc_ref[...] / l_both


def _attn_bias(s, tq, tk):
    first = jnp.arange(s // tk, dtype=jnp.int32)[:, None, None] - (tq // tk - 1)
    dist = first * tk + jnp.arange(tq, dtype=jnp.int32)[None, :, None] \
        - jnp.arange(tk, dtype=jnp.int32)[None, None, :]
    cnt = _branch_count(dist)
    return jnp.where(cnt > 0.0, jnp.log(jnp.maximum(cnt, 1.0)), NEG)


def _attn_prompt(q, k, v, tq, tk):
    b, s, _ = q.shape
    heads = LANES // HEAD_DIM
    assert heads == 2 and tk % LANES == 0 and tq % tk == 0
    nkb = s // tk
    blk = lambda bi, hp, qi: (bi, qi, hp)
    full = lambda bi, hp, qi: (bi, 0, hp)
    return pl.pallas_call(
        functools.partial(_attn_prompt_kernel, tq=tq, tk=tk),
        grid=(b, ATTN_W // LANES, s // tq),
        in_specs=[pl.BlockSpec((None, tq, LANES), blk),
                  pl.BlockSpec((None, LANES, s), lambda bi, hp, qi: (bi, hp, 0)),
                  pl.BlockSpec((None, s, LANES), full),
                  pl.BlockSpec((nkb, tq, tk), lambda bi, hp, qi: (0, 0, 0))],
        out_specs=pl.BlockSpec((None, tq, LANES), blk),
        out_shape=jax.ShapeDtypeStruct((b, s, ATTN_W), F32),
        scratch_shapes=[pltpu.VMEM((nkb, LANES, tk), BF16),
                        pltpu.VMEM((nkb * heads, tk, LANES), BF16),
                        pltpu.VMEM((heads, tq, LANES), F32),
                        pltpu.VMEM((heads, tq, LANES), F32),
                        pltpu.VMEM((tq, LANES), F32)],
        compiler_params=_params(("parallel", "parallel", "arbitrary")),
        name="attn_prompt",
    )(q, k, v, _attn_bias(s, tq, tk))


def _attn_sample_kernel(q_ref, kn_ref, vn_ref, kc_ref, vc_ref, o_ref, *, chunk):
    t = q_ref.shape[0]
    n_past = kc_ref.shape[1]
    rows = N_HEADS * t
    qt = jnp.concatenate([q_ref[...]] * N_HEADS, axis=0)
    r_id = lax.broadcasted_iota(jnp.int32, (rows, ATTN_W), 0)
    c_id = lax.broadcasted_iota(jnp.int32, (rows, ATTN_W), 1)
    own = (r_id // t) == (c_id // HEAD_DIM)
    qbd = jnp.where(own, qt, jnp.zeros_like(qt))

    def masked(s, first_key):
        nk = s.shape[1]
        qpos = n_past + lax.broadcasted_iota(jnp.int32, (rows, nk), 0) % t
        kpos = first_key + lax.broadcasted_iota(jnp.int32, (rows, nk), 1)
        cnt = _branch_count(qpos - kpos)
        return jnp.where(cnt > 0.0, s, NEG), cnt

    cols = [slice(c, c + chunk) for c in range(0, n_past, chunk)]
    parts = [masked(jnp.dot(qbd, kc_ref[:, sl].astype(BF16), preferred_element_type=F32), sl.start)
             for sl in cols]
    parts.append(masked(lax.dot_general(qbd, kn_ref[...].astype(BF16), _NT,
                                        preferred_element_type=F32), n_past))
    m = functools.reduce(jnp.maximum, [jnp.max(s, axis=-1, keepdims=True) for s, _ in parts])
    den = jnp.zeros((rows, 1), F32)
    acc = jnp.zeros((rows, ATTN_W), F32)
    for c, (s, cnt) in enumerate(parts):
        p = cnt * jnp.exp(s - m)
        den = den + jnp.sum(p, axis=-1, keepdims=True)
        pb = p.astype(BF16)
        if c < len(cols):
            acc = acc + lax.dot_general(pb, vc_ref[:, cols[c]].astype(BF16), _NT,
                                        preferred_element_type=F32)
        else:
            acc = acc + jnp.dot(pb, vn_ref[...].astype(BF16), preferred_element_type=F32)
    o_full = jnp.where(own, acc / den, 0.0)
    out = o_full[0:t, :]
    for h in range(1, N_HEADS):
        out = out + o_full[h * t:(h + 1) * t, :]
    o_ref[...] = out


def _attn_sample(q, k_new, v_new, k_cache, v_cache):
    b, t, _ = q.shape
    per_b = lambda bi: (bi, 0, 0)
    blk = lambda a: pl.BlockSpec((None,) + a.shape[1:], per_b)
    args = (q, k_new, v_new, k_cache, v_cache)
    return pl.pallas_call(
        functools.partial(_attn_sample_kernel, chunk=512),
        grid=(b,),
        in_specs=[blk(a) for a in args],
        out_specs=pl.BlockSpec((None, t, ATTN_W), per_b),
        out_shape=jax.ShapeDtypeStruct((b, t, ATTN_W), F32),
        compiler_params=_params(("parallel",)),
        name="attn_sample",
    )(*args)


def _mix_kernel(attn_ref, cb_ref, z_ref, zprev_ref, state_ref, x_ref, cw_ref, ga_ref, gc_ref,
                wo_ref, g2_ref, h_ref, xn_ref, xnt_ref, zbuf, *, stride, tiles_per_seq):
    tm = z_ref.shape[0]
    hb = state_ref.shape[0]
    if tiles_per_seq > 1:
        first = (pl.program_id(0) % tiles_per_seq) == 0
        zbuf[0:hb, :] = jnp.where(first, state_ref[...], zprev_ref[...])
    else:
        zbuf[0:hb, :] = state_ref[...]
    z = z_ref[...]
    zbuf[hb:hb + tm, :] = z
    cw = cw_ref[...]
    y = (cw[0:1, :] * zbuf[hb - 2 * stride:hb - 2 * stride + tm, :]
         + cw[1:2, :] * zbuf[hb - stride:hb - stride + tm, :]
         + cw[2:3, :] * z)
    conv_out = cb_ref[...] * y
    a = _rms(attn_ref[...], ga_ref[...]).astype(BF16)
    c = _rms(conv_out, gc_ref[...]).astype(BF16)
    mixed = (jnp.dot(a, wo_ref[0:ATTN_W, :], preferred_element_type=F32)
             + jnp.dot(c, wo_ref[ATTN_W:ATTN_W + CONV_W, :], preferred_element_type=F32))
    h = x_ref[...] + mixed
    h_ref[...] = h
    xn = _rms(h, g2_ref[...])
    xn_ref[...] = xn.astype(BF16)
    xnt_ref[...] = xn.T.astype(BF16)


def _mix(attn, cb, z, state, x, conv_w, ga, gc, wo_bf, g2, tm, stride, tiles_per_seq):
    n, d = x.shape
    hb = state.shape[1]
    row = lambda i: (i, 0)
    const = lambda i: (0, 0)
    prev = lambda i: (jnp.maximum(i * (tm // hb) - 1, 0), 0)
    return pl.pallas_call(
        functools.partial(_mix_kernel, stride=stride, tiles_per_seq=tiles_per_seq),
        grid=(n // tm,),
        in_specs=[pl.BlockSpec((tm, ATTN_W), row),
                  pl.BlockSpec((tm, CONV_W), row),
                  pl.BlockSpec((tm, CONV_W), row),
                  pl.BlockSpec((hb, CONV_W), prev),
                  pl.BlockSpec((None, hb, CONV_W), lambda i: (i // tiles_per_seq, 0, 0)),
                  pl.BlockSpec((tm, d), row),
                  pl.BlockSpec((CONV_K, CONV_W), const),
                  pl.BlockSpec((1, ATTN_W), const),
                  pl.BlockSpec((1, CONV_W), const),
                  pl.BlockSpec((d, d), const),
                  pl.BlockSpec((1, d), const)],
        out_specs=[pl.BlockSpec((tm, d), row), pl.BlockSpec((tm, d), row),
                   pl.BlockSpec((d, tm), lambda i: (0, i))],
        out_shape=[jax.ShapeDtypeStruct((n, d), F32), jax.ShapeDtypeStruct((n, d), BF16),
                   jax.ShapeDtypeStruct((d, n), BF16)],
        scratch_shapes=[pltpu.VMEM((hb + tm, CONV_W), F32)],
        compiler_params=_params(("parallel",)),
        name="mix",
    )(attn, cb, z, z, state, x, conv_w, ga, gc, wo_bf, g2)


def _top16(s, order):
    unit = 2.0 ** 120
    big = jnp.int32(2 ** 30)
    vals = []
    for r in range(PEER_TOPK):
        m = jnp.max(s, axis=0, keepdims=True)
        sel = order == jnp.min(jnp.where(s == m, order, big), axis=0, keepdims=True)
        s = jnp.where(sel, -(128.0 + r) * unit, s)
        vals.append(m)
    rank = jnp.where(s <= -128.0 * unit, s * (-1.0 / unit) - 128.0, float(PEER_TOPK))
    return rank, jnp.concatenate(vals, axis=0)


def _top16_paired(s):
    unit = 2.0 ** 120
    mark = -128.0 * unit
    n2 = s.shape[0] // 2
    a, b = s[:n2], s[n2:]
    a_first = a >= b
    hi, lo = jnp.maximum(a, b), jnp.minimum(a, b)
    vals = []
    for r in range(PEER_TOPK):
        m = jnp.max(hi, axis=0, keepdims=True)
        sel = hi == m
        hi = jnp.where(sel, lo, hi)
        lo = jnp.where(sel, -(128.0 + r) * unit, lo)
        vals.append(m)
    decode = lambda x: x * (-1.0 / unit) - 128.0
    none = float(PEER_TOPK)
    twice = hi <= mark
    rank_big = jnp.where(twice, decode(hi), jnp.where(lo <= mark, decode(lo), none))
    rank_small = jnp.where(twice, decode(lo), none)
    rank = jnp.concatenate([jnp.where(a_first, rank_big, rank_small),
                            jnp.where(a_first, rank_small, rank_big)], axis=0)
    return rank, jnp.concatenate(vals, axis=0)


_CAND_GROUPS = ((0, 0), (0, 8), (1, 0), (2, 0), (3, 0), (4, 0), (5, 0), (6, 0), (7, 0))


def _route_tile(s1, s2, exact):
    tm = s1.shape[1]
    sub = 8
    if exact:
        iota128 = lax.broadcasted_iota(jnp.int32, s1.shape, 0)
        top16 = lambda s, order: _top16(s, iota128 if order is None else order)
    else:
        top16 = lambda s, order: _top16_paired(s)
    rank1, sv1 = top16(s1, None)
    rank2, sv2 = top16(s2, None)
    groups = [sv1[a:a + 1, :] + sv2[b0:b0 + sub, :] for a, b0 in _CAND_GROUPS]
    groups.append(sv1[sub:2 * sub, :] + sv2[0:1, :])
    cand = jnp.concatenate(groups, axis=0)
    order = None
    if exact:
        i8 = lax.broadcasted_iota(jnp.int32, (sub, tm), 0)
        order = jnp.concatenate([a * PEER_TOPK + b0 + i8 for a, b0 in _CAND_GROUPS]
                                + [(sub + i8) * PEER_TOPK], axis=0)
    crank, _ = top16(cand, order)
    chosen = crank < float(PEER_TOPK)
    cmax = sv1[0:1, :] + sv2[0:1, :]
    zsum = jnp.sum(jnp.where(chosen, jnp.exp(cand - cmax), 0.0), axis=0, keepdims=True)
    picked = jnp.where(chosen, 1.0, 0.0)
    n_of_a = [jnp.sum(picked[0:2 * sub, :], axis=0, keepdims=True)]
    for a in range(1, sub):
        n_of_a.append(jnp.sum(picked[(a + 1) * sub:(a + 2) * sub, :], axis=0, keepdims=True))
    last = (len(_CAND_GROUPS)) * sub
    for a in range(sub, PEER_TOPK):
        n_of_a.append(picked[last + a - sub:last + a - sub + 1, :])
    lb = jnp.zeros(rank1.shape, F32)
    for a in range(PEER_TOPK):
        lb = jnp.where(rank1 == float(a), n_of_a[a], lb)
    a2 = jnp.exp(s2 - sv2[0:1, :])
    a1 = jnp.exp(s1 - sv1[0:1, :]) / zsum
    tied = None
    if not exact:
        def extra(rk):
            n_sel = jnp.sum(jnp.where(rk < float(PEER_TOPK), 1.0, 0.0), axis=0, keepdims=True)
            return jnp.max(n_sel) > float(PEER_TOPK)
        tied = extra(rank1) | extra(rank2) | extra(crank)
    return rank2, a2, lb, a1, tied


def _route_kernel(xn_ref, wq_ref, keys_ref, r2_ref, a2_ref, lb_ref, a1_ref, q_scr):
    h = pl.program_id(1)
    nsub = 2 * PEER_HEADS

    @pl.when(h == 0)
    def _():
        q = jnp.dot(xn_ref[...], wq_ref[...], preferred_element_type=F32)
        for c in range(nsub):
            q_scr[c] = q[:, c * PEER_NKEYS:(c + 1) * PEER_NKEYS].astype(BF16)

    s1 = lax.dot_general(keys_ref[0], q_scr[2 * h], _NT, preferred_element_type=F32)
    s2 = lax.dot_general(keys_ref[1], q_scr[2 * h + 1], _NT, preferred_element_type=F32)

    def twice(x):
        hi = lax.bitcast_convert_type(x.astype(BF16).astype(F32), jnp.uint32)
        return hi | (hi >> 16)

    def emit(rank2, a2, lb, a1):
        r2_ref[...] = _pack_pairs(rank2)
        a2_ref[...] = _pack_pairs(a2)
        lb_ref[...] = twice(lb)
        a1_ref[...] = twice(a1)

    *outs, tied = _route_tile(s1, s2, exact=False)
    emit(*outs)

    @pl.when(tied)
    def _():
        emit(*_route_tile(s1, s2, exact=True)[:4])


def _route(xn, wq_bf, keys_bf, tm):
    n, d = xn.shape
    qd = wq_bf.shape[1]
    out_spec = lambda rows: pl.BlockSpec((None, rows, tm), lambda i, h: (h, 0, i))
    out_sds = lambda rows: jax.ShapeDtypeStruct((PEER_HEADS, rows, n), jnp.uint32)
    half = PEER_NKEYS // 2
    return pl.pallas_call(
        _route_kernel,
        grid=(n // tm, PEER_HEADS),
        in_specs=[pl.BlockSpec((tm, d), lambda i, h: (i, 0)),
                  pl.BlockSpec((d, qd), lambda i, h: (0, 0)),
                  pl.BlockSpec((None, 2, PEER_NKEYS, PEER_NKEYS), lambda i, h: (h, 0, 0, 0))],
        out_specs=[out_spec(half), out_spec(half), out_spec(PEER_NKEYS), out_spec(PEER_NKEYS)],
        out_shape=[out_sds(half), out_sds(half), out_sds(PEER_NKEYS), out_sds(PEER_NKEYS)],
        scratch_shapes=[pltpu.VMEM((2 * PEER_HEADS, tm, PEER_NKEYS), BF16)],
        compiler_params=_params(("parallel", "arbitrary")),
        name="peer_route",
    )(xn, wq_bf, keys_bf)


def _peer_kernel(xn_ref, u_ref, vt_ref, r2_ref, a2_ref, lb_ref, a1_ref, h_ref, gf_ref,
                 y_ref, acc_ref, *, final_norm):
    j = pl.program_id(1)
    te = vt_ref.shape[1]

    @pl.when(j == 0)
    def _():
        acc_ref[...] = jnp.zeros_like(acc_ref)

    tm = xn_ref.shape[1]

    def row_bf16(ref, hh, blk):
        word = jnp.broadcast_to(ref[hh, blk:blk + 1, :], (8, tm))
        tile = pltpu.bitcast(word, BF16)
        return jnp.concatenate([tile] * (PEER_NKEYS // tile.shape[0]), axis=0)

    sub = min(te, SUB_EXPERTS)
    ps = []
    for s0 in range(0, te, sub):
        u_rows = pltpu.bitcast(u_ref[s0 // 2:(s0 + sub) // 2, :], BF16)
        ht = jnp.dot(u_rows, xn_ref[...], preferred_element_type=F32)
        hb = ht.astype(BF16)
        act = (0.5 * hb) * (1.0 + lax.erf(hb * (1.0 / math.sqrt(2.0))))
        gates = []
        for bb in range(s0 // PEER_NKEYS, (s0 + sub) // PEER_NKEYS):
            g = jnp.zeros((PEER_NKEYS, tm), BF16)
            for hh in range(PEER_HEADS):
                lim = row_bf16(lb_ref, hh, bb)
                w1 = row_bf16(a1_ref, hh, bb)
                r2 = pltpu.bitcast(r2_ref[hh], BF16)
                a2 = pltpu.bitcast(a2_ref[hh], BF16)
                g = g + jnp.where(r2 < lim, w1 * a2, jnp.zeros_like(g))
            gates.append(g)
        ps.append(act.astype(BF16) * jnp.concatenate(gates, axis=0))
    p = jnp.concatenate(ps, axis=0)
    acc_ref[...] += jnp.dot(vt_ref[...], p, preferred_element_type=F32)

    @pl.when(j == pl.num_programs(1) - 1)
    def _():
        out = h_ref[...] + acc_ref[...].T
        y_ref[...] = _rms(out, gf_ref[...]) if final_norm else out


def _peer(xnt, u_words, vt_bf, r2, a2, lb, a1, h, gf, final_norm, tm, te):
    d, n = xnt.shape
    ne = 2 * u_words.shape[0]
    gate_spec = pl.BlockSpec((PEER_HEADS, PEER_NKEYS // 2, tm), lambda i, j: (0, 0, i))
    nblk = te // PEER_NKEYS
    assert nblk % 8 == 0, "first-key rows are delivered as whole sublane tiles"
    row_spec = pl.BlockSpec((PEER_HEADS, nblk, tm), lambda i, j: (0, j, i))
    return pl.pallas_call(
        functools.partial(_peer_kernel, final_norm=final_norm),
        grid=(n // tm, ne // te),
        in_specs=[pl.BlockSpec((d, tm), lambda i, j: (0, i)),
                  pl.BlockSpec((te // 2, d), lambda i, j: (j, 0)),
                  pl.BlockSpec((d, te), lambda i, j: (0, j)),
                  gate_spec, gate_spec, row_spec, row_spec,
                  pl.BlockSpec((tm, d), lambda i, j: (i, 0)),
                  pl.BlockSpec((1, d), lambda i, j: (0, 0))],
        out_specs=pl.BlockSpec((tm, d), lambda i, j: (i, 0)),
        out_shape=jax.ShapeDtypeStruct((n, d), F32),
        scratch_shapes=[pltpu.VMEM((d, tm), F32)],
        compiler_params=_params(("parallel", "arbitrary")),
        name="peer_experts",
    )(xnt, u_words, vt_bf, r2, a2, lb, a1, h, gf)


def _tile(n, pref):
    t = min(n, pref)
    assert n % t == 0, (n, t)
    return t


def kernel(x_prompt, x_sample, cache_win_k, cache_win_v, state_conv, norm1_g, w_in, conv_w,
           attn_out_g, conv_out_g, w_out, norm2_g, peer_wq, peer_sub_keys, peer_u, peer_v,
           final_norm_g):
    bp, s, d = x_prompt.shape
    bs, t, _ = x_sample.shape
    depth = w_in.shape[0]
    past_len = 8192
    win_p = min(BRANCHES[-1][0], s)
    np_, ns = bp * s, bs * t

    tm_p = _tile(s, 512)
    tq = _tile(s, 512)
    tk = tq
    cos_p, sin_p = _rope_tables(jnp.arange(s, dtype=jnp.int32))
    pos_s = past_len + jnp.repeat(jnp.arange(t, dtype=jnp.int32), bs)
    cos_s, sin_s = _rope_tables(pos_s)

    hp = x_prompt.reshape(np_, d)
    hs = x_sample.transpose(1, 0, 2).reshape(ns, d)
    outs = {k: [] for k in ("kp", "vp", "cp", "ks", "vs", "cs")}
    row = lambda g: g.reshape(1, -1)
    y_p = y_s = None
    for l in range(depth):
        w_in_bf = w_in[l].astype(BF16)
        w_out_bf = w_out[l].astype(BF16)
        wq_bf = peer_wq[l].astype(BF16)
        keys_bf = peer_sub_keys[l].astype(BF16)
        u_words = _pack_pairs(peer_u[l])
        vt_bf = _pair_order(peer_v[l]).T.astype(BF16)
        last = l == depth - 1

        def tail(hres, attn, cb, z, state, stride, tiles_per_seq, tm):
            h_mid, xn2, xn2t = _mix(attn, cb, z, state, hres, conv_w[l], row(attn_out_g[l]),
                                    row(conv_out_g[l]), w_out_bf, row(norm2_g[l]), tm, stride,
                                    tiles_per_seq)
            n = hres.shape[0]
            r2, a2, lb, a1 = _route(xn2, wq_bf, keys_bf, _tile(n, 512))
            return _peer(xn2t, u_words, vt_bf, r2, a2, lb, a1, h_mid, row(final_norm_g), last,
                         _tile(n, 512), 2048)

        q, vb, kt, vt, cb, z = _inproj(hp, row(norm1_g[l]), w_in_bf, cos_p, sin_p, tm_p, s)
        attn = _attn_prompt(q.reshape(bp, s, ATTN_W), kt, vb.reshape(bp, s, ATTN_W),
                            tq, tk).reshape(np_, ATTN_W)
        rows_first = lambda a: a.reshape(bp, N_HEADS, HEAD_DIM, s).transpose(0, 3, 1, 2)
        outs["kp"].append(rows_first(kt)[:, s - win_p:])
        outs["vp"].append(rows_first(vt)[:, s - win_p:])
        outs["cp"].append(z.reshape(bp, s, CONV_W)[:, s - (CONV_K - 1):])
        zero_state = jnp.zeros((bp, 8, CONV_W), F32)
        y_p = tail(hp, attn, cb, z, zero_state, 1, s // tm_p, tm_p)

        q, _, kt, vt, cb, z = _inproj(hs, row(norm1_g[l]), w_in_bf, cos_s, sin_s,
                                      _tile(ns, 512), ns)
        to_bt = lambda a: a.reshape(t, bs, -1).transpose(1, 0, 2)
        k_bt, v_bt = to_bt(kt[0].T), to_bt(vt[0].T)
        feat_major = lambda c: c.transpose(0, 2, 3, 1).reshape(bs, ATTN_W, c.shape[1])
        attn = _attn_sample(to_bt(q), k_bt, v_bt,
                            feat_major(cache_win_k[l]), feat_major(cache_win_v[l]))
        attn = attn.transpose(1, 0, 2).reshape(ns, ATTN_W)
        outs["ks"].append(k_bt.reshape(bs, t, N_HEADS, HEAD_DIM))
        outs["vs"].append(v_bt.reshape(bs, t, N_HEADS, HEAD_DIM))
        outs["cs"].append(to_bt(z)[:, t - (CONV_K - 1):])
        state = state_conv[l].transpose(1, 0, 2).reshape(1, (CONV_K - 1) * bs, CONV_W)
        y_s = tail(hs, attn, cb, z, state, bs, 1, ns)
        hp, hs = y_p, y_s

    y_prompt = y_p.reshape(bp, s, d)
    y_sample = y_s.reshape(t, bs, d).transpose(1, 0, 2)
    st = lambda name: jnp.stack(outs[name])
    return (y_prompt, y_sample, st("kp"), st("vp"), st("cp"), st("ks"), st("vs"), st("cs"))
```

```python
import functools
import math

import jax
import jax.numpy as jnp
from jax import lax
from jax.experimental import pallas as pl
from jax.experimental.pallas import tpu as pltpu

F32 = jnp.float32
BF16 = jnp.bfloat16

HEAD_DIM = 64
N_HEADS = 12
ATTN_W = N_HEADS * HEAD_DIM
CONV_W = 256
CONV_K = 3
BRANCHES = ((128, 1), (512, 4), (2048, 16))
ROPE_THETA = 10000.0
EPS = 1e-6
PEER_HEADS = 8
PEER_NKEYS = 128
PEER_TOPK = 16
NEG = -1e30
LANES = 128
SUB_EXPERTS = 128
VMEM_LIMIT = 56 * 1024 * 1024

_NT = (((1,), (1,)), ((), ()))


def _rms(x, g):
    return x * lax.rsqrt(jnp.mean(x * x, axis=-1, keepdims=True) + EPS) * g


def _pack_pairs(x):
    r, n = x.shape
    bits = lax.bitcast_convert_type(x.astype(BF16).astype(F32), jnp.uint32) >> 16
    bits = bits.reshape(r // 16, 2, 8, n)
    return (bits[:, 0] | (bits[:, 1] << 16)).reshape(r // 2, n)


def _pair_order(x):
    r = x.shape[0]
    return x.reshape(r // 16, 2, 8, -1).transpose(0, 2, 1, 3).reshape(x.shape)


def _params(sem):
    return pltpu.CompilerParams(dimension_semantics=sem, vmem_limit_bytes=VMEM_LIMIT)


def _inproj_kernel(x_ref, g_ref, w_ref, cos_ref, sin_ref,
                   q_ref, vb_ref, kt_ref, vt_ref, cb_ref, z_ref):
    xb = _rms(x_ref[...], g_ref[...]).astype(BF16)
    tm = xb.shape[0]
    lane = lax.broadcasted_iota(jnp.int32, (tm, LANES), 1)
    low_half = (lane % HEAD_DIM) < (HEAD_DIM // 2)
    cos = cos_ref[...]
    sin = sin_ref[...]

    def rope(t):
        partner = jnp.where(low_half, pltpu.roll(t, LANES - HEAD_DIM // 2, 1),
                            pltpu.roll(t, HEAD_DIM // 2, 1))
        return t * cos + partner * sin

    scale = 1.0 / math.sqrt(HEAD_DIM)
    q = jnp.dot(xb, w_ref[:, 0:ATTN_W], preferred_element_type=F32)
    for c in range(ATTN_W // LANES):
        sl = slice(c * LANES, (c + 1) * LANES)
        q_ref[:, sl] = (rope(q[:, sl]) * scale).astype(BF16)
    k = jnp.dot(xb, w_ref[:, ATTN_W:2 * ATTN_W], preferred_element_type=F32)
    for c in range(ATTN_W // LANES):
        sl = slice(c * LANES, (c + 1) * LANES)
        kt_ref[sl, :] = rope(k[:, sl]).T
    v = jnp.dot(xb, w_ref[:, 2 * ATTN_W:3 * ATTN_W], preferred_element_type=F32)
    vt_ref[...] = v.T
    vb_ref[...] = v.astype(BF16)
    c0 = 3 * ATTN_W
    conv = jnp.dot(xb, w_ref[:, c0:c0 + 3 * CONV_W], preferred_element_type=F32)
    cb_ref[...] = conv[:, 0:CONV_W]
    z_ref[...] = conv[:, CONV_W:2 * CONV_W] * conv[:, 2 * CONV_W:3 * CONV_W]


def _inproj(x, g, w_bf, cos, sin, tm, seq):
    n, d = x.shape
    win = w_bf.shape[1]
    period = cos.shape[0] // tm
    per_seq = seq // tm
    row = lambda i: (i, 0)
    const = lambda i: (0, 0)
    feat = pl.BlockSpec((None, ATTN_W, tm), lambda i: (i // per_seq, 0, i % per_seq))
    return pl.pallas_call(
        _inproj_kernel,
        grid=(n // tm,),
        in_specs=[pl.BlockSpec((tm, d), row),
                  pl.BlockSpec((1, d), const),
                  pl.BlockSpec((d, win), const),
                  pl.BlockSpec((tm, LANES), lambda i: (i % period, 0)),
                  pl.BlockSpec((tm, LANES), lambda i: (i % period, 0))],
        out_specs=[pl.BlockSpec((tm, ATTN_W), row), pl.BlockSpec((tm, ATTN_W), row),
                   feat, feat, pl.BlockSpec((tm, CONV_W), row),
                   pl.BlockSpec((tm, CONV_W), row)],
        out_shape=[jax.ShapeDtypeStruct((n, ATTN_W), BF16),
                   jax.ShapeDtypeStruct((n, ATTN_W), BF16),
                   jax.ShapeDtypeStruct((n // seq, ATTN_W, seq), F32),
                   jax.ShapeDtypeStruct((n // seq, ATTN_W, seq), F32),
                   jax.ShapeDtypeStruct((n, CONV_W), F32),
                   jax.ShapeDtypeStruct((n, CONV_W), F32)],
        compiler_params=_params(("parallel",)),
        name="inproj",
    )(x, g, w_bf, cos, sin)


def _rope_tables(pos):
    half = HEAD_DIM // 2
    inv = jnp.exp(-math.log(ROPE_THETA) * jnp.arange(half, dtype=F32) * (2.0 / HEAD_DIM))
    ang = pos.astype(F32)[:, None] * inv[None, :]
    cos = jnp.cos(ang)
    sin = jnp.sin(ang)
    reps = LANES // HEAD_DIM
    cos_t = jnp.tile(jnp.concatenate([cos, cos], axis=1), (1, reps))
    sin_t = jnp.tile(jnp.concatenate([-sin, sin], axis=1), (1, reps))
    return cos_t, sin_t


def _branch_count(dist):
    cnt = jnp.zeros(dist.shape, F32)
    for window, dil in BRANCHES:
        hit = (dist <= window) & ((dist & (dil - 1)) == 0)
        cnt = cnt + jnp.where(hit, 1.0, 0.0)
    return jnp.where(dist >= 0, cnt, 0.0)


def _attn_prompt_kernel(q_ref, k_ref, v_ref, bias_ref, o_ref,
                        kt_ref, vz_ref, m_ref, l_ref, acc_ref, *, tq, tk):
    qi = pl.program_id(2)
    heads = LANES // HEAD_DIM
    ratio = tq // tk
    head_of_lane = lax.broadcasted_iota(jnp.int32, (tq, LANES), 1) // HEAD_DIM
    head_of_key_lane = lax.broadcasted_iota(jnp.int32, (tk, LANES), 1) // HEAD_DIM

    @pl.when(qi == 0)
    def _():
        for jb in range(v_ref.shape[0] // tk):
            rows = slice(jb * tk, (jb + 1) * tk)
            kt_ref[jb] = k_ref[:, rows].astype(BF16)
            vb = v_ref[rows, :]
            for h in range(heads):
                vz_ref[heads * jb + h] = jnp.where(head_of_key_lane == h, vb, jnp.zeros_like(vb))

    q = q_ref[...]
    qz = [jnp.where(head_of_lane == h, q, jnp.zeros_like(q)) for h in range(heads)]
    m_ref[...] = jnp.full(m_ref.shape, NEG, F32)
    l_ref[...] = jnp.zeros(l_ref.shape, F32)
    acc_ref[...] = jnp.zeros(acc_ref.shape, F32)

    def body(j, _):
        bias = bias_ref[qi * ratio + (ratio - 1) - j]
        kt = kt_ref[j]
        alphas = []
        pv = None
        for h in range(heads):
            s = jnp.dot(qz[h], kt, preferred_element_type=F32) + bias
            m_prev = m_ref[h]
            m_next = jnp.maximum(m_prev, jnp.max(s, axis=-1, keepdims=True))
            p = jnp.concatenate(
                [jnp.exp(s[:, c * LANES:(c + 1) * LANES] - m_next) for c in range(tk // LANES)],
                axis=-1)
            alpha = jnp.exp(m_prev - m_next)
            l_ref[h] = alpha * l_ref[h] + jnp.sum(p, axis=-1, keepdims=True)
            m_ref[h] = m_next
            alphas.append(alpha)
            d = jnp.dot(p.astype(BF16), vz_ref[heads * j + h], preferred_element_type=F32)
            pv = d if pv is None else pv + d
        alpha_both = jnp.where(head_of_lane == 0, alphas[0], alphas[1])
        acc_ref[...] = alpha_both * acc_ref[...] + pv
        return 0

    lax.fori_loop(0, (qi + 1) * ratio, body, 0)
    l_both = jnp.where(head_of_lane == 0, l_ref[0], l_ref[1])
    o_ref[...] = acc_ref[...] / l_both


def _attn_bias(s, tq, tk):
    first = jnp.arange(s // tk, dtype=jnp.int32)[:, None, None] - (tq // tk - 1)
    dist = first * tk + jnp.arange(tq, dtype=jnp.int32)[None, :, None] \
        - jnp.arange(tk, dtype=jnp.int32)[None, None, :]
    cnt = _branch_count(dist)
    return jnp.where(cnt > 0.0, jnp.log(jnp.maximum(cnt, 1.0)), NEG)


def _attn_prompt(q, k, v, tq, tk):
    b, s, _ = q.shape
    heads = LANES // HEAD_DIM
    assert heads == 2 and tk % LANES == 0 and tq % tk == 0
    nkb = s // tk
    blk = lambda bi, hp, qi: (bi, qi, hp)
    full = lambda bi, hp, qi: (bi, 0, hp)
    return pl.pallas_call(
        functools.partial(_attn_prompt_kernel, tq=tq, tk=tk),
        grid=(b, ATTN_W // LANES, s // tq),
        in_specs=[pl.BlockSpec((None, tq, LANES), blk),
                  pl.BlockSpec((None, LANES, s), lambda bi, hp, qi: (bi, hp, 0)),
                  pl.BlockSpec((None, s, LANES), full),
                  pl.BlockSpec((nkb, tq, tk), lambda bi, hp, qi: (0, 0, 0))],
        out_specs=pl.BlockSpec((None, tq, LANES), blk),
        out_shape=jax.ShapeDtypeStruct((b, s, ATTN_W), F32),
        scratch_shapes=[pltpu.VMEM((nkb, LANES, tk), BF16),
                        pltpu.VMEM((nkb * heads, tk, LANES), BF16),
                        pltpu.VMEM((heads, tq, LANES), F32),
                        pltpu.VMEM((heads, tq, LANES), F32),
                        pltpu.VMEM((tq, LANES), F32)],
        compiler_params=_params(("parallel", "parallel", "arbitrary")),
        name="attn_prompt",
    )(q, k, v, _attn_bias(s, tq, tk))


def _attn_sample_kernel(q_ref, kn_ref, vn_ref, kc_ref, vc_ref, o_ref, *, chunk):
    t = q_ref.shape[0]
    n_past = kc_ref.shape[1]
    rows = N_HEADS * t
    qt = jnp.concatenate([q_ref[...]] * N_HEADS, axis=0)
    r_id = lax.broadcasted_iota(jnp.int32, (rows, ATTN_W), 0)
    c_id = lax.broadcasted_iota(jnp.int32, (rows, ATTN_W), 1)
    own = (r_id // t) == (c_id // HEAD_DIM)
    qbd = jnp.where(own, qt, jnp.zeros_like(qt))

    def masked(s, first_key):
        nk = s.shape[1]
        qpos = n_past + lax.broadcasted_iota(jnp.int32, (rows, nk), 0) % t
        kpos = first_key + lax.broadcasted_iota(jnp.int32, (rows, nk), 1)
        cnt = _branch_count(qpos - kpos)
        return jnp.where(cnt > 0.0, s, NEG), cnt

    cols = [slice(c, c + chunk) for c in range(0, n_past, chunk)]
    parts = [masked(jnp.dot(qbd, kc_ref[:, sl].astype(BF16), preferred_element_type=F32), sl.start)
             for sl in cols]
    parts.append(masked(lax.dot_general(qbd, kn_ref[...].astype(BF16), _NT,
                                        preferred_element_type=F32), n_past))
    m = functools.reduce(jnp.maximum, [jnp.max(s, axis=-1, keepdims=True) for s, _ in parts])
    den = jnp.zeros((rows, 1), F32)
    acc = jnp.zeros((rows, ATTN_W), F32)
    for c, (s, cnt) in enumerate(parts):
        p = cnt * jnp.exp(s - m)
        den = den + jnp.sum(p, axis=-1, keepdims=True)
        pb = p.astype(BF16)
        if c < len(cols):
            acc = acc + lax.dot_general(pb, vc_ref[:, cols[c]].astype(BF16), _NT,
                                        preferred_element_type=F32)
        else:
            acc = acc + jnp.dot(pb, vn_ref[...].astype(BF16), preferred_element_type=F32)
    o_full = jnp.where(own, acc / den, 0.0)
    out = o_full[0:t, :]
    for h in range(1, N_HEADS):
        out = out + o_full[h * t:(h + 1) * t, :]
    o_ref[...] = out


def _attn_sample(q, k_new, v_new, k_cache, v_cache):
    b, t, _ = q.shape
    per_b = lambda bi: (bi, 0, 0)
    blk = lambda a: pl.BlockSpec((None,) + a.shape[1:], per_b)
    args = (q, k_new, v_new, k_cache, v_cache)
    return pl.pallas_call(
        functools.partial(_attn_sample_kernel, chunk=512),
        grid=(b,),
        in_specs=[blk(a) for a in args],
        out_specs=pl.BlockSpec((None, t, ATTN_W), per_b),
        out_shape=jax.ShapeDtypeStruct((b, t, ATTN_W), F32),
        compiler_params=_params(("parallel",)),
        name="attn_sample",
    )(*args)


def _mix_kernel(attn_ref, cb_ref, z_ref, zprev_ref, state_ref, x_ref, cw_ref, ga_ref, gc_ref,
                wo_ref, g2_ref, h_ref, xn_ref, xnt_ref, zbuf, *, stride, tiles_per_seq):
    tm = z_ref.shape[0]
    hb = state_ref.shape[0]
    if tiles_per_seq > 1:
        first = (pl.program_id(0) % tiles_per_seq) == 0
        zbuf[0:hb, :] = jnp.where(first, state_ref[...], zprev_ref[...])
    else:
        zbuf[0:hb, :] = state_ref[...]
    z = z_ref[...]
    zbuf[hb:hb + tm, :] = z
    cw = cw_ref[...]
    y = (cw[0:1, :] * zbuf[hb - 2 * stride:hb - 2 * stride + tm, :]
         + cw[1:2, :] * zbuf[hb - stride:hb - stride + tm, :]
         + cw[2:3, :] * z)
    conv_out = cb_ref[...] * y
    a = _rms(attn_ref[...], ga_ref[...]).astype(BF16)
    c = _rms(conv_out, gc_ref[...]).astype(BF16)
    mixed = (jnp.dot(a, wo_ref[0:ATTN_W, :], preferred_element_type=F32)
             + jnp.dot(c, wo_ref[ATTN_W:ATTN_W + CONV_W, :], preferred_element_type=F32))
    h = x_ref[...] + mixed
    h_ref[...] = h
    xn = _rms(h, g2_ref[...])
    xn_ref[...] = xn.astype(BF16)
    xnt_ref[...] = xn.T.astype(BF16)


def _mix(attn, cb, z, state, x, conv_w, ga, gc, wo_bf, g2, tm, stride, tiles_per_seq):
    n, d = x.shape
    hb = state.shape[1]
    row = lambda i: (i, 0)
    const = lambda i: (0, 0)
    prev = lambda i: (jnp.maximum(i * (tm // hb) - 1, 0), 0)
    return pl.pallas_call(
        functools.partial(_mix_kernel, stride=stride, tiles_per_seq=tiles_per_seq),
        grid=(n // tm,),
        in_specs=[pl.BlockSpec((tm, ATTN_W), row),
                  pl.BlockSpec((tm, CONV_W), row),
                  pl.BlockSpec((tm, CONV_W), row),
                  pl.BlockSpec((hb, CONV_W), prev),
                  pl.BlockSpec((None, hb, CONV_W), lambda i: (i // tiles_per_seq, 0, 0)),
                  pl.BlockSpec((tm, d), row),
                  pl.BlockSpec((CONV_K, CONV_W), const),
                  pl.BlockSpec((1, ATTN_W), const),
                  pl.BlockSpec((1, CONV_W), const),
                  pl.BlockSpec((d, d), const),
                  pl.BlockSpec((1, d), const)],
        out_specs=[pl.BlockSpec((tm, d), row), pl.BlockSpec((tm, d), row),
                   pl.BlockSpec((d, tm), lambda i: (0, i))],
        out_shape=[jax.ShapeDtypeStruct((n, d), F32), jax.ShapeDtypeStruct((n, d), BF16),
                   jax.ShapeDtypeStruct((d, n), BF16)],
        scratch_shapes=[pltpu.VMEM((hb + tm, CONV_W), F32)],
        compiler_params=_params(("parallel",)),
        name="mix",
    )(attn, cb, z, z, state, x, conv_w, ga, gc, wo_bf, g2)


def _top16(s, order):
    unit = 2.0 ** 120
    big = jnp.int32(2 ** 30)
    vals = []
    for r in range(PEER_TOPK):
        m = jnp.max(s, axis=0, keepdims=True)
        sel = order == jnp.min(jnp.where(s == m, order, big), axis=0, keepdims=True)
        s = jnp.where(sel, -(128.0 + r) * unit, s)
        vals.append(m)
    rank = jnp.where(s <= -128.0 * unit, s * (-1.0 / unit) - 128.0, float(PEER_TOPK))
    return rank, jnp.concatenate(vals, axis=0)


def _top16_paired(s):
    unit = 2.0 ** 120
    mark = -128.0 * unit
    n2 = s.shape[0] // 2
    a, b = s[:n2], s[n2:]
    a_first = a >= b
    hi, lo = jnp.maximum(a, b), jnp.minimum(a, b)
    vals = []
    for r in range(PEER_TOPK):
        m = jnp.max(hi, axis=0, keepdims=True)
        sel = hi == m
        hi = jnp.where(sel, lo, hi)
        lo = jnp.where(sel, -(128.0 + r) * unit, lo)
        vals.append(m)
    decode = lambda x: x * (-1.0 / unit) - 128.0
    none = float(PEER_TOPK)
    twice = hi <= mark
    rank_big = jnp.where(twice, decode(hi), jnp.where(lo <= mark, decode(lo), none))
    rank_small = jnp.where(twice, decode(lo), none)
    rank = jnp.concatenate([jnp.where(a_first, rank_big, rank_small),
                            jnp.where(a_first, rank_small, rank_big)], axis=0)
    return rank, jnp.concatenate(vals, axis=0)


_CAND_GROUPS = ((0, 0), (0, 8), (1, 0), (2, 0), (3, 0), (4, 0), (5, 0), (6, 0), (7, 0))


def _route_tile(s1, s2, exact):
    tm = s1.shape[1]
    sub = 8
    if exact:
        iota128 = lax.broadcasted_iota(jnp.int32, s1.shape, 0)
        top16 = lambda s, order: _top16(s, iota128 if order is None else order)
    else:
        top16 = lambda s, order: _top16_paired(s)
    rank1, sv1 = top16(s1, None)
    rank2, sv2 = top16(s2, None)
    groups = [sv1[a:a + 1, :] + sv2[b0:b0 + sub, :] for a, b0 in _CAND_GROUPS]
    groups.append(sv1[sub:2 * sub, :] + sv2[0:1, :])
    cand = jnp.concatenate(groups, axis=0)
    order = None
    if exact:
        i8 = lax.broadcasted_iota(jnp.int32, (sub, tm), 0)
        order = jnp.concatenate([a * PEER_TOPK + b0 + i8 for a, b0 in _CAND_GROUPS]
                                + [(sub + i8) * PEER_TOPK], axis=0)
    crank, _ = top16(cand, order)
    chosen = crank < float(PEER_TOPK)
    cmax = sv1[0:1, :] + sv2[0:1, :]
    zsum = jnp.sum(jnp.where(chosen, jnp.exp(cand - cmax), 0.0), axis=0, keepdims=True)
    picked = jnp.where(chosen, 1.0, 0.0)
    n_of_a = [jnp.sum(picked[0:2 * sub, :], axis=0, keepdims=True)]
    for a in range(1, sub):
        n_of_a.append(jnp.sum(picked[(a + 1) * sub:(a + 2) * sub, :], axis=0, keepdims=True))
    last = (len(_CAND_GROUPS)) * sub
    for a in range(sub, PEER_TOPK):
        n_of_a.append(picked[last + a - sub:last + a - sub + 1, :])
    lb = jnp.zeros(rank1.shape, F32)
    for a in range(PEER_TOPK):
        lb = jnp.where(rank1 == float(a), n_of_a[a], lb)
    a2 = jnp.exp(s2 - sv2[0:1, :])
    a1 = jnp.exp(s1 - sv1[0:1, :]) / zsum
    tied = None
    if not exact:
        def extra(rk):
            n_sel = jnp.sum(jnp.where(rk < float(PEER_TOPK), 1.0, 0.0), axis=0, keepdims=True)
            return jnp.max(n_sel) > float(PEER_TOPK)
        tied = extra(rank1) | extra(rank2) | extra(crank)
    return rank2, a2, lb, a1, tied


def _route_kernel(xn_ref, wq_ref, keys_ref, r2_ref, a2_ref, lb_ref, a1_ref, q_scr):
    h = pl.program_id(1)
    nsub = 2 * PEER_HEADS

    @pl.when(h == 0)
    def _():
        q = jnp.dot(xn_ref[...], wq_ref[...], preferred_element_type=F32)
        for c in range(nsub):
            q_scr[c] = q[:, c * PEER_NKEYS:(c + 1) * PEER_NKEYS].astype(BF16)

    s1 = lax.dot_general(keys_ref[0], q_scr[2 * h], _NT, preferred_element_type=F32)
    s2 = lax.dot_general(keys_ref[1], q_scr[2 * h + 1], _NT, preferred_element_type=F32)

    def twice(x):
        hi = lax.bitcast_convert_type(x.astype(BF16).astype(F32), jnp.uint32)
        return hi | (hi >> 16)

    def emit(rank2, a2, lb, a1):
        r2_ref[...] = _pack_pairs(rank2)
        a2_ref[...] = _pack_pairs(a2)
        lb_ref[...] = twice(lb)
        a1_ref[...] = twice(a1)

    *outs, tied = _route_tile(s1, s2, exact=False)
    emit(*outs)

    @pl.when(tied)
    def _():
        emit(*_route_tile(s1, s2, exact=True)[:4])


def _route(xn, wq_bf, keys_bf, tm):
    n, d = xn.shape
    qd = wq_bf.shape[1]
    out_spec = lambda rows: pl.BlockSpec((None, rows, tm), lambda i, h: (h, 0, i))
    out_sds = lambda rows: jax.ShapeDtypeStruct((PEER_HEADS, rows, n), jnp.uint32)
    half = PEER_NKEYS // 2
    return pl.pallas_call(
        _route_kernel,
        grid=(n // tm, PEER_HEADS),
        in_specs=[pl.BlockSpec((tm, d), lambda i, h: (i, 0)),
                  pl.BlockSpec((d, qd), lambda i, h: (0, 0)),
                  pl.BlockSpec((None, 2, PEER_NKEYS, PEER_NKEYS), lambda i, h: (h, 0, 0, 0))],
        out_specs=[out_spec(half), out_spec(half), out_spec(PEER_NKEYS), out_spec(PEER_NKEYS)],
        out_shape=[out_sds(half), out_sds(half), out_sds(PEER_NKEYS), out_sds(PEER_NKEYS)],
        scratch_shapes=[pltpu.VMEM((2 * PEER_HEADS, tm, PEER_NKEYS), BF16)],
        compiler_params=_params(("parallel", "arbitrary")),
        name="peer_route",
    )(xn, wq_bf, keys_bf)


def _peer_kernel(xn_ref, u_ref, vt_ref, r2_ref, a2_ref, lb_ref, a1_ref, h_ref, gf_ref,
                 y_ref, acc_ref, *, final_norm):
    j = pl.program_id(1)
    te = vt_ref.shape[1]

    @pl.when(j == 0)
    def _():
        acc_ref[...] = jnp.zeros_like(acc_ref)

    tm = xn_ref.shape[1]

    def row_bf16(ref, hh, blk):
        word = jnp.broadcast_to(ref[hh, blk:blk + 1, :], (8, tm))
        tile = pltpu.bitcast(word, BF16)
        return jnp.concatenate([tile] * (PEER_NKEYS // tile.shape[0]), axis=0)

    sub = min(te, SUB_EXPERTS)
    ps = []
    for s0 in range(0, te, sub):
        u_rows = pltpu.bitcast(u_ref[s0 // 2:(s0 + sub) // 2, :], BF16)
        ht = jnp.dot(u_rows, xn_ref[...], preferred_element_type=F32)
        hb = ht.astype(BF16)
        act = (0.5 * hb) * (1.0 + lax.erf(hb * (1.0 / math.sqrt(2.0))))
        gates = []
        for bb in range(s0 // PEER_NKEYS, (s0 + sub) // PEER_NKEYS):
            g = jnp.zeros((PEER_NKEYS, tm), BF16)
            for hh in range(PEER_HEADS):
                lim = row_bf16(lb_ref, hh, bb)
                w1 = row_bf16(a1_ref, hh, bb)
                r2 = pltpu.bitcast(r2_ref[hh], BF16)
                a2 = pltpu.bitcast(a2_ref[hh], BF16)
                g = g + jnp.where(r2 < lim, w1 * a2, jnp.zeros_like(g))
            gates.append(g)
        ps.append(act.astype(BF16) * jnp.concatenate(gates, axis=0))
    p = jnp.concatenate(ps, axis=0)
    acc_ref[...] += jnp.dot(vt_ref[...], p, preferred_element_type=F32)

    @pl.when(j == pl.num_programs(1) - 1)
    def _():
        out = h_ref[...] + acc_ref[...].T
        y_ref[...] = _rms(out, gf_ref[...]) if final_norm else out


def _peer(xnt, u_words, vt_bf, r2, a2, lb, a1, h, gf, final_norm, tm, te):
    d, n = xnt.shape
    ne = 2 * u_words.shape[0]
    gate_spec = pl.BlockSpec((PEER_HEADS, PEER_NKEYS // 2, tm), lambda i, j: (0, 0, i))
    nblk = te // PEER_NKEYS
    assert nblk % 8 == 0, "first-key rows are delivered as whole sublane tiles"
    row_spec = pl.BlockSpec((PEER_HEADS, nblk, tm), lambda i, j: (0, j, i))
    return pl.pallas_call(
        functools.partial(_peer_kernel, final_norm=final_norm),
        grid=(n // tm, ne // te),
        in_specs=[pl.BlockSpec((d, tm), lambda i, j: (0, i)),
                  pl.BlockSpec((te // 2, d), lambda i, j: (j, 0)),
                  pl.BlockSpec((d, te), lambda i, j: (0, j)),
                  gate_spec, gate_spec, row_spec, row_spec,
                  pl.BlockSpec((tm, d), lambda i, j: (i, 0)),
                  pl.BlockSpec((1, d), lambda i, j: (0, 0))],
        out_specs=pl.BlockSpec((tm, d), lambda i, j: (i, 0)),
        out_shape=jax.ShapeDtypeStruct((n, d), F32),
        scratch_shapes=[pltpu.VMEM((d, tm), F32)],
        compiler_params=_params(("parallel", "arbitrary")),
        name="peer_experts",
    )(xnt, u_words, vt_bf, r2, a2, lb, a1, h, gf)


def _tile(n, pref):
    t = min(n, pref)
    assert n % t == 0, (n, t)
    return t


def kernel(x_prompt, x_sample, cache_win_k, cache_win_v, state_conv, norm1_g, w_in, conv_w,
           attn_out_g, conv_out_g, w_out, norm2_g, peer_wq, peer_sub_keys, peer_u, peer_v,
           final_norm_g):
    bp, s, d = x_prompt.shape
    bs, t, _ = x_sample.shape
    depth = w_in.shape[0]
    past_len = 8192
    win_p = min(BRANCHES[-1][0], s)
    np_, ns = bp * s, bs * t

    tm_p = _tile(s, 512)
    tq = _tile(s, 512)
    tk = tq
    cos_p, sin_p = _rope_tables(jnp.arange(s, dtype=jnp.int32))
    pos_s = past_len + jnp.repeat(jnp.arange(t, dtype=jnp.int32), bs)
    cos_s, sin_s = _rope_tables(pos_s)

    hp = x_prompt.reshape(np_, d)
    hs = x_sample.transpose(1, 0, 2).reshape(ns, d)
    outs = {k: [] for k in ("kp", "vp", "cp", "ks", "vs", "cs")}
    row = lambda g: g.reshape(1, -1)
    y_p = y_s = None
    for l in range(depth):
        w_in_bf = w_in[l].astype(BF16)
        w_out_bf = w_out[l].astype(BF16)
        wq_bf = peer_wq[l].astype(BF16)
        keys_bf = peer_sub_keys[l].astype(BF16)
        u_words = _pack_pairs(peer_u[l])
        vt_bf = _pair_order(peer_v[l]).T.astype(BF16)
        last = l == depth - 1

        def tail(hres, attn, cb, z, state, stride, tiles_per_seq, tm):
            h_mid, xn2, xn2t = _mix(attn, cb, z, state, hres, conv_w[l], row(attn_out_g[l]),
                                    row(conv_out_g[l]), w_out_bf, row(norm2_g[l]), tm, stride,
                                    tiles_per_seq)
            n = hres.shape[0]
            r2, a2, lb, a1 = _route(xn2, wq_bf, keys_bf, _tile(n, 512))
            return _peer(xn2t, u_words, vt_bf, r2, a2, lb, a1, h_mid, row(final_norm_g), last,
                         _tile(n, 512), 2048)

        q, vb, kt, vt, cb, z = _inproj(hp, row(norm1_g[l]), w_in_bf, cos_p, sin_p, tm_p, s)
        attn = _attn_prompt(q.reshape(bp, s, ATTN_W), kt, vb.reshape(bp, s, ATTN_W),
                            tq, tk).reshape(np_, ATTN_W)
        rows_first = lambda a: a.reshape(bp, N_HEADS, HEAD_DIM, s).transpose(0, 3, 1, 2)
        outs["kp"].append(rows_first(kt)[:, s - win_p:])
        outs["vp"].append(rows_first(vt)[:, s - win_p:])
        outs["cp"].append(z.reshape(bp, s, CONV_W)[:, s - (CONV_K - 1):])
        zero_state = jnp.zeros((bp, 8, CONV_W), F32)
        y_p = tail(hp, attn, cb, z, zero_state, 1, s // tm_p, tm_p)

        q, _, kt, vt, cb, z = _inproj(hs, row(norm1_g[l]), w_in_bf, cos_s, sin_s,
                                      _tile(ns, 512), ns)
        to_bt = lambda a: a.reshape(t, bs, -1).transpose(1, 0, 2)
        k_bt, v_bt = to_bt(kt[0].T), to_bt(vt[0].T)
        feat_major = lambda c: c.transpose(0, 2, 3, 1).reshape(bs, ATTN_W, c.shape[1])
        attn = _attn_sample(to_bt(q), k_bt, v_bt,
                            feat_major(cache_win_k[l]), feat_major(cache_win_v[l]))
        attn = attn.transpose(1, 0, 2).reshape(ns, ATTN_W)
        outs["ks"].append(k_bt.reshape(bs, t, N_HEADS, HEAD_DIM))
        outs["vs"].append(v_bt.reshape(bs, t, N_HEADS, HEAD_DIM))
        outs["cs"].append(to_bt(z)[:, t - (CONV_K - 1):])
        state = state_conv[l].transpose(1, 0, 2).reshape(1, (CONV_K - 1) * bs, CONV_W)
        y_s = tail(hs, attn, cb, z, state, bs, 1, ns)
        hp, hs = y_p, y_s

    y_prompt = y_p.reshape(bp, s, d)
    y_sample = y_s.reshape(t, bs, d).transpose(1, 0, 2)
    st = lambda name: jnp.stack(outs[name])
    return (y_prompt, y_sample, st("kp"), st("vp"), st("cp"), st("ks"), st("vs"), st("cs"))
```

```python
import functools
import math

import jax
import jax.numpy as jnp
from jax import lax
from jax.experimental import pallas as pl
from jax.experimental.pallas import tpu as pltpu

F32 = jnp.float32
BF16 = jnp.bfloat16

HEAD_DIM = 64
N_HEADS = 12
ATTN_W = N_HEADS * HEAD_DIM
CONV_W = 256
CONV_K = 3
BRANCHES = ((128, 1), (512, 4), (2048, 16))
ROPE_THETA = 10000.0
EPS = 1e-6
PEER_HEADS = 8
PEER_NKEYS = 128
PEER_TOPK = 16
NEG = -1e30
LANES = 128
SUB_EXPERTS = 128
VMEM_LIMIT = 56 * 1024 * 1024

_NT = (((1,), (1,)), ((), ()))


def _rms(x, g):
    return x * lax.rsqrt(jnp.mean(x * x, axis=-1, keepdims=True) + EPS) * g


def _pack_pairs(x):
    r, n = x.shape
    bits = lax.bitcast_convert_type(x.astype(BF16).astype(F32), jnp.uint32) >> 16
    bits = bits.reshape(r // 16, 2, 8, n)
    return (bits[:, 0] | (bits[:, 1] << 16)).reshape(r // 2, n)


def _pair_order(x):
    r = x.shape[0]
    return x.reshape(r // 16, 2, 8, -1).transpose(0, 2, 1, 3).reshape(x.shape)


def _params(sem):
    return pltpu.CompilerParams(dimension_semantics=sem, vmem_limit_bytes=VMEM_LIMIT)


def _inproj_kernel(x_ref, g_ref, w_ref, cos_ref, sin_ref,
                   q_ref, vb_ref, kt_ref, vt_ref, cb_ref, z_ref):
    xb = _rms(x_ref[...], g_ref[...]).astype(BF16)
    tm = xb.shape[0]
    lane = lax.broadcasted_iota(jnp.int32, (tm, LANES), 1)
    low_half = (lane % HEAD_DIM) < (HEAD_DIM // 2)
    cos = cos_ref[...]
    sin = sin_ref[...]

    def rope(t):
        partner = jnp.where(low_half, pltpu.roll(t, LANES - HEAD_DIM // 2, 1),
                            pltpu.roll(t, HEAD_DIM // 2, 1))
        return t * cos + partner * sin

    scale = 1.0 / math.sqrt(HEAD_DIM)
    q = jnp.dot(xb, w_ref[:, 0:ATTN_W], preferred_element_type=F32)
    for c in range(ATTN_W // LANES):
        sl = slice(c * LANES, (c + 1) * LANES)
        q_ref[:, sl] = (rope(q[:, sl]) * scale).astype(BF16)
    k = jnp.dot(xb, w_ref[:, ATTN_W:2 * ATTN_W], preferred_element_type=F32)
    for c in range(ATTN_W // LANES):
        sl = slice(c * LANES, (c + 1) * LANES)
        kt_ref[sl, :] = rope(k[:, sl]).T
    v = jnp.dot(xb, w_ref[:, 2 * ATTN_W:3 * ATTN_W], preferred_element_type=F32)
    vt_ref[...] = v.T
    vb_ref[...] = v.astype(BF16)
    c0 = 3 * ATTN_W
    conv = jnp.dot(xb, w_ref[:, c0:c0 + 3 * CONV_W], preferred_element_type=F32)
    cb_ref[...] = conv[:, 0:CONV_W]
    z_ref[...] = conv[:, CONV_W:2 * CONV_W] * conv[:, 2 * CONV_W:3 * CONV_W]


def _inproj(x, g, w_bf, cos, sin, tm, seq):
    n, d = x.shape
    win = w_bf.shape[1]
    period = cos.shape[0] // tm
    per_seq = seq // tm
    row = lambda i: (i, 0)
    const = lambda i: (0, 0)
    feat = pl.BlockSpec((None, ATTN_W, tm), lambda i: (i // per_seq, 0, i % per_seq))
    return pl.pallas_call(
        _inproj_kernel,
        grid=(n // tm,),
        in_specs=[pl.BlockSpec((tm, d), row),
                  pl.BlockSpec((1, d), const),
                  pl.BlockSpec((d, win), const),
                  pl.BlockSpec((tm, LANES), lambda i: (i % period, 0)),
                  pl.BlockSpec((tm, LANES), lambda i: (i % period, 0))],
        out_specs=[pl.BlockSpec((tm, ATTN_W), row), pl.BlockSpec((tm, ATTN_W), row),
                   feat, feat, pl.BlockSpec((tm, CONV_W), row),
                   pl.BlockSpec((tm, CONV_W), row)],
        out_shape=[jax.ShapeDtypeStruct((n, ATTN_W), BF16),
                   jax.ShapeDtypeStruct((n, ATTN_W), BF16),
                   jax.ShapeDtypeStruct((n // seq, ATTN_W, seq), F32),
                   jax.ShapeDtypeStruct((n // seq, ATTN_W, seq), F32),
                   jax.ShapeDtypeStruct((n, CONV_W), F32),
                   jax.ShapeDtypeStruct((n, CONV_W), F32)],
        compiler_params=_params(("parallel",)),
        name="inproj",
    )(x, g, w_bf, cos, sin)


def _rope_tables(pos):
    half = HEAD_DIM // 2
    inv = jnp.exp(-math.log(ROPE_THETA) * jnp.arange(half, dtype=F32) * (2.0 / HEAD_DIM))
    ang = pos.astype(F32)[:, None] * inv[None, :]
    cos = jnp.cos(ang)
    sin = jnp.sin(ang)
    reps = LANES // HEAD_DIM
    cos_t = jnp.tile(jnp.concatenate([cos, cos], axis=1), (1, reps))
    sin_t = jnp.tile(jnp.concatenate([-sin, sin], axis=1), (1, reps))
    return cos_t, sin_t


def _branch_count(dist):
    cnt = jnp.zeros(dist.shape, F32)
    for window, dil in BRANCHES:
        hit = (dist <= window) & ((dist & (dil - 1)) == 0)
        cnt = cnt + jnp.where(hit, 1.0, 0.0)
    return jnp.where(dist >= 0, cnt, 0.0)


def _attn_prompt_kernel(q_ref, k_ref, v_ref, bias_ref, o_ref,
                        kt_ref, vz_ref, m_ref, l_ref, acc_ref, *, tq, tk):
    qi = pl.program_id(2)
    heads = LANES // HEAD_DIM
    ratio = tq // tk
    head_of_lane = lax.broadcasted_iota(jnp.int32, (tq, LANES), 1) // HEAD_DIM
    head_of_key_lane = lax.broadcasted_iota(jnp.int32, (tk, LANES), 1) // HEAD_DIM

    @pl.when(qi == 0)
    def _():
        for jb in range(v_ref.shape[0] // tk):
            rows = slice(jb * tk, (jb + 1) * tk)
            kt_ref[jb] = k_ref[:, rows].astype(BF16)
            vb = v_ref[rows, :]
            for h in range(heads):
                vz_ref[heads * jb + h] = jnp.where(head_of_key_lane == h, vb, jnp.zeros_like(vb))

    q = q_ref[...]
    qz = [jnp.where(head_of_lane == h, q, jnp.zeros_like(q)) for h in range(heads)]
    m_ref[...] = jnp.full(m_ref.shape, NEG, F32)
    l_ref[...] = jnp.zeros(l_ref.shape, F32)
    acc_ref[...] = jnp.zeros(acc_ref.shape, F32)

    def body(j, _):
        bias = bias_ref[qi * ratio + (ratio - 1) - j]
        kt = kt_ref[j]
        alphas = []
        pv = None
        for h in range(heads):
            s = jnp.dot(qz[h], kt, preferred_element_type=F32) + bias
            m_prev = m_ref[h]
            m_next = jnp.maximum(m_prev, jnp.max(s, axis=-1, keepdims=True))
            p = jnp.concatenate(
                [jnp.exp(s[:, c * LANES:(c + 1) * LANES] - m_next) for c in range(tk // LANES)],
                axis=-1)
            alpha = jnp.exp(m_prev - m_next)
            l_ref[h] = alpha * l_ref[h] + jnp.sum(p, axis=-1, keepdims=True)
            m_ref[h] = m_next
            alphas.append(alpha)
            d = jnp.dot(p.astype(BF16), vz_ref[heads * j + h], preferred_element_type=F32)
            pv = d if pv is None else pv + d
        alpha_both = jnp.where(head_of_lane == 0, alphas[0], alphas[1])
        acc_ref[...] = alpha_both * acc_ref[...] + pv
        return 0

    lax.fori_loop(0, (qi + 1) * ratio, body, 0)
    l_both = jnp.where(head_of_lane == 0, l_ref[0], l_ref[1])
    o_ref[...] = acc_ref[...] / l_both


def _attn_bias(s, tq, tk):
    first = jnp.arange(s // tk, dtype=jnp.int32)[:, None, None] - (tq // tk - 1)
    dist = first * tk + jnp.arange(tq, dtype=jnp.int32)[None, :, None] \
        - jnp.arange(tk, dtype=jnp.int32)[None, None, :]
    cnt = _branch_count(dist)
    return jnp.where(cnt > 0.0, jnp.log(jnp.maximum(cnt, 1.0)), NEG)


def _attn_prompt(q, k, v, tq, tk):
    b, s, _ = q.shape
    heads = LANES // HEAD_DIM
    assert heads == 2 and tk % LANES == 0 and tq % tk == 0
    nkb = s // tk
    blk = lambda bi, hp, qi: (bi, qi, hp)
    full = lambda bi, hp, qi: (bi, 0, hp)
    return pl.pallas_call(
        functools.partial(_attn_prompt_kernel, tq=tq, tk=tk),
        grid=(b, ATTN_W // LANES, s // tq),
        in_specs=[pl.BlockSpec((None, tq, LANES), blk),
                  pl.BlockSpec((None, LANES, s), lambda bi, hp, qi: (bi, hp, 0)),
                  pl.BlockSpec((None, s, LANES), full),
                  pl.BlockSpec((nkb, tq, tk), lambda bi, hp, qi: (0, 0, 0))],
        out_specs=pl.BlockSpec((None, tq, LANES), blk),
        out_shape=jax.ShapeDtypeStruct((b, s, ATTN_W), F32),
        scratch_shapes=[pltpu.VMEM((nkb, LANES, tk), BF16),
                        pltpu.VMEM((nkb * heads, tk, LANES), BF16),
                        pltpu.VMEM((heads, tq, LANES), F32),
                        pltpu.VMEM((heads, tq, LANES), F32),
                        pltpu.VMEM((tq, LANES), F32)],
        compiler_params=_params(("parallel", "parallel", "arbitrary")),
        name="attn_prompt",
    )(q, k, v, _attn_bias(s, tq, tk))


def _attn_sample_kernel(q_ref, kn_ref, vn_ref, kc_ref, vc_ref, o_ref, *, chunk):
    t = q_ref.shape[0]
    n_past = kc_ref.shape[1]
    rows = N_HEADS * t
    qt = jnp.concatenate([q_ref[...]] * N_HEADS, axis=0)
    r_id = lax.broadcasted_iota(jnp.int32, (rows, ATTN_W), 0)
    c_id = lax.broadcasted_iota(jnp.int32, (rows, ATTN_W), 1)
    own = (r_id // t) == (c_id // HEAD_DIM)
    qbd = jnp.where(own, qt, jnp.zeros_like(qt))

    def masked(s, first_key):
        nk = s.shape[1]
        qpos = n_past + lax.broadcasted_iota(jnp.int32, (rows, nk), 0) % t
        kpos = first_key + lax.broadcasted_iota(jnp.int32, (rows, nk), 1)
        cnt = _branch_count(qpos - kpos)
        return jnp.where(cnt > 0.0, s, NEG), cnt

    cols = [slice(c, c + chunk) for c in range(0, n_past, chunk)]
    parts = [masked(jnp.dot(qbd, kc_ref[:, sl].astype(BF16), preferred_element_type=F32), sl.start)
             for sl in cols]
    parts.append(masked(lax.dot_general(qbd, kn_ref[...].astype(BF16), _NT,
                                        preferred_element_type=F32), n_past))
    m = functools.reduce(jnp.maximum, [jnp.max(s, axis=-1, keepdims=True) for s, _ in parts])
    den = jnp.zeros((rows, 1), F32)
    acc = jnp.zeros((rows, ATTN_W), F32)
    for c, (s, cnt) in enumerate(parts):
        p = cnt * jnp.exp(s - m)
        den = den + jnp.sum(p, axis=-1, keepdims=True)
        pb = p.astype(BF16)
        if c < len(cols):
            acc = acc + lax.dot_general(pb, vc_ref[:, cols[c]].astype(BF16), _NT,
                                        preferred_element_type=F32)
        else:
            acc = acc + jnp.dot(pb, vn_ref[...].astype(BF16), preferred_element_type=F32)
    o_full = jnp.where(own, acc / den, 0.0)
    out = o_full[0:t, :]
    for h in range(1, N_HEADS):
        out = out + o_full[h * t:(h + 1) * t, :]
    o_ref[...] = out


def _attn_sample(q, k_new, v_new, k_cache, v_cache):
    b, t, _ = q.shape
    per_b = lambda bi: (bi, 0, 0)
    blk = lambda a: pl.BlockSpec((None,) + a.shape[1:], per_b)
    args = (q, k_new, v_new, k_cache, v_cache)
    return pl.pallas_call(
        functools.partial(_attn_sample_kernel, chunk=512),
        grid=(b,),
        in_specs=[blk(a) for a in args],
        out_specs=pl.BlockSpec((None, t, ATTN_W), per_b),
        out_shape=jax.ShapeDtypeStruct((b, t, ATTN_W), F32),
        compiler_params=_params(("parallel",)),
        name="attn_sample",
    )(*args)


def _mix_kernel(attn_ref, cb_ref, z_ref, zprev_ref, state_ref, x_ref, cw_ref, ga_ref, gc_ref,
                wo_ref, g2_ref, h_ref, xn_ref, xnt_ref, zbuf, *, stride, tiles_per_seq):
    tm = z_ref.shape[0]
    hb = state_ref.shape[0]
    if tiles_per_seq > 1:
        first = (pl.program_id(0) % tiles_per_seq) == 0
        zbuf[0:hb, :] = jnp.where(first, state_ref[...], zprev_ref[...])
    else:
        zbuf[0:hb, :] = state_ref[...]
    z = z_ref[...]
    zbuf[hb:hb + tm, :] = z
    cw = cw_ref[...]
    y = (cw[0:1, :] * zbuf[hb - 2 * stride:hb - 2 * stride + tm, :]
         + cw[1:2, :] * zbuf[hb - stride:hb - stride + tm, :]
         + cw[2:3, :] * z)
    conv_out = cb_ref[...] * y
    a = _rms(attn_ref[...], ga_ref[...]).astype(BF16)
    c = _rms(conv_out, gc_ref[...]).astype(BF16)
    mixed = (jnp.dot(a, wo_ref[0:ATTN_W, :], preferred_element_type=F32)
             + jnp.dot(c, wo_ref[ATTN_W:ATTN_W + CONV_W, :], preferred_element_type=F32))
    h = x_ref[...] + mixed
    h_ref[...] = h
    xn = _rms(h, g2_ref[...])
    xn_ref[...] = xn.astype(BF16)
    xnt_ref[...] = xn.T.astype(BF16)


def _mix(attn, cb, z, state, x, conv_w, ga, gc, wo_bf, g2, tm, stride, tiles_per_seq):
    n, d = x.shape
    hb = state.shape[1]
    row = lambda i: (i, 0)
    const = lambda i: (0, 0)
    prev = lambda i: (jnp.maximum(i * (tm // hb) - 1, 0), 0)
    return pl.pallas_call(
        functools.partial(_mix_kernel, stride=stride, tiles_per_seq=tiles_per_seq),
        grid=(n // tm,),
        in_specs=[pl.BlockSpec((tm, ATTN_W), row),
                  pl.BlockSpec((tm, CONV_W), row),
                  pl.BlockSpec((tm, CONV_W), row),
                  pl.BlockSpec((hb, CONV_W), prev),
                  pl.BlockSpec((None, hb, CONV_W), lambda i: (i // tiles_per_seq, 0, 0)),
                  pl.BlockSpec((tm, d), row),
                  pl.BlockSpec((CONV_K, CONV_W), const),
                  pl.BlockSpec((1, ATTN_W), const),
                  pl.BlockSpec((1, CONV_W), const),
                  pl.BlockSpec((d, d), const),
                  pl.BlockSpec((1, d), const)],
        out_specs=[pl.BlockSpec((tm, d), row), pl.BlockSpec((tm, d), row),
                   pl.BlockSpec((d, tm), lambda i: (0, i))],
        out_shape=[jax.ShapeDtypeStruct((n, d), F32), jax.ShapeDtypeStruct((n, d), BF16),
                   jax.ShapeDtypeStruct((d, n), BF16)],
        scratch_shapes=[pltpu.VMEM((hb + tm, CONV_W), F32)],
        compiler_params=_params(("parallel",)),
        name="mix",
    )(attn, cb, z, z, state, x, conv_w, ga, gc, wo_bf, g2)


def _top16(s, order):
    unit = 2.0 ** 120
    big = jnp.int32(2 ** 30)
    vals = []
    for r in range(PEER_TOPK):
        m = jnp.max(s, axis=0, keepdims=True)
        sel = order == jnp.min(jnp.where(s == m, order, big), axis=0, keepdims=True)
        s = jnp.where(sel, -(128.0 + r) * unit, s)
        vals.append(m)
    rank = jnp.where(s <= -128.0 * unit, s * (-1.0 / unit) - 128.0, float(PEER_TOPK))
    return rank, jnp.concatenate(vals, axis=0)


def _top16_paired(s):
    unit = 2.0 ** 120
    mark = -128.0 * unit
    n2 = s.shape[0] // 2
    a, b = s[:n2], s[n2:]
    a_first = a >= b
    hi, lo = jnp.maximum(a, b), jnp.minimum(a, b)
    vals = []
    for r in range(PEER_TOPK):
        m = jnp.max(hi, axis=0, keepdims=True)
        sel = hi == m
        hi = jnp.where(sel, lo, hi)
        lo = jnp.where(sel, -(128.0 + r) * unit, lo)
        vals.append(m)
    decode = lambda x: x * (-1.0 / unit) - 128.0
    none = float(PEER_TOPK)
    twice = hi <= mark
    rank_big = jnp.where(twice, decode(hi), jnp.where(lo <= mark, decode(lo), none))
    rank_small = jnp.where(twice, decode(lo), none)
    rank = jnp.concatenate([jnp.where(a_first, rank_big, rank_small),
                            jnp.where(a_first, rank_small, rank_big)], axis=0)
    return rank, jnp.concatenate(vals, axis=0)


_CAND_GROUPS = ((0, 0), (0, 8), (1, 0), (2, 0), (3, 0), (4, 0), (5, 0), (6, 0), (7, 0))


def _route_tile(s1, s2, exact):
    tm = s1.shape[1]
    sub = 8
    if exact:
        iota128 = lax.broadcasted_iota(jnp.int32, s1.shape, 0)
        top16 = lambda s, order: _top16(s, iota128 if order is None else order)
    else:
        top16 = lambda s, order: _top16_paired(s)
    rank1, sv1 = top16(s1, None)
    rank2, sv2 = top16(s2, None)
    groups = [sv1[a:a + 1, :] + sv2[b0:b0 + sub, :] for a, b0 in _CAND_GROUPS]
    groups.append(sv1[sub:2 * sub, :] + sv2[0:1, :])
    cand = jnp.concatenate(groups, axis=0)
    order = None
    if exact:
        i8 = lax.broadcasted_iota(jnp.int32, (sub, tm), 0)
        order = jnp.concatenate([a * PEER_TOPK + b0 + i8 for a, b0 in _CAND_GROUPS]
                                + [(sub + i8) * PEER_TOPK], axis=0)
    crank, _ = top16(cand, order)
    chosen = crank < float(PEER_TOPK)
    cmax = sv1[0:1, :] + sv2[0:1, :]
    zsum = jnp.sum(jnp.where(chosen, jnp.exp(cand - cmax), 0.0), axis=0, keepdims=True)
    picked = jnp.where(chosen, 1.0, 0.0)
    n_of_a = [jnp.sum(picked[0:2 * sub, :], axis=0, keepdims=True)]
    for a in range(1, sub):
        n_of_a.append(jnp.sum(picked[(a + 1) * sub:(a + 2) * sub, :], axis=0, keepdims=True))
    last = (len(_CAND_GROUPS)) * sub
    for a in range(sub, PEER_TOPK):
        n_of_a.append(picked[last + a - sub:last + a - sub + 1, :])
    lb = jnp.zeros(rank1.shape, F32)
    for a in range(PEER_TOPK):
        lb = jnp.where(rank1 == float(a), n_of_a[a], lb)
    a2 = jnp.exp(s2 - sv2[0:1, :])
    a1 = jnp.exp(s1 - sv1[0:1, :]) / zsum
    tied = None
    if not exact:
        def extra(rk):
            n_sel = jnp.sum(jnp.where(rk < float(PEER_TOPK), 1.0, 0.0), axis=0, keepdims=True)
            return jnp.max(n_sel) > float(PEER_TOPK)
        tied = extra(rank1) | extra(rank2) | extra(crank)
    return rank2, a2, lb, a1, tied


def _route_kernel(xn_ref, wq_ref, keys_ref, r2_ref, a2_ref, lb_ref, a1_ref, q_scr):
    h = pl.program_id(1)
    nsub = 2 * PEER_HEADS

    @pl.when(h == 0)
    def _():
        q = jnp.dot(xn_ref[...], wq_ref[...], preferred_element_type=F32)
        for c in range(nsub):
            q_scr[c] = q[:, c * PEER_NKEYS:(c + 1) * PEER_NKEYS].astype(BF16)

    s1 = lax.dot_general(keys_ref[0], q_scr[2 * h], _NT, preferred_element_type=F32)
    s2 = lax.dot_general(keys_ref[1], q_scr[2 * h + 1], _NT, preferred_element_type=F32)

    def twice(x):
        hi = lax.bitcast_convert_type(x.astype(BF16).astype(F32), jnp.uint32)
        return hi | (hi >> 16)

    def emit(rank2, a2, lb, a1):
        r2_ref[...] = _pack_pairs(rank2)
        a2_ref[...] = _pack_pairs(a2)
        lb_ref[...] = twice(lb)
        a1_ref[...] = twice(a1)

    *outs, tied = _route_tile(s1, s2, exact=False)
    emit(*outs)

    @pl.when(tied)
    def _():
        emit(*_route_tile(s1, s2, exact=True)[:4])


def _route(xn, wq_bf, keys_bf, tm):
    n, d = xn.shape
    qd = wq_bf.shape[1]
    out_spec = lambda rows: pl.BlockSpec((None, rows, tm), lambda i, h: (h, 0, i))
    out_sds = lambda rows: jax.ShapeDtypeStruct((PEER_HEADS, rows, n), jnp.uint32)
    half = PEER_NKEYS // 2
    return pl.pallas_call(
        _route_kernel,
        grid=(n // tm, PEER_HEADS),
        in_specs=[pl.BlockSpec((tm, d), lambda i, h: (i, 0)),
                  pl.BlockSpec((d, qd), lambda i, h: (0, 0)),
                  pl.BlockSpec((None, 2, PEER_NKEYS, PEER_NKEYS), lambda i, h: (h, 0, 0, 0))],
        out_specs=[out_spec(half), out_spec(half), out_spec(PEER_NKEYS), out_spec(PEER_NKEYS)],
        out_shape=[out_sds(half), out_sds(half), out_sds(PEER_NKEYS), out_sds(PEER_NKEYS)],
        scratch_shapes=[pltpu.VMEM((2 * PEER_HEADS, tm, PEER_NKEYS), BF16)],
        compiler_params=_params(("parallel", "arbitrary")),
        name="peer_route",
    )(xn, wq_bf, keys_bf)


def _peer_kernel(xn_ref, u_ref, vt_ref, r2_ref, a2_ref, lb_ref, a1_ref, h_ref, gf_ref,
                 y_ref, acc_ref, *, final_norm):
    j = pl.program_id(1)
    te = vt_ref.shape[1]

    @pl.when(j == 0)
    def _():
        acc_ref[...] = jnp.zeros_like(acc_ref)

    tm = xn_ref.shape[1]

    def row_bf16(ref, hh, blk):
        word = jnp.broadcast_to(ref[hh, blk:blk + 1, :], (8, tm))
        tile = pltpu.bitcast(word, BF16)
        return jnp.concatenate([tile] * (PEER_NKEYS // tile.shape[0]), axis=0)

    sub = min(te, SUB_EXPERTS)
    ps = []
    for s0 in range(0, te, sub):
        u_rows = pltpu.bitcast(u_ref[s0 // 2:(s0 + sub) // 2, :], BF16)
        ht = jnp.dot(u_rows, xn_ref[...], preferred_element_type=F32)
        hb = ht.astype(BF16)
        act = (0.5 * hb) * (1.0 + lax.erf(hb * (1.0 / math.sqrt(2.0))))
        gates = []
        for bb in range(s0 // PEER_NKEYS, (s0 + sub) // PEER_NKEYS):
            g = jnp.zeros((PEER_NKEYS, tm), BF16)
            for hh in range(PEER_HEADS):
                lim = row_bf16(lb_ref, hh, bb)
                w1 = row_bf16(a1_ref, hh, bb)
                r2 = pltpu.bitcast(r2_ref[hh], BF16)
                a2 = pltpu.bitcast(a2_ref[hh], BF16)
                g = g + jnp.where(r2 < lim, w1 * a2, jnp.zeros_like(g))
            gates.append(g)
        ps.append(act.astype(BF16) * jnp.concatenate(gates, axis=0))
    p = jnp.concatenate(ps, axis=0)
    acc_ref[...] += jnp.dot(vt_ref[...], p, preferred_element_type=F32)

    @pl.when(j == pl.num_programs(1) - 1)
    def _():
        out = h_ref[...] + acc_ref[...].T
        y_ref[...] = _rms(out, gf_ref[...]) if final_norm else out


def _peer(xnt, u_words, vt_bf, r2, a2, lb, a1, h, gf, final_norm, tm, te):
    d, n = xnt.shape
    ne = 2 * u_words.shape[0]
    once = pl.Buffered(1)
    gate_spec = pl.BlockSpec((PEER_HEADS, PEER_NKEYS // 2, tm), lambda i, j: (0, 0, i),
                             pipeline_mode=once)
    nblk = te // PEER_NKEYS
    assert nblk % 8 == 0, "first-key rows are delivered as whole sublane tiles"
    row_spec = pl.BlockSpec((PEER_HEADS, nblk, tm), lambda i, j: (0, j, i))
    return pl.pallas_call(
        functools.partial(_peer_kernel, final_norm=final_norm),
        grid=(n // tm, ne // te),
        in_specs=[pl.BlockSpec((d, tm), lambda i, j: (0, i), pipeline_mode=once),
                  pl.BlockSpec((te // 2, d), lambda i, j: (j, 0)),
                  pl.BlockSpec((d, te), lambda i, j: (0, j)),
                  gate_spec, gate_spec, row_spec, row_spec,
                  pl.BlockSpec((tm, d), lambda i, j: (i, 0), pipeline_mode=once),
                  pl.BlockSpec((1, d), lambda i, j: (0, 0))],
        out_specs=pl.BlockSpec((tm, d), lambda i, j: (i, 0)),
        out_shape=jax.ShapeDtypeStruct((n, d), F32),
        scratch_shapes=[pltpu.VMEM((d, tm), F32)],
        compiler_params=_params(("parallel", "arbitrary")),
        name="peer_experts",
    )(xnt, u_words, vt_bf, r2, a2, lb, a1, h, gf)


def _tile(n, pref):
    t = min(n, pref)
    assert n % t == 0, (n, t)
    return t


def kernel(x_prompt, x_sample, cache_win_k, cache_win_v, state_conv, norm1_g, w_in, conv_w,
           attn_out_g, conv_out_g, w_out, norm2_g, peer_wq, peer_sub_keys, peer_u, peer_v,
           final_norm_g):
    bp, s, d = x_prompt.shape
    bs, t, _ = x_sample.shape
    depth = w_in.shape[0]
    past_len = 8192
    win_p = min(BRANCHES[-1][0], s)
    np_, ns = bp * s, bs * t

    tm_p = _tile(s, 512)
    tq = _tile(s, 512)
    tk = tq
    cos_p, sin_p = _rope_tables(jnp.arange(s, dtype=jnp.int32))
    pos_s = past_len + jnp.repeat(jnp.arange(t, dtype=jnp.int32), bs)
    cos_s, sin_s = _rope_tables(pos_s)

    hp = x_prompt.reshape(np_, d)
    hs = x_sample.transpose(1, 0, 2).reshape(ns, d)
    outs = {k: [] for k in ("kp", "vp", "cp", "ks", "vs", "cs")}
    row = lambda g: g.reshape(1, -1)
    y_p = y_s = None
    for l in range(depth):
        w_in_bf = w_in[l].astype(BF16)
        w_out_bf = w_out[l].astype(BF16)
        wq_bf = peer_wq[l].astype(BF16)
        keys_bf = peer_sub_keys[l].astype(BF16)
        u_words = _pack_pairs(peer_u[l])
        vt_bf = _pair_order(peer_v[l]).T.astype(BF16)
        last = l == depth - 1

        def tail(hres, attn, cb, z, state, stride, tiles_per_seq, tm):
            h_mid, xn2, xn2t = _mix(attn, cb, z, state, hres, conv_w[l], row(attn_out_g[l]),
                                    row(conv_out_g[l]), w_out_bf, row(norm2_g[l]), tm, stride,
                                    tiles_per_seq)
            n = hres.shape[0]
            r2, a2, lb, a1 = _route(xn2, wq_bf, keys_bf, _tile(n, 512))
            return _peer(xn2t, u_words, vt_bf, r2, a2, lb, a1, h_mid, row(final_norm_g), last,
                         _tile(n, 1024), 2048)

        q, vb, kt, vt, cb, z = _inproj(hp, row(norm1_g[l]), w_in_bf, cos_p, sin_p, tm_p, s)
        attn = _attn_prompt(q.reshape(bp, s, ATTN_W), kt, vb.reshape(bp, s, ATTN_W),
                            tq, tk).reshape(np_, ATTN_W)
        rows_first = lambda a: a.reshape(bp, N_HEADS, HEAD_DIM, s).transpose(0, 3, 1, 2)
        outs["kp"].append(rows_first(kt)[:, s - win_p:])
        outs["vp"].append(rows_first(vt)[:, s - win_p:])
        outs["cp"].append(z.reshape(bp, s, CONV_W)[:, s - (CONV_K - 1):])
        zero_state = jnp.zeros((bp, 8, CONV_W), F32)
        y_p = tail(hp, attn, cb, z, zero_state, 1, s // tm_p, tm_p)

        q, _, kt, vt, cb, z = _inproj(hs, row(norm1_g[l]), w_in_bf, cos_s, sin_s,
                                      _tile(ns, 512), ns)
        to_bt = lambda a: a.reshape(t, bs, -1).transpose(1, 0, 2)
        k_bt, v_bt = to_bt(kt[0].T), to_bt(vt[0].T)
        feat_major = lambda c: c.transpose(0, 2, 3, 1).reshape(bs, ATTN_W, c.shape[1])
        attn = _attn_sample(to_bt(q), k_bt, v_bt,
                            feat_major(cache_win_k[l]), feat_major(cache_win_v[l]))
        attn = attn.transpose(1, 0, 2).reshape(ns, ATTN_W)
        outs["ks"].append(k_bt.reshape(bs, t, N_HEADS, HEAD_DIM))
        outs["vs"].append(v_bt.reshape(bs, t, N_HEADS, HEAD_DIM))
        outs["cs"].append(to_bt(z)[:, t - (CONV_K - 1):])
        state = state_conv[l].transpose(1, 0, 2).reshape(1, (CONV_K - 1) * bs, CONV_W)
        y_s = tail(hs, attn, cb, z, state, bs, 1, ns)
        hp, hs = y_p, y_s

    y_prompt = y_p.reshape(bp, s, d)
    y_sample = y_s.reshape(t, bs, d).transpose(1, 0, 2)
    st = lambda name: jnp.stack(outs[name])
    return (y_prompt, y_sample, st("kp"), st("vp"), st("cp"), st("ks"), st("vs"), st("cs"))
```
